```python
import math
import jax
import jax.numpy as jnp
from jax import lax
import numpy as np

D_MODEL = 1024
BATCH = 8
SEQ = 4096
DEPTH = 4

GRID_W = 64
CTX_LEN = 256
EPS = 1e-6

F_GROUPS = 4
F_GROUP_W = D_MODEL // 8
F_W = F_GROUPS * F_GROUP_W

ATT_HEADS = 4
ATT_QK = D_MODEL // 16
ATT_V = 2 * ATT_QK
ATT_QW = ATT_HEADS * 2 * ATT_QK
ATT_VW = ATT_HEADS * ATT_V
ROPE_AXIS = ATT_QK // 2
ROPE_BASE = 10000.0
Q_BLOCK = 128

D_INNER = D_MODEL // 2
SSD_P = 64
SSD_HEADS = D_INNER // SSD_P
SSD_GROUPS = 2
D_STATE = 128
CONV_W = 5
CHUNK = 128
XBC_W = D_INNER + 2 * SSD_GROUPS * D_STATE
DT_W = 2 * SSD_HEADS

N_BRANCH = 3
MERGE_W = N_BRANCH * D_MODEL
IN_SPLITS = (F_W, F_W, ATT_QW, ATT_QW, ATT_VW, ATT_VW, D_INNER, XBC_W, DT_W, MERGE_W)
IN_W = 2 * F_W + 2 * ATT_QW + 2 * ATT_VW + D_INNER + XBC_W + DT_W + MERGE_W

kernel_name = 'hybrid_fourier_diffattn_ssd_dit'


def rmsnorm(x, w):
    xf = x.astype(jnp.float32)
    y = xf * lax.rsqrt(jnp.mean(xf * xf, axis=-1, keepdims=True) + EPS)
    return (y * w.astype(jnp.float32)).astype(x.dtype)


def axial_rope_tables(n, dtype):
    rows = n // GRID_W
    row = jnp.repeat(jnp.arange(rows, dtype=jnp.float32), GRID_W)
    col = jnp.tile(jnp.arange(GRID_W, dtype=jnp.float32), rows)
    freqs = ROPE_BASE ** (-jnp.arange(0, ROPE_AXIS, 2, dtype=jnp.float32) / ROPE_AXIS)
    ang_r = row[:, None] * freqs
    ang_c = col[:, None] * freqs
    ang = jnp.concatenate([ang_r, ang_r, ang_c, ang_c], axis=-1)
    return jnp.cos(ang).astype(dtype), jnp.sin(ang).astype(dtype)


def apply_rope(t, cos, sin):
    tt = t.reshape(t.shape[:-1] + (2, 2, ROPE_AXIS // 2))
    rot = jnp.concatenate([-tt[..., 1:, :], tt[..., :1, :]], axis=-2).reshape(t.shape)
    return t * cos[None, :, None, None, :] + rot * sin[None, :, None, None, :]


def diff_attention(q, k, v, lam, lam_init, subln_w):
    s = jnp.einsum('bqhmd,bkhmd->bhmqk', q, k).astype(jnp.float32) * (ATT_QK ** -0.5)
    p = jax.nn.softmax(s, axis=-1)
    a = p[:, :, 0] - lam * p[:, :, 1]
    o = jnp.einsum('bhqk,bkhe->bqhe', a.astype(v.dtype), v)
    return rmsnorm(o, subln_w) * (1.0 - lam_init)


def fourier_mix(u):
    b, n, _ = u.shape
    uf = u.astype(jnp.float32).reshape(b, n, F_GROUPS, F_GROUP_W)
    y = jnp.fft.fftn(uf, axes=(1, 3), norm='ortho').real
    return y.reshape(b, n, F_W).astype(u.dtype)


def depthwise_conv(x, w, bias):
    y = lax.conv_general_dilated(x, w[:, None, :], window_strides=(1,),
                                 padding=[(CONV_W // 2, CONV_W // 2)],
                                 dimension_numbers=('NWC', 'WIO', 'NWC'),
                                 feature_group_count=x.shape[-1])
    return y + bias


def ssd_scan(x, dt, a, bm, cm, init):
    f32 = jnp.float32
    b, L, h, p = x.shape
    g, n = bm.shape[2], bm.shape[3]
    r = h // g
    c = L // CHUNK
    xd = (x.astype(f32) * dt[..., None]).reshape(b, c, CHUNK, g, r, p)
    ad = (dt * a).reshape(b, c, CHUNK, g, r).transpose(0, 3, 4, 1, 2)
    a_cs = jnp.cumsum(ad, axis=-1)
    bc = bm.astype(f32).reshape(b, c, CHUNK, g, n)
    cc = cm.astype(f32).reshape(b, c, CHUNK, g, n)
    tri = jnp.tril(jnp.ones((CHUNK, CHUNK), dtype=bool))
    diff = a_cs[..., :, None] - a_cs[..., None, :]
    lmat = jnp.exp(jnp.where(tri, diff, -jnp.inf))
    cb = jnp.einsum('bclgn,bcsgn->bcgls', cc, bc)
    y_diag = jnp.einsum('bcgls,bgrcls,bcsgrp->bclgrp', cb, lmat, xd)
    decay_states = jnp.exp(a_cs[..., -1:] - a_cs)
    states = jnp.einsum('bclgn,bgrcl,bclgrp->bcgrpn', bc, decay_states, xd)
    states = jnp.concatenate([init.reshape(b, g, r, p, n)[:, None], states], axis=1)
    chunk_a = jnp.pad(a_cs[..., -1], [(0, 0), (0, 0), (0, 0), (1, 0)])
    cs2 = jnp.cumsum(chunk_a, axis=-1)
    tri2 = jnp.tril(jnp.ones((c + 1, c + 1), dtype=bool))
    decay_chunk = jnp.exp(jnp.where(tri2, cs2[..., :, None] - cs2[..., None, :], -jnp.inf))
    new_states = jnp.einsum('bgrzk,bkgrpn->bzgrpn', decay_chunk, states)
    prev_states = new_states[:, :-1]
    final = new_states[:, -1]
    y_off = jnp.einsum('bclgn,bcgrpn,bgrcl->bclgrp', cc, prev_states, jnp.exp(a_cs))
    y = (y_diag + y_off).reshape(b, L, h, p)
    return y, final.reshape(b, h, p, n)


def ssd_branch(xbc, dt_raw, conv_w, conv_b, a_log, dt_bias, d_skip, init_f, init_b):
    b, n, _ = xbc.shape
    f32 = jnp.float32
    xbc = jax.nn.silu(depthwise_conv(xbc, conv_w, conv_b))
    xs, bm, cm = jnp.split(xbc, [D_INNER, D_INNER + SSD_GROUPS * D_STATE], axis=-1)
    xs = xs.reshape(b, n, SSD_HEADS, SSD_P)
    bm = bm.reshape(b, n, SSD_GROUPS, D_STATE)
    cm = cm.reshape(b, n, SSD_GROUPS, D_STATE)
    dt = jax.nn.softplus(dt_raw.astype(f32).reshape(b, n, 2, SSD_HEADS) + dt_bias.astype(f32))
    a = -jnp.exp(a_log.astype(f32))
    y_f, s_f = ssd_scan(xs, dt[:, :, 0], a[0], bm, cm, init_f)
    rev = lambda t: jnp.flip(t, axis=1)
    y_b, s_b = ssd_scan(rev(xs), rev(dt[:, :, 1]), a[1], rev(bm), rev(cm), init_b)
    y = y_f + rev(y_b) + d_skip.astype(f32)[:, None] * xs.astype(f32)
    return y.reshape(b, n, D_INNER), s_f, s_b


def branch_merge(f_u, f_g, att_o, a_g, ssd_y, z, gates, w_of, w_oa, w_os, w_out, ssd_norm_w):
    b, n, _ = f_u.shape
    y_f = (fourier_mix(f_u) * jax.nn.silu(f_g)) @ w_of
    y_a = (att_o.reshape(b, n, ATT_VW) * jax.nn.silu(a_g)) @ w_oa
    y_s = rmsnorm(ssd_y * jax.nn.silu(z.astype(jnp.float32)), ssd_norm_w).astype(z.dtype) @ w_os
    g = jax.nn.sigmoid(gates.astype(jnp.float32)).astype(f_u.dtype).reshape(b, n, N_BRANCH, D_MODEL)
    y = g[:, :, 0] * y_f + g[:, :, 1] * y_a + g[:, :, 2] * y_s
    return y @ w_out


def setup_inputs(seed: int = 0) -> dict:
    key = jax.random.key(seed)
    ks = jax.random.split(key, 24)
    f32 = jnp.float32
    nrm = lambda k, shape, scale: jax.random.normal(k, shape, f32) * scale
    x = nrm(ks[0], (BATCH, SEQ, D_MODEL), 1.0)
    c = nrm(ks[1], (BATCH, D_MODEL), 1.0)
    ctx = nrm(ks[2], (BATCH, CTX_LEN, D_MODEL), 1.0)
    c_ctx = nrm(ks[3], (D_MODEL,), 1.0)
    w_mod = nrm(ks[4], (DEPTH, D_MODEL, 3 * D_MODEL), 0.5 * D_MODEL ** -0.5)
    b_mod = nrm(ks[5], (DEPTH, 3 * D_MODEL), 0.02)
    norm_w = 1.0 + nrm(ks[6], (DEPTH, D_MODEL), 0.02)
    w_in = nrm(ks[7], (DEPTH, D_MODEL, IN_W), D_MODEL ** -0.5)
    conv_w = nrm(ks[8], (DEPTH, CONV_W, XBC_W), CONV_W ** -0.5)
    conv_b = nrm(ks[9], (DEPTH, XBC_W), 0.02)
    a_log = jnp.log(jax.random.uniform(ks[10], (DEPTH, 2, SSD_HEADS), f32, 1.0, 16.0))
    dt0 = jnp.exp(jax.random.uniform(ks[11], (DEPTH, 2, SSD_HEADS), f32,
                                     math.log(1e-3), math.log(1e-1)))
    dt_bias = dt0 + jnp.log(-jnp.expm1(-dt0))
    d_skip = 1.0 + nrm(ks[12], (DEPTH, SSD_HEADS), 0.1)
    ssd_norm_w = 1.0 + nrm(ks[13], (DEPTH, D_INNER), 0.02)
    lam = nrm(ks[14], (DEPTH, 4, ATT_QK), 0.1)
    subln_w = 1.0 + nrm(ks[15], (DEPTH, ATT_V), 0.02)
    w_of = nrm(ks[16], (DEPTH, F_W, D_MODEL), F_W ** -0.5)
    w_oa = nrm(ks[17], (DEPTH, ATT_VW, D_MODEL), ATT_VW ** -0.5)
    w_os = nrm(ks[18], (DEPTH, D_INNER, D_MODEL), D_INNER ** -0.5)
    w_out = nrm(ks[19], (DEPTH, D_MODEL, D_MODEL), D_MODEL ** -0.5)
    norm_f = 1.0 + nrm(ks[20], (D_MODEL,), 0.02)
    return {'x': x, 'c': c, 'ctx': ctx, 'c_ctx': c_ctx, 'w_mod': w_mod, 'b_mod': b_mod,
            'norm_w': norm_w, 'w_in': w_in, 'conv_w': conv_w, 'conv_b': conv_b,
            'a_log': a_log, 'dt_bias': dt_bias, 'd_skip': d_skip, 'ssd_norm_w': ssd_norm_w,
            'lam': lam, 'subln_w': subln_w, 'w_of': w_of, 'w_oa': w_oa, 'w_os': w_os,
            'w_out': w_out, 'norm_f': norm_f}


def reference(x, c, ctx, c_ctx, w_mod, b_mod, norm_w, w_in, conv_w, conv_b, a_log, dt_bias,
              d_skip, ssd_norm_w, lam, subln_w, w_of, w_oa, w_os, w_out, norm_f):
    f32 = jnp.float32
    b, n, _ = x.shape
    n_ctx = ctx.shape[1]
    n_blocks = n // Q_BLOCK
    cos, sin = axial_rope_tables(n, x.dtype)
    split_at = [int(v) for v in np.cumsum(IN_SPLITS)[:-1]]
    zero_state = jnp.zeros((b, SSD_HEADS, SSD_P, D_STATE), f32)
    xc = ctx
    for l in range(DEPTH):
        lam_init = 0.8 - 0.6 * math.exp(-0.3 * l)
        lp = lam[l].astype(f32)
        lam_l = jnp.exp(jnp.sum(lp[0] * lp[1])) - jnp.exp(jnp.sum(lp[2] * lp[3])) + lam_init
        mod_l = jax.nn.silu(c) @ w_mod[l] + b_mod[l]
        mod_c = jax.nn.silu(c_ctx) @ w_mod[l] + b_mod[l]
        sh_l, sc_l, g_l = jnp.split(mod_l[:, None, :], 3, axis=-1)
        sh_c, sc_c, g_c = jnp.split(mod_c, 3, axis=-1)
        h_l = rmsnorm(x, norm_w[l]) * (1.0 + sc_l) + sh_l
        h_c = rmsnorm(xc, norm_w[l]) * (1.0 + sc_c) + sh_c
        (fu_c, fg_c, q_c, k_c, v_c, ag_c, z_c, xbc_c, dt_c, gt_c) = jnp.split(h_c @ w_in[l], split_at, axis=-1)
        (fu_l, fg_l, q_l, k_l, v_l, ag_l, z_l, xbc_l, dt_l, gt_l) = jnp.split(h_l @ w_in[l], split_at, axis=-1)
        q_c = q_c.reshape(b, n_ctx, ATT_HEADS, 2, ATT_QK)
        k_c = k_c.reshape(b, n_ctx, ATT_HEADS, 2, ATT_QK)
        v_c = v_c.reshape(b, n_ctx, ATT_HEADS, ATT_V)
        q_l = apply_rope(q_l.reshape(b, n, ATT_HEADS, 2, ATT_QK), cos, sin)
        k_l = apply_rope(k_l.reshape(b, n, ATT_HEADS, 2, ATT_QK), cos, sin)
        v_l = v_l.reshape(b, n, ATT_HEADS, ATT_V)
        k_all = jnp.concatenate([k_c, k_l], axis=1)
        v_all = jnp.concatenate([v_c, v_l], axis=1)
        q_blocks = jnp.moveaxis(q_l.reshape(b, n_blocks, Q_BLOCK, ATT_HEADS, 2, ATT_QK), 1, 0)
        att_l = lax.map(lambda qb: diff_attention(qb, k_all, v_all, lam_l, lam_init, subln_w[l]), q_blocks)
        att_l = jnp.moveaxis(att_l, 0, 1).reshape(b, n, ATT_HEADS, ATT_V)
        ys_c, sf_c, sb_c = ssd_branch(xbc_c, dt_c, conv_w[l], conv_b[l], a_log[l], dt_bias[l],
                                      d_skip[l], zero_state, zero_state)
        ys_l, _, _ = ssd_branch(xbc_l, dt_l, conv_w[l], conv_b[l], a_log[l], dt_bias[l],
                                d_skip[l], sf_c, sb_c)
        out_l = branch_merge(fu_l, fg_l, att_l, ag_l, ys_l, z_l, gt_l,
                             w_of[l], w_oa[l], w_os[l], w_out[l], ssd_norm_w[l])
        if l < DEPTH - 1:
            att_c = diff_attention(q_c, k_c, v_c, lam_l, lam_init, subln_w[l])
            out_c = branch_merge(fu_c, fg_c, att_c, ag_c, ys_c, z_c, gt_c,
                                 w_of[l], w_oa[l], w_os[l], w_out[l], ssd_norm_w[l])
            xc = xc + g_c * out_c
        x = x + g_l * out_l
    return rmsnorm(x, norm_f)
```

```python
import functools
import math

import numpy as np
import jax
import jax.numpy as jnp
from jax import lax
from jax.experimental import pallas as pl
from jax.experimental.pallas import tpu as pltpu

F32 = jnp.float32
BF16 = jnp.bfloat16
EPS = 1e-6

D_MODEL = 1024
GRID_W = 64
F_GROUP_W = 128
F_W = 512
ATT_HEADS = 4
ATT_QK = 64
ATT_V = 128
ROPE_AXIS = 32
ROPE_BASE = 10000.0
D_INNER = 512
SSD_P = 64
SSD_HEADS = 8
SSD_GROUPS = 2
D_STATE = 128
CONV_W = 5
CHUNK = 128
XBC_W = 1024
DT_W = 16
MERGE_W = 3072

LANE = 128
SUBLANE = 8
VMEM_LIMIT = 56 * 1024 * 1024

CB = 512
COL_XBC, COL_GATES, COL_FU, COL_FG, COL_Q, COL_K, COL_V, COL_AG, COL_Z = 0, 2, 8, 9, 10, 11, 12, 13, 14
N_COLB = 15
N_MAIN = N_COLB * CB


def _dot(a, b):
    return jnp.dot(a, b, preferred_element_type=F32)


def _dot_nt(a, b):
    return lax.dot_general(a, b, (((1,), (1,)), ((), ())), preferred_element_type=F32)


def _split3(x):
    x1 = x.astype(BF16)
    r1 = x - x1.astype(F32)
    x2 = r1.astype(BF16)
    x3 = (r1 - x2.astype(F32)).astype(BF16)
    return x1, x2, x3


def _dot_sel_r(x, sel):
    x1, x2, x3 = _split3(x)
    return _dot(x1, sel) + _dot(x2, sel) + _dot(x3, sel)


def _dot_sel_l(sel, x):
    x1, x2, x3 = _split3(x)
    return _dot(sel, x1) + _dot(sel, x2) + _dot(sel, x3)


def _silu(x):
    return x * jax.nn.sigmoid(x)


def _params(*sem):
    return pltpu.CompilerParams(dimension_semantics=sem, vmem_limit_bytes=VMEM_LIMIT)


def _mod_kernel(cc_ref, w_ref, b_ref, lam_ref, linit_ref, mod_ref, lam_out_ref):
    s = _silu(cc_ref[...])
    mod_ref[0] = jnp.dot(s, w_ref[0], precision=lax.Precision.HIGHEST,
                         preferred_element_type=F32) + b_ref[0]
    lp = lam_ref[0]
    s1 = jnp.sum(lp[0:1] * lp[1:2], axis=-1, keepdims=True)
    s2 = jnp.sum(lp[2:3] * lp[3:4], axis=-1, keepdims=True)
    lam_out_ref[0] = jnp.broadcast_to(jnp.exp(s1) - jnp.exp(s2), (SUBLANE, LANE)) + linit_ref[0]


def _modulation(cc, w_mod, b_mod, lam, linit):
    depth = w_mod.shape[0]
    rows = cc.shape[0]
    tn = D_MODEL
    return pl.pallas_call(
        _mod_kernel,
        grid=(depth, 3 * D_MODEL // tn),
        in_specs=[
            pl.BlockSpec((rows, D_MODEL), lambda l, j: (0, 0)),
            pl.BlockSpec((1, D_MODEL, tn), lambda l, j: (l, 0, j)),
            pl.BlockSpec((1, 1, tn), lambda l, j: (l, 0, j)),
            pl.BlockSpec((1, 4, ATT_QK), lambda l, j: (l, 0, 0)),
            pl.BlockSpec((1, SUBLANE, LANE), lambda l, j: (l, 0, 0)),
        ],
        out_specs=[
            pl.BlockSpec((1, rows, tn), lambda l, j: (l, 0, j)),
            pl.BlockSpec((1, SUBLANE, LANE), lambda l, j: (l, 0, 0)),
        ],
        out_shape=[
            jax.ShapeDtypeStruct((depth, rows, 3 * D_MODEL), F32),
            jax.ShapeDtypeStruct((depth, SUBLANE, LANE), F32),
        ],
        compiler_params=_params("arbitrary", "arbitrary"),
    )(cc, w_mod, b_mod, lam, linit)


def _rope(t, cos, sin):
    w = t.shape[-1]
    lane = lax.broadcasted_iota(jnp.int32, t.shape, 1)
    first = (lane % ROPE_AXIS) < (ROPE_AXIS // 2)
    rot = jnp.where(first, -pltpu.roll(t, w - ROPE_AXIS // 2, 1), pltpu.roll(t, ROPE_AXIS // 2, 1))
    return t * cos + rot * sin


def _inproj_kernel(*refs, rope):
    if rope:
        x_ref, mod_ref, nw_ref, w_ref, wdt_ref, cos_ref, sin_ref, p_ref, dt_ref, h_ref = refs
    else:
        x_ref, mod_ref, nw_ref, w_ref, wdt_ref, p_ref, dt_ref, h_ref = refs
    j = pl.program_id(1)

    @pl.when(j == 0)
    def _():
        x = x_ref[...]
        y = x * lax.rsqrt(jnp.mean(x * x, axis=-1, keepdims=True) + EPS) * nw_ref[...]
        m = mod_ref[0]
        h = y * (1.0 + m[:, D_MODEL:2 * D_MODEL]) + m[:, 0:D_MODEL]
        hb = h.astype(BF16)
        h_ref[...] = hb
        dt_ref[...] = _dot(hb, wdt_ref[...])

    acc = _dot(h_ref[...], w_ref[...])
    if rope:
        is_qk = (j == COL_Q) | (j == COL_K)

        @pl.when(is_qk)
        def _():
            p_ref[...] = _rope(acc, cos_ref[...], sin_ref[...])

        @pl.when(jnp.logical_not(is_qk))
        def _():
            p_ref[...] = acc
    else:
        p_ref[...] = acc


def _inproj(x2, mod, norm_w, w_main, w_dt, cos, sin, n, tm):
    rows = x2.shape[0]
    tiles_per_b = n // tm
    rope = cos is not None
    in_specs = [
        pl.BlockSpec((tm, D_MODEL), lambda i, j: (i, 0)),
        pl.BlockSpec((1, 1, 3 * D_MODEL), lambda i, j: (i // tiles_per_b, 0, 0)),
        pl.BlockSpec((1, D_MODEL), lambda i, j: (0, 0)),
        pl.BlockSpec((D_MODEL, CB), lambda i, j: (0, j)),
        pl.BlockSpec((D_MODEL, LANE), lambda i, j: (0, 0)),
    ]
    args = [x2, mod, norm_w, w_main, w_dt]
    if rope:
        in_specs += [pl.BlockSpec((tm, CB), lambda i, j: (i % tiles_per_b, 0))] * 2
        args += [cos, sin]
    return pl.pallas_call(
        functools.partial(_inproj_kernel, rope=rope),
        grid=(rows // tm, N_COLB),
        in_specs=in_specs,
        out_specs=[
            pl.BlockSpec((tm, CB), lambda i, j: (i, j)),
            pl.BlockSpec((tm, LANE), lambda i, j: (i, 0)),
        ],
        out_shape=[
            jax.ShapeDtypeStruct((rows, N_MAIN), F32),
            jax.ShapeDtypeStruct((rows, LANE), F32),
        ],
        scratch_shapes=[pltpu.VMEM((tm, D_MODEL), BF16)],
        compiler_params=_params("parallel", "arbitrary"),
    )(*args)


HALO = SUBLANE


def _conv_kernel(cur_ref, prev_ref, next_ref, w_ref, b_ref, o_ref, ext_ref, *, tiles_per_b):
    i = pl.program_id(0)
    tc = cur_ref.shape[0]
    t = i % tiles_per_b
    ext_ref[0:HALO, :] = jnp.where(t == 0, 0.0, prev_ref[...])
    ext_ref[HALO:HALO + tc, :] = cur_ref[...]
    ext_ref[HALO + tc:2 * HALO + tc, :] = jnp.where(t == tiles_per_b - 1, 0.0, next_ref[...])
    acc = jnp.broadcast_to(b_ref[...], o_ref.shape)
    for k in range(CONV_W):
        acc = acc + ext_ref[pl.ds(HALO - CONV_W // 2 + k, tc), :] * w_ref[k:k + 1, :]
    o_ref[...] = _silu(acc)


def _conv_silu(p, conv_w8, conv_b, n, tc):
    rows = p.shape[0]
    tiles_per_b = n // tc
    hb = tc // HALO
    last_hb = rows // HALO - 1
    return pl.pallas_call(
        functools.partial(_conv_kernel, tiles_per_b=tiles_per_b),
        grid=(rows // tc, XBC_W // CB),
        in_specs=[
            pl.BlockSpec((tc, CB), lambda i, j: (i, COL_XBC + j)),
            pl.BlockSpec((HALO, CB), lambda i, j: (jnp.maximum(i * hb - 1, 0), COL_XBC + j)),
            pl.BlockSpec((HALO, CB), lambda i, j: (jnp.minimum((i + 1) * hb, last_hb), COL_XBC + j)),
            pl.BlockSpec((SUBLANE, CB), lambda i, j: (0, j)),
            pl.BlockSpec((1, CB), lambda i, j: (0, j)),
        ],
        out_specs=pl.BlockSpec((tc, CB), lambda i, j: (i, j)),
        out_shape=jax.ShapeDtypeStruct((rows, XBC_W), F32),
        scratch_shapes=[pltpu.VMEM((tc + 2 * HALO, CB), F32)],
        compiler_params=_params("parallel", "parallel"),
    )(p, p, p, conv_w8, conv_b)


def _attn_kernel(*refs, with_latent, one_minus_lam_init):
    if with_latent:
        (q_ref, kc_ref, vc_ref, kl_ref, vl_ref, lam_ref, sw_ref, o_ref,
         m_ref, l_ref, acc_ref) = refs
    else:
        q_ref, kc_ref, vc_ref, lam_ref, sw_ref, o_ref, m_ref, l_ref, acc_ref = refs
    j = pl.program_id(3)
    nj = pl.num_programs(3)

    @pl.when(j == 0)
    def _():
        m_ref[...] = jnp.full(m_ref.shape, -jnp.inf, F32)
        l_ref[...] = jnp.zeros(l_ref.shape, F32)
        acc_ref[...] = jnp.zeros(acc_ref.shape, F32)

    def step(k_ref, v_ref):
        qb = (q_ref[...] * (ATT_QK ** -0.5)).astype(BF16)
        kb = k_ref[...].astype(BF16)
        vb = v_ref[...].astype(BF16)
        for mi in range(2):
            sl = slice(mi * ATT_QK, (mi + 1) * ATT_QK)
            s = _dot_nt(qb[:, sl], kb[:, sl])
            m_prev = m_ref[mi]
            m_new = jnp.maximum(m_prev, jnp.max(s, axis=-1, keepdims=True))
            alpha = jnp.exp(m_prev - m_new)
            p = jnp.exp(s - m_new)
            l_ref[mi] = alpha * l_ref[mi] + jnp.sum(p, axis=-1, keepdims=True)
            acc_ref[mi] = alpha * acc_ref[mi] + _dot(p.astype(BF16), vb)
            m_ref[mi] = m_new

    if with_latent:
        @pl.when(j == 0)
        def _():
            step(kc_ref, vc_ref)

        @pl.when(j > 0)
        def _():
            step(kl_ref, vl_ref)
    else:
        step(kc_ref, vc_ref)

    @pl.when(j == nj - 1)
    def _():
        lam = lam_ref[0:1, :]
        o = acc_ref[0] / l_ref[0] - lam * (acc_ref[1] / l_ref[1])
        y = o * lax.rsqrt(jnp.mean(o * o, axis=-1, keepdims=True) + EPS) * sw_ref[...]
        o_ref[...] = y * one_minus_lam_init


def _attention(p_q, p_c, p_l, lam_l, subln_w, lam_init, batch, nq_len, nc_len, nl_len, tq, tk):
    with_latent = p_l is not None
    nq = nq_len // tq
    hq, hk, hv = (c * (CB // LANE) for c in (COL_Q, COL_K, COL_V))
    in_specs = [
        pl.BlockSpec((tq, LANE), lambda b, h, i, j: (b * nq + i, hq + h)),
        pl.BlockSpec((nc_len, LANE), lambda b, h, i, j: (b, hk + h)),
        pl.BlockSpec((nc_len, LANE), lambda b, h, i, j: (b, hv + h)),
    ]
    args = [p_q, p_c, p_c]
    nkv = 1
    if with_latent:
        nk = nl_len // tk
        nkv = 1 + nk
        in_specs += [
            pl.BlockSpec((tk, LANE), lambda b, h, i, j: (b * nk + jnp.maximum(j - 1, 0), hk + h)),
            pl.BlockSpec((tk, LANE), lambda b, h, i, j: (b * nk + jnp.maximum(j - 1, 0), hv + h)),
        ]
        args += [p_l, p_l]
    in_specs += [
        pl.BlockSpec((SUBLANE, LANE), lambda b, h, i, j: (0, 0)),
        pl.BlockSpec((1, ATT_V), lambda b, h, i, j: (0, 0)),
    ]
    args += [lam_l, subln_w]
    return pl.pallas_call(
        functools.partial(_attn_kernel, with_latent=with_latent,
                          one_minus_lam_init=1.0 - lam_init),
        grid=(batch, ATT_HEADS, nq, nkv),
        in_specs=in_specs,
        out_specs=pl.BlockSpec((tq, ATT_V), lambda b, h, i, j: (b * nq + i, h)),
        out_shape=jax.ShapeDtypeStruct((batch * nq_len, ATT_HEADS * ATT_V), F32),
        scratch_shapes=[
            pltpu.VMEM((2, tq, 1), F32),
            pltpu.VMEM((2, tq, 1), F32),
            pltpu.VMEM((2, tq, ATT_V), F32),
        ],
        compiler_params=_params("parallel", "parallel", "parallel", "arbitrary"),
    )(*args)


def _fourier_kernel(x_ref, cn_ref, sn_ref, cc_ref, sc_ref, o_ref, *, scale):
    k = pl.program_id(1)
    xb = x_ref[...].astype(BF16)
    xc = _dot(xb, cc_ref[...]).astype(BF16)
    xs = _dot(xb, sc_ref[...]).astype(BF16)
    part = _dot(cn_ref[...], xc) - _dot(sn_ref[...], xs)

    @pl.when(k == 0)
    def _():
        o_ref[...] = part

    @pl.when(k > 0)
    def _():
        o_ref[...] += part

    @pl.when(k == pl.num_programs(1) - 1)
    def _():
        o_ref[...] *= scale


def _dft_tables(n):
    idx = jnp.arange(n, dtype=jnp.int32)
    ang = ((idx[:, None] * idx[None, :]) % n).astype(F32) * (2.0 * math.pi / n)
    return jnp.cos(ang), jnp.sin(ang)


def _fourier(p, cn, sn, ccd, scd, batch, n, tk):
    nk = n // tk
    scale = 1.0 / math.sqrt(n * F_GROUP_W)
    return pl.pallas_call(
        functools.partial(_fourier_kernel, scale=scale),
        grid=(batch, nk),
        in_specs=[
            pl.BlockSpec((tk, CB), lambda b, k: (b * nk + k, COL_FU)),
            pl.BlockSpec((n, tk), lambda b, k: (0, k)),
            pl.BlockSpec((n, tk), lambda b, k: (0, k)),
            pl.BlockSpec((F_W, F_W), lambda b, k: (0, 0)),
            pl.BlockSpec((F_W, F_W), lambda b, k: (0, 0)),
        ],
        out_specs=pl.BlockSpec((n, F_W), lambda b, k: (b, 0)),
        out_shape=jax.ShapeDtypeStruct((batch * n, F_W), F32),
        compiler_params=_params("parallel", "arbitrary"),
    )(p, cn, sn, ccd, scd)


def _softplus(x):
    return jnp.maximum(x, 0.0) + jnp.log1p(jnp.exp(-jnp.abs(x)))


def _ssd_kernel(xsf_ref, bcf_ref, dtf_ref, xsb_ref, bcb_ref, dtb_ref, alog_ref, dtbias_ref,
                e_ref, et_ref, init_ref, yf_ref, yb_ref, fin_ref, state_ref):
    c = pl.program_id(1)
    q = xsf_ref.shape[0]
    gw = D_INNER // SSD_GROUPS
    hpg = SSD_HEADS // SSD_GROUPS

    @pl.when(c == 0)
    def _():
        state_ref[...] = init_ref[0]

    row = lax.broadcasted_iota(jnp.int32, (q, q), 0)
    col = lax.broadcasted_iota(jnp.int32, (q, q), 1)
    head_of_lane = lax.broadcasted_iota(jnp.int32, (q, gw), 1) // SSD_P
    neg_a = -jnp.exp(alog_ref[...])

    dirs = ((xsf_ref, bcf_ref, dtf_ref, yf_ref), (xsb_ref, bcb_ref, dtb_ref, yb_ref))
    for d, (xs_ref, bc_ref, dt_ref, y_ref) in enumerate(dirs):
        fwd = d == 0
        tri = (row >= col) if fwd else (row <= col)
        tri_b = tri.astype(F32).astype(BF16)
        dt = _softplus(dt_ref[...] + dtbias_ref[...])
        acs = _dot_sel_l(tri_b, dt * neg_a)
        acs_t = acs.T
        last = q - 1 if fwd else 0
        tot_row = acs[last:last + 1, :]
        tot_col = acs_t[:, last:last + 1]
        e = e_ref[d]
        dt_w = _dot_sel_r(dt, e)
        eacs_w = _dot_sel_r(jnp.exp(acs), e)
        dec_w = _dot_sel_r(jnp.exp(tot_row - acs), e)
        sdec = _dot_sel_l(et_ref[d], jnp.broadcast_to(jnp.exp(tot_col), (LANE, D_STATE)))
        xd = xs_ref[...] * dt_w
        xdb = xd.astype(BF16)
        xdd_t = (xd * dec_w).T.astype(BF16)
        bc = bc_ref[...]
        for g in range(SSD_GROUPS):
            bg = bc[:, g * D_STATE:(g + 1) * D_STATE].astype(BF16)
            cg = bc[:, (SSD_GROUPS + g) * D_STATE:(SSD_GROUPS + g + 1) * D_STATE].astype(BF16)
            cb = _dot_nt(cg, bg)
            lanes = slice(g * gw, (g + 1) * gw)
            s_g = state_ref[d, lanes, :]
            y = _dot_nt(cg, s_g.astype(BF16)) * eacs_w[:, lanes]
            xg = xdb[:, lanes]
            for r in range(hpg):
                jl = d * SSD_HEADS + g * hpg + r
                seg = jnp.where(tri, acs[:, jl:jl + 1] - acs_t[jl:jl + 1, :], -jnp.inf)
                mat = (cb * jnp.exp(seg)).astype(BF16)
                y = y + _dot(mat, jnp.where(head_of_lane == r, xg, jnp.zeros_like(xg)))
            y_ref[:, lanes] = y
            state_ref[d, lanes, :] = s_g * sdec[lanes, :] + _dot(xdd_t[lanes, :], bg)

    @pl.when(c == pl.num_programs(1) - 1)
    def _():
        fin_ref[0] = state_ref[...]


def _ssd(act, dt, alog, dtbias, emat, emat_t, init, batch, n):
    nc = n // CHUNK
    fidx = lambda b, c: b * nc + c
    bidx = lambda b, c: b * nc + (nc - 1 - c)
    st_shape = (2, D_INNER, D_STATE)
    return pl.pallas_call(
        _ssd_kernel,
        grid=(batch, nc),
        in_specs=[
            pl.BlockSpec((CHUNK, CB), lambda b, c: (fidx(b, c), 0)),
            pl.BlockSpec((CHUNK, CB), lambda b, c: (fidx(b, c), 1)),
            pl.BlockSpec((CHUNK, LANE), lambda b, c: (fidx(b, c), 0)),
            pl.BlockSpec((CHUNK, CB), lambda b, c: (bidx(b, c), 0)),
            pl.BlockSpec((CHUNK, CB), lambda b, c: (bidx(b, c), 1)),
            pl.BlockSpec((CHUNK, LANE), lambda b, c: (bidx(b, c), 0)),
            pl.BlockSpec((1, LANE), lambda b, c: (0, 0)),
            pl.BlockSpec((1, LANE), lambda b, c: (0, 0)),
            pl.BlockSpec((2, LANE, D_INNER), lambda b, c: (0, 0, 0)),
            pl.BlockSpec((2, D_INNER, LANE), lambda b, c: (0, 0, 0)),
            pl.BlockSpec((1,) + st_shape, lambda b, c: (b, 0, 0, 0)),
        ],
        out_specs=[
            pl.BlockSpec((CHUNK, D_INNER), lambda b, c: (fidx(b, c), 0)),
            pl.BlockSpec((CHUNK, D_INNER), lambda b, c: (bidx(b, c), 0)),
            pl.BlockSpec((1,) + st_shape, lambda b, c: (b, 0, 0, 0)),
        ],
        out_shape=[
            jax.ShapeDtypeStruct((batch * n, D_INNER), F32),
            jax.ShapeDtypeStruct((batch * n, D_INNER), F32),
            jax.ShapeDtypeStruct((batch,) + st_shape, F32),
        ],
        scratch_shapes=[pltpu.VMEM(st_shape, F32)],
        compiler_params=_params("parallel", "arbitrary"),
    )(act, act, dt, act, act, dt, alog, dtbias, emat, emat_t, init)


def _merge_kernel(four_ref, att_ref, yf_ref, yb_ref, xs_ref, fg_ref, ag_ref, z_ref,
                  g0_ref, g1_ref, g2_ref, x_ref, mod_ref, dskip_ref, snw_ref,
                  wof_ref, woa_ref, wos_ref, wout_ref, nf_ref, o_ref, *, final_norm):
    y_f = _dot((four_ref[...] * _silu(fg_ref[...])).astype(BF16), wof_ref[...])
    y_a = _dot((att_ref[...] * _silu(ag_ref[...])).astype(BF16), woa_ref[...])
    ys = yf_ref[...] + yb_ref[...] + dskip_ref[...] * xs_ref[...]
    t = ys * _silu(z_ref[...])
    t = t * lax.rsqrt(jnp.mean(t * t, axis=-1, keepdims=True) + EPS) * snw_ref[...]
    y_s = _dot(t.astype(BF16), wos_ref[...])
    y = (jax.nn.sigmoid(g0_ref[...]) * y_f + jax.nn.sigmoid(g1_ref[...]) * y_a
         + jax.nn.sigmoid(g2_ref[...]) * y_s)
    out = _dot(y.astype(BF16), wout_ref[...])
    xn = x_ref[...] + mod_ref[0][:, 2 * D_MODEL:3 * D_MODEL] * out
    if final_norm:
        xn = xn * lax.rsqrt(jnp.mean(xn * xn, axis=-1, keepdims=True) + EPS) * nf_ref[...]
    o_ref[...] = xn


def _merge(four, att, yf, yb, act, p, x2, mod, dskip_w, snw, wof, woa, wos, wout, norm_f,
           n, tm, final_norm):
    rows = x2.shape[0]
    tiles_per_b = n // tm
    gcol = COL_GATES * CB // D_MODEL
    row_blk = lambda w, cidx: pl.BlockSpec((tm, w), lambda i: (i, cidx))
    const = lambda shape: pl.BlockSpec(shape, lambda i: (0,) * len(shape))
    return pl.pallas_call(
        functools.partial(_merge_kernel, final_norm=final_norm),
        grid=(rows // tm,),
        in_specs=[
            row_blk(F_W, 0), row_blk(CB, 0), row_blk(D_INNER, 0), row_blk(D_INNER, 0),
            row_blk(CB, 0),
            row_blk(CB, COL_FG), row_blk(CB, COL_AG), row_blk(CB, COL_Z),
            row_blk(D_MODEL, gcol), row_blk(D_MODEL, gcol + 1), row_blk(D_MODEL, gcol + 2),
            row_blk(D_MODEL, 0),
            pl.BlockSpec((1, 1, 3 * D_MODEL), lambda i: (i // tiles_per_b, 0, 0)),
            const((1, D_INNER)), const((1, D_INNER)),
            const((F_W, D_MODEL)), const((CB, D_MODEL)), const((D_INNER, D_MODEL)),
            const((D_MODEL, D_MODEL)), const((1, D_MODEL)),
        ],
        out_specs=pl.BlockSpec((tm, D_MODEL), lambda i: (i, 0)),
        out_shape=jax.ShapeDtypeStruct((rows, D_MODEL), F32),
        compiler_params=_params("parallel"),
    )(four, att, yf, yb, act, p, p, p, p, p, p, x2, mod, dskip_w, snw, wof, woa, wos, wout, norm_f)


def _rope_tables(n):
    rows = n // GRID_W
    row = jnp.repeat(jnp.arange(rows, dtype=F32), GRID_W)
    col = jnp.tile(jnp.arange(GRID_W, dtype=F32), rows)
    freqs = ROPE_BASE ** (-jnp.arange(0, ROPE_AXIS, 2, dtype=F32) / ROPE_AXIS)
    ang_r = row[:, None] * freqs
    ang_c = col[:, None] * freqs
    ang = jnp.concatenate([ang_r, ang_r, ang_c, ang_c], axis=-1)
    reps = CB // ATT_QK
    return jnp.tile(jnp.cos(ang), (1, reps)), jnp.tile(jnp.sin(ang), (1, reps))


def _head_expanders():
    e = np.zeros((2, LANE, D_INNER), np.float32)
    for d in range(2):
        for h in range(SSD_HEADS):
            e[d, d * SSD_HEADS + h, h * SSD_P:(h + 1) * SSD_P] = 1.0
    return jnp.asarray(e, BF16), jnp.asarray(e.transpose(0, 2, 1), BF16)


def _group_dft(w):
    cw, sw = _dft_tables(F_GROUP_W)
    eye = jnp.eye(w // F_GROUP_W, dtype=F32)
    return jnp.kron(eye, cw).astype(BF16), jnp.kron(eye, sw).astype(BF16)


def kernel(x, c, ctx, c_ctx, w_mod, b_mod, norm_w, w_in, conv_w, conv_b, a_log, dt_bias, d_skip,
           ssd_norm_w, lam, subln_w, w_of, w_oa, w_os, w_out, norm_f):
    batch, n, _ = x.shape
    n_ctx = ctx.shape[1]
    depth = w_mod.shape[0]
    assert n % GRID_W == 0 and n % CHUNK == 0 and n_ctx % CHUNK == 0

    o_xbc = 2 * F_W + 2 * CB + 2 * CB + D_INNER
    o_dt = o_xbc + XBC_W
    o_gt = o_dt + DT_W
    w_main = jnp.concatenate([w_in[:, :, o_xbc:o_dt], w_in[:, :, o_gt:], w_in[:, :, :o_xbc]],
                             axis=-1).astype(BF16)
    w_dt = jnp.pad(w_in[:, :, o_dt:o_gt], ((0, 0), (0, 0), (0, LANE - DT_W))).astype(BF16)
    conv_w8 = jnp.pad(conv_w, ((0, 0), (0, SUBLANE - CONV_W), (0, 0)))
    pad_lanes = lambda a: jnp.pad(a.reshape(depth, 1, DT_W), ((0, 0), (0, 0), (0, LANE - DT_W)))
    alog_p, dtbias_p = pad_lanes(a_log), pad_lanes(dt_bias)
    dskip_w = jnp.repeat(d_skip, SSD_P, axis=-1).reshape(depth, 1, D_INNER)
    wof_b, woa_b, wos_b, wout_b = (w.astype(BF16) for w in (w_of, w_oa, w_os, w_out))
    lam_inits = [0.8 - 0.6 * math.exp(-0.3 * l) for l in range(depth)]
    linit = jnp.asarray(np.broadcast_to(np.asarray(lam_inits, np.float32)[:, None, None],
                                        (depth, SUBLANE, LANE)))

    cos_t, sin_t = _rope_tables(n)
    cn_l, sn_l = (t.astype(BF16) for t in _dft_tables(n))
    cn_c, sn_c = (t.astype(BF16) for t in _dft_tables(n_ctx))
    ccd, scd = _group_dft(F_W)
    emat, emat_t = _head_expanders()
    zero_state = jnp.zeros((batch, 2, D_INNER, D_STATE), F32)

    mod_rows = -(-(batch + 1) // SUBLANE) * SUBLANE
    cc = jnp.concatenate([c, c_ctx[None, :], jnp.zeros((mod_rows - batch - 1, D_MODEL), F32)], axis=0)
    mod_all, lam_all = _modulation(cc, w_mod, b_mod.reshape(depth, 1, 3 * D_MODEL), lam, linit)

    tm_l = min(n, 1024)
    tq_l = min(n, 512)
    tk_l = min(n, 512)
    tf_l = min(n, 512)
    tmerge_l = min(n, 256)
    xl = x.reshape(batch * n, D_MODEL)
    xc = ctx.reshape(batch * n_ctx, D_MODEL)
    for l in range(depth):
        last = l == depth - 1
        mod_l = mod_all[l, :batch].reshape(batch, 1, 3 * D_MODEL)
        mod_c = jnp.broadcast_to(mod_all[l, batch].reshape(1, 1, 3 * D_MODEL), (batch, 1, 3 * D_MODEL))
        nw = norm_w[l].reshape(1, D_MODEL)
        p_c, dt_c = _inproj(xc, mod_c, nw, w_main[l], w_dt[l], None, None, n_ctx, n_ctx)
        p_l, dt_l = _inproj(xl, mod_l, nw, w_main[l], w_dt[l], cos_t, sin_t, n, tm_l)
        cb_ = conv_b[l].reshape(1, XBC_W)
        act_c = _conv_silu(p_c, conv_w8[l], cb_, n_ctx, n_ctx)
        act_l = _conv_silu(p_l, conv_w8[l], cb_, n, min(n, 512))
        yf_c, yb_c, st_c = _ssd(act_c, dt_c, alog_p[l], dtbias_p[l], emat, emat_t, zero_state,
                                batch, n_ctx)
        yf_l, yb_l, _ = _ssd(act_l, dt_l, alog_p[l], dtbias_p[l], emat, emat_t, st_c, batch, n)
        sw = subln_w[l].reshape(1, ATT_V)
        att_l = _attention(p_l, p_c, p_l, lam_all[l], sw, lam_inits[l], batch, n, n_ctx, n, tq_l, tk_l)
        four_l = _fourier(p_l, cn_l, sn_l, ccd, scd, batch, n, tf_l)
        merge_w = (dskip_w[l], ssd_norm_w[l].reshape(1, D_INNER), wof_b[l], woa_b[l], wos_b[l],
                   wout_b[l], norm_f.reshape(1, D_MODEL))
        if not last:
            att_c = _attention(p_c, p_c, None, lam_all[l], sw, lam_inits[l], batch, n_ctx, n_ctx, 0,
                               n_ctx, 0)
            four_c = _fourier(p_c, cn_c, sn_c, ccd, scd, batch, n_ctx, n_ctx)
            xc = _merge(four_c, att_c, yf_c, yb_c, act_c, p_c, xc, mod_c, *merge_w,
                        n_ctx, n_ctx, False)
        xl = _merge(four_l, att_l, yf_l, yb_l, act_l, p_l, xl, mod_l, *merge_w, n, tmerge_l, last)
    return xl.reshape(batch, n, D_MODEL)
```

```python
import functools
import math

import numpy as np
import jax
import jax.numpy as jnp
from jax import lax
from jax.experimental import pallas as pl
from jax.experimental.pallas import tpu as pltpu

F32 = jnp.float32
BF16 = jnp.bfloat16
EPS = 1e-6

D_MODEL = 1024
GRID_W = 64
F_GROUP_W = 128
F_W = 512
ATT_HEADS = 4
ATT_QK = 64
ATT_V = 128
ROPE_AXIS = 32
ROPE_BASE = 10000.0
D_INNER = 512
SSD_P = 64
SSD_HEADS = 8
SSD_GROUPS = 2
D_STATE = 128
CONV_W = 5
CHUNK = 128
XBC_W = 1024
DT_W = 16
MERGE_W = 3072

LANE = 128
SUBLANE = 8
VMEM_LIMIT = 56 * 1024 * 1024

CB = 512
COL_XBC, COL_GATES, COL_FU, COL_FG, COL_AG, COL_Z = 0, 2, 8, 9, 10, 11
N_F32B = 12
COL_Q, COL_K, COL_V = 12, 13, 14
N_COLB = 15
N_F32 = N_F32B * CB
N_QKV = (N_COLB - N_F32B) * CB
Q_SCALE = ATT_QK ** -0.5 * math.log2(math.e)


def _dot(a, b):
    return jnp.dot(a, b, preferred_element_type=F32)


def _dot_nt(a, b):
    return lax.dot_general(a, b, (((1,), (1,)), ((), ())), preferred_element_type=F32)


def _split3(x):
    x1 = x.astype(BF16)
    r1 = x - x1.astype(F32)
    x2 = r1.astype(BF16)
    x3 = (r1 - x2.astype(F32)).astype(BF16)
    return x1, x2, x3


def _dot_sel_r(x, sel):
    x1, x2, x3 = _split3(x)
    return _dot(x1, sel) + _dot(x2, sel) + _dot(x3, sel)


def _dot_sel_l(sel, x):
    x1, x2, x3 = _split3(x)
    return _dot(sel, x1) + _dot(sel, x2) + _dot(sel, x3)


def _silu(x):
    return x * jax.nn.sigmoid(x)


def _params(*sem):
    return pltpu.CompilerParams(dimension_semantics=sem, vmem_limit_bytes=VMEM_LIMIT)


def _mod_kernel(cc_ref, w_ref, b_ref, lam_ref, linit_ref, mod_ref, lam_out_ref):
    s = _silu(cc_ref[...])
    mod_ref[0] = jnp.dot(s, w_ref[0], precision=lax.Precision.HIGHEST,
                         preferred_element_type=F32) + b_ref[0]
    lp = lam_ref[0]
    s1 = jnp.sum(lp[0:1] * lp[1:2], axis=-1, keepdims=True)
    s2 = jnp.sum(lp[2:3] * lp[3:4], axis=-1, keepdims=True)
    lam_out_ref[0] = jnp.broadcast_to(jnp.exp(s1) - jnp.exp(s2), (SUBLANE, LANE)) + linit_ref[0]


def _modulation(cc, w_mod, b_mod, lam, linit):
    depth = w_mod.shape[0]
    rows = cc.shape[0]
    tn = D_MODEL
    return pl.pallas_call(
        _mod_kernel,
        grid=(depth, 3 * D_MODEL // tn),
        in_specs=[
            pl.BlockSpec((rows, D_MODEL), lambda l, j: (0, 0)),
            pl.BlockSpec((1, D_MODEL, tn), lambda l, j: (l, 0, j)),
            pl.BlockSpec((1, 1, tn), lambda l, j: (l, 0, j)),
            pl.BlockSpec((1, 4, ATT_QK), lambda l, j: (l, 0, 0)),
            pl.BlockSpec((1, SUBLANE, LANE), lambda l, j: (l, 0, 0)),
        ],
        out_specs=[
            pl.BlockSpec((1, rows, tn), lambda l, j: (l, 0, j)),
            pl.BlockSpec((1, SUBLANE, LANE), lambda l, j: (l, 0, 0)),
        ],
        out_shape=[
            jax.ShapeDtypeStruct((depth, rows, 3 * D_MODEL), F32),
            jax.ShapeDtypeStruct((depth, SUBLANE, LANE), F32),
        ],
        compiler_params=_params("arbitrary", "arbitrary"),
        name="adaln_mod",
    )(cc, w_mod, b_mod, lam, linit)


def _rope(t, cos, sin):
    w = t.shape[-1]
    lane = lax.broadcasted_iota(jnp.int32, t.shape, 1)
    first = (lane % ROPE_AXIS) < (ROPE_AXIS // 2)
    rot = jnp.where(first, -pltpu.roll(t, w - ROPE_AXIS // 2, 1), pltpu.roll(t, ROPE_AXIS // 2, 1))
    return t * cos + rot * sin


def _inproj_kernel(*refs, rope):
    if rope:
        x_ref, mod_ref, nw_ref, w_ref, wdt_ref, cos_ref, sin_ref, p_ref, qkv_ref, dt_ref, h_ref = refs
    else:
        x_ref, mod_ref, nw_ref, w_ref, wdt_ref, p_ref, qkv_ref, dt_ref, h_ref = refs
    j = pl.program_id(1)

    @pl.when(j == 0)
    def _():
        x = x_ref[...]
        y = x * lax.rsqrt(jnp.mean(x * x, axis=-1, keepdims=True) + EPS) * nw_ref[...]
        m = mod_ref[0]
        h = y * (1.0 + m[:, D_MODEL:2 * D_MODEL]) + m[:, 0:D_MODEL]
        hb = h.astype(BF16)
        h_ref[...] = hb
        dt_ref[...] = _dot(hb, wdt_ref[...])

    acc = _dot(h_ref[...], w_ref[...])
    roped = (lambda t: _rope(t, cos_ref[...], sin_ref[...])) if rope else (lambda t: t)

    @pl.when(j < N_F32B)
    def _():
        p_ref[...] = acc

    @pl.when(j == COL_Q)
    def _():
        qkv_ref[...] = (roped(acc) * Q_SCALE).astype(BF16)

    @pl.when(j == COL_K)
    def _():
        qkv_ref[...] = roped(acc).astype(BF16)

    @pl.when(j == COL_V)
    def _():
        qkv_ref[...] = acc.astype(BF16)


def _inproj(x2, mod, norm_w, w_main, w_dt, cos, sin, n, tm):
    rows = x2.shape[0]
    tiles_per_b = n // tm
    rope = cos is not None
    in_specs = [
        pl.BlockSpec((tm, D_MODEL), lambda i, j: (i, 0)),
        pl.BlockSpec((1, 1, 3 * D_MODEL), lambda i, j: (i // tiles_per_b, 0, 0)),
        pl.BlockSpec((1, D_MODEL), lambda i, j: (0, 0)),
        pl.BlockSpec((D_MODEL, CB), lambda i, j: (0, j)),
        pl.BlockSpec((D_MODEL, LANE), lambda i, j: (0, 0)),
    ]
    args = [x2, mod, norm_w, w_main, w_dt]
    if rope:
        in_specs += [pl.BlockSpec((tm, CB), lambda i, j: (i % tiles_per_b, 0))] * 2
        args += [cos, sin]
    return pl.pallas_call(
        functools.partial(_inproj_kernel, rope=rope),
        grid=(rows // tm, N_COLB),
        in_specs=in_specs,
        out_specs=[
            pl.BlockSpec((tm, CB), lambda i, j: (i, jnp.minimum(j, N_F32B - 1))),
            pl.BlockSpec((tm, CB), lambda i, j: (i, jnp.maximum(j - N_F32B, 0))),
            pl.BlockSpec((tm, LANE), lambda i, j: (i, 0)),
        ],
        out_shape=[
            jax.ShapeDtypeStruct((rows, N_F32), F32),
            jax.ShapeDtypeStruct((rows, N_QKV), BF16),
            jax.ShapeDtypeStruct((rows, LANE), F32),
        ],
        scratch_shapes=[pltpu.VMEM((tm, D_MODEL), BF16)],
        compiler_params=_params("parallel", "arbitrary"),
        name="inproj",
    )(*args)


HALO = SUBLANE


def _conv_kernel(cur_ref, prev_ref, next_ref, w_ref, b_ref, o_ref, ext_ref, *, tiles_per_b):
    i = pl.program_id(0)
    tc = cur_ref.shape[0]
    t = i % tiles_per_b
    ext_ref[0:HALO, :] = jnp.where(t == 0, 0.0, prev_ref[...])
    ext_ref[HALO:HALO + tc, :] = cur_ref[...]
    ext_ref[HALO + tc:2 * HALO + tc, :] = jnp.where(t == tiles_per_b - 1, 0.0, next_ref[...])
    acc = jnp.broadcast_to(b_ref[...], o_ref.shape)
    for k in range(CONV_W):
        acc = acc + ext_ref[pl.ds(HALO - CONV_W // 2 + k, tc), :] * w_ref[k:k + 1, :]
    o_ref[...] = _silu(acc)


def _conv_silu(p, conv_w8, conv_b, n, tc):
    rows = p.shape[0]
    tiles_per_b = n // tc
    hb = tc // HALO
    last_hb = rows // HALO - 1
    return pl.pallas_call(
        functools.partial(_conv_kernel, tiles_per_b=tiles_per_b),
        grid=(rows // tc, XBC_W // CB),
        in_specs=[
            pl.BlockSpec((tc, CB), lambda i, j: (i, COL_XBC + j)),
            pl.BlockSpec((HALO, CB), lambda i, j: (jnp.maximum(i * hb - 1, 0), COL_XBC + j)),
            pl.BlockSpec((HALO, CB), lambda i, j: (jnp.minimum((i + 1) * hb, last_hb), COL_XBC + j)),
            pl.BlockSpec((SUBLANE, CB), lambda i, j: (0, j)),
            pl.BlockSpec((1, CB), lambda i, j: (0, j)),
        ],
        out_specs=pl.BlockSpec((tc, CB), lambda i, j: (i, j)),
        out_shape=jax.ShapeDtypeStruct((rows, XBC_W), F32),
        scratch_shapes=[pltpu.VMEM((tc + 2 * HALO, CB), F32)],
        compiler_params=_params("parallel", "parallel"),
        name="conv_silu",
    )(p, p, p, conv_w8, conv_b)


def _attn_kernel(*refs, with_latent, one_minus_lam_init, n_ctx, n_lat, tk):
    if with_latent:
        (q_ref, kc_ref, vc_ref, kl_ref, vl_ref, lam_ref, sw_ref, o_ref,
         k_scr, v_scr, m_scr, acc_scr) = refs
    else:
        q_ref, kc_ref, vc_ref, lam_ref, sw_ref, o_ref, k_scr, v_scr, m_scr, acc_scr = refs
    tq = q_ref.shape[0]

    @pl.when(pl.program_id(2) == 0)
    def _():
        def put(k_ref, v_ref, off):
            rows = k_ref.shape[0]
            k = k_ref[...]
            lane = lax.broadcasted_iota(jnp.int32, k.shape, 1)
            zero = jnp.zeros_like(k)
            k_scr[0, off:off + rows, :] = jnp.where(lane < ATT_QK, k, zero)
            k_scr[1, off:off + rows, :] = jnp.where(lane >= ATT_QK, k, zero)
            v_scr[off:off + rows, 0:ATT_V] = v_ref[...]
            v_scr[off:off + rows, ATT_V:2 * ATT_V] = jnp.ones((rows, ATT_V), BF16)
        put(kc_ref, vc_ref, 0)
        if with_latent:
            put(kl_ref, vl_ref, n_ctx)

    m_scr[...] = jnp.full(m_scr.shape, -jnp.inf, F32)
    acc_scr[...] = jnp.zeros(acc_scr.shape, F32)
    q = q_ref[...]

    def chunk(off, size):
        v = v_scr[pl.ds(off, size), :]
        nt = size // LANE
        for mi in range(2):
            s = _dot_nt(q, k_scr[mi, pl.ds(off, size), :])
            tiles = [s[:, t * LANE:(t + 1) * LANE] for t in range(nt)]
            mx = functools.reduce(jnp.maximum, tiles)
            m_prev = m_scr[mi]
            m_new = jnp.maximum(m_prev, jnp.max(mx, axis=-1, keepdims=True))
            p = jnp.concatenate([jnp.exp2(t - m_new) for t in tiles], axis=1).astype(BF16)
            alpha = jnp.exp2(m_prev - m_new)
            acc_scr[mi] = jnp.concatenate([alpha, alpha], axis=1) * acc_scr[mi] + _dot(p, v)
            m_scr[mi] = m_new

    chunk(0, n_ctx)
    if with_latent:
        def body(t, carry):
            chunk(pl.multiple_of(n_ctx + t * tk, LANE), tk)
            return carry
        lax.fori_loop(0, n_lat // tk, body, 0)

    lam = lam_ref[0:1, :]
    a0, a1 = acc_scr[0], acc_scr[1]
    o = a0[:, 0:ATT_V] / a0[:, ATT_V:] - lam * (a1[:, 0:ATT_V] / a1[:, ATT_V:])
    y = o * lax.rsqrt(jnp.mean(o * o, axis=-1, keepdims=True) + EPS) * sw_ref[...]
    o_ref[...] = y * one_minus_lam_init


def _attention(qkv_q, qkv_c, qkv_l, lam_l, subln_w, lam_init, batch, nq_len, nc_len, nl_len, tq, tk):
    with_latent = qkv_l is not None
    nq = nq_len // tq
    hq, hk, hv = ((c - N_F32B) * (CB // LANE) for c in (COL_Q, COL_K, COL_V))
    in_specs = [
        pl.BlockSpec((tq, LANE), lambda b, h, i: (b * nq + i, hq + h)),
        pl.BlockSpec((nc_len, LANE), lambda b, h, i: (b, hk + h)),
        pl.BlockSpec((nc_len, LANE), lambda b, h, i: (b, hv + h)),
    ]
    args = [qkv_q, qkv_c, qkv_c]
    if with_latent:
        in_specs += [
            pl.BlockSpec((nl_len, LANE), lambda b, h, i: (b, hk + h)),
            pl.BlockSpec((nl_len, LANE), lambda b, h, i: (b, hv + h)),
        ]
        args += [qkv_l, qkv_l]
    in_specs += [
        pl.BlockSpec((SUBLANE, LANE), lambda b, h, i: (0, 0)),
        pl.BlockSpec((1, ATT_V), lambda b, h, i: (0, 0)),
    ]
    args += [lam_l, subln_w]
    n_keys = nc_len + nl_len
    return pl.pallas_call(
        functools.partial(_attn_kernel, with_latent=with_latent,
                          one_minus_lam_init=1.0 - lam_init, n_ctx=nc_len, n_lat=nl_len, tk=tk),
        grid=(batch, ATT_HEADS, nq),
        in_specs=in_specs,
        out_specs=pl.BlockSpec((tq, ATT_V), lambda b, h, i: (b * nq + i, h)),
        out_shape=jax.ShapeDtypeStruct((batch * nq_len, ATT_HEADS * ATT_V), F32),
        scratch_shapes=[
            pltpu.VMEM((2, n_keys, LANE), BF16),
            pltpu.VMEM((n_keys, 2 * ATT_V), BF16),
            pltpu.VMEM((2, tq, LANE), F32),
            pltpu.VMEM((2, tq, 2 * ATT_V), F32),
        ],
        compiler_params=_params("parallel", "parallel", "arbitrary"),
        name="diff_attention",
    )(*args)


def _fourier_kernel(x_ref, cn_ref, sn_ref, cc_ref, sc_ref, o_ref, *, scale):
    k = pl.program_id(1)
    xb = x_ref[...].astype(BF16)
    xc = _dot(xb, cc_ref[...]).astype(BF16)
    xs = _dot(xb, sc_ref[...]).astype(BF16)
    part = _dot(cn_ref[...], xc) - _dot(sn_ref[...], xs)

    @pl.when(k == 0)
    def _():
        o_ref[...] = part

    @pl.when(k > 0)
    def _():
        o_ref[...] += part

    @pl.when(k == pl.num_programs(1) - 1)
    def _():
        o_ref[...] *= scale


def _dft_tables(n):
    idx = jnp.arange(n, dtype=jnp.int32)
    ang = ((idx[:, None] * idx[None, :]) % n).astype(F32) * (2.0 * math.pi / n)
    return jnp.cos(ang), jnp.sin(ang)


def _fourier(p, cn, sn, ccd, scd, batch, n, tk):
    nk = n // tk
    scale = 1.0 / math.sqrt(n * F_GROUP_W)
    return pl.pallas_call(
        functools.partial(_fourier_kernel, scale=scale),
        grid=(batch, nk),
        in_specs=[
            pl.BlockSpec((tk, CB), lambda b, k: (b * nk + k, COL_FU)),
            pl.BlockSpec((n, tk), lambda b, k: (0, k)),
            pl.BlockSpec((n, tk), lambda b, k: (0, k)),
            pl.BlockSpec((F_W, F_W), lambda b, k: (0, 0)),
            pl.BlockSpec((F_W, F_W), lambda b, k: (0, 0)),
        ],
        out_specs=pl.BlockSpec((n, F_W), lambda b, k: (b, 0)),
        out_shape=jax.ShapeDtypeStruct((batch * n, F_W), F32),
        compiler_params=_params("parallel", "arbitrary"),
        name="fourier_mix",
    )(p, cn, sn, ccd, scd)


def _softplus(x):
    return jnp.maximum(x, 0.0) + jnp.log1p(jnp.exp(-jnp.abs(x)))


def _ssd_kernel(xsf_ref, bcf_ref, dtf_ref, xsb_ref, bcb_ref, dtb_ref, alog_ref, dtbias_ref,
                e_ref, et_ref, init_ref, yf_ref, yb_ref, fin_ref, state_ref):
    c = pl.program_id(1)
    q = xsf_ref.shape[0]
    gw = D_INNER // SSD_GROUPS
    hpg = SSD_HEADS // SSD_GROUPS

    @pl.when(c == 0)
    def _():
        state_ref[...] = init_ref[0]

    row = lax.broadcasted_iota(jnp.int32, (q, q), 0)
    col = lax.broadcasted_iota(jnp.int32, (q, q), 1)
    head_of_lane = lax.broadcasted_iota(jnp.int32, (q, gw), 1) // SSD_P
    neg_a = -jnp.exp(alog_ref[...])

    dirs = ((xsf_ref, bcf_ref, dtf_ref, yf_ref), (xsb_ref, bcb_ref, dtb_ref, yb_ref))
    for d, (xs_ref, bc_ref, dt_ref, y_ref) in enumerate(dirs):
        fwd = d == 0
        tri = (row >= col) if fwd else (row <= col)
        tri_b = tri.astype(F32).astype(BF16)
        dt = _softplus(dt_ref[...] + dtbias_ref[...])
        acs = _dot_sel_l(tri_b, dt * neg_a)
        acs_t = acs.T
        last = q - 1 if fwd else 0
        tot_row = acs[last:last + 1, :]
        tot_col = acs_t[:, last:last + 1]
        e = e_ref[d]
        dt_w = _dot_sel_r(dt, e)
        eacs_w = _dot_sel_r(jnp.exp(acs), e)
        dec_w = _dot_sel_r(jnp.exp(tot_row - acs), e)
        sdec = _dot_sel_l(et_ref[d], jnp.broadcast_to(jnp.exp(tot_col), (LANE, D_STATE)))
        xd = xs_ref[...] * dt_w
        xdb = xd.astype(BF16)
        xdd_t = (xd * dec_w).T.astype(BF16)
        bc = bc_ref[...]
        for g in range(SSD_GROUPS):
            bg = bc[:, g * D_STATE:(g + 1) * D_STATE].astype(BF16)
            cg = bc[:, (SSD_GROUPS + g) * D_STATE:(SSD_GROUPS + g + 1) * D_STATE].astype(BF16)
            cb = _dot_nt(cg, bg)
            lanes = slice(g * gw, (g + 1) * gw)
            s_g = state_ref[d, lanes, :]
            y = _dot_nt(cg, s_g.astype(BF16)) * eacs_w[:, lanes]
            xg = xdb[:, lanes]
            for r in range(hpg):
                jl = d * SSD_HEADS + g * hpg + r
                seg = jnp.where(tri, acs[:, jl:jl + 1] - acs_t[jl:jl + 1, :], -jnp.inf)
                mat = (cb * jnp.exp(seg)).astype(BF16)
                y = y + _dot(mat, jnp.where(head_of_lane == r, xg, jnp.zeros_like(xg)))
            y_ref[:, lanes] = y
            state_ref[d, lanes, :] = s_g * sdec[lanes, :] + _dot(xdd_t[lanes, :], bg)

    @pl.when(c == pl.num_programs(1) - 1)
    def _():
        fin_ref[0] = state_ref[...]


def _ssd(act, dt, alog, dtbias, emat, emat_t, init, batch, n):
    nc = n // CHUNK
    fidx = lambda b, c: b * nc + c
    bidx = lambda b, c: b * nc + (nc - 1 - c)
    st_shape = (2, D_INNER, D_STATE)
    return pl.pallas_call(
        _ssd_kernel,
        grid=(batch, nc),
        in_specs=[
            pl.BlockSpec((CHUNK, CB), lambda b, c: (fidx(b, c), 0)),
            pl.BlockSpec((CHUNK, CB), lambda b, c: (fidx(b, c), 1)),
            pl.BlockSpec((CHUNK, LANE), lambda b, c: (fidx(b, c), 0)),
            pl.BlockSpec((CHUNK, CB), lambda b, c: (bidx(b, c), 0)),
            pl.BlockSpec((CHUNK, CB), lambda b, c: (bidx(b, c), 1)),
            pl.BlockSpec((CHUNK, LANE), lambda b, c: (bidx(b, c), 0)),
            pl.BlockSpec((1, LANE), lambda b, c: (0, 0)),
            pl.BlockSpec((1, LANE), lambda b, c: (0, 0)),
            pl.BlockSpec((2, LANE, D_INNER), lambda b, c: (0, 0, 0)),
            pl.BlockSpec((2, D_INNER, LANE), lambda b, c: (0, 0, 0)),
            pl.BlockSpec((1,) + st_shape, lambda b, c: (b, 0, 0, 0)),
        ],
        out_specs=[
            pl.BlockSpec((CHUNK, D_INNER), lambda b, c: (fidx(b, c), 0)),
            pl.BlockSpec((CHUNK, D_INNER), lambda b, c: (bidx(b, c), 0)),
            pl.BlockSpec((1,) + st_shape, lambda b, c: (b, 0, 0, 0)),
        ],
        out_shape=[
            jax.ShapeDtypeStruct((batch * n, D_INNER), F32),
            jax.ShapeDtypeStruct((batch * n, D_INNER), F32),
            jax.ShapeDtypeStruct((batch,) + st_shape, F32),
        ],
        scratch_shapes=[pltpu.VMEM(st_shape, F32)],
        compiler_params=_params("parallel", "arbitrary"),
        name="ssd_scan",
    )(act, act, dt, act, act, dt, alog, dtbias, emat, emat_t, init)


def _merge_kernel(four_ref, att_ref, yf_ref, yb_ref, xs_ref, fg_ref, ag_ref, z_ref,
                  g0_ref, g1_ref, g2_ref, x_ref, mod_ref, dskip_ref, snw_ref,
                  wof_ref, woa_ref, wos_ref, wout_ref, nf_ref, o_ref, *, final_norm):
    y_f = _dot((four_ref[...] * _silu(fg_ref[...])).astype(BF16), wof_ref[...])
    y_a = _dot((att_ref[...] * _silu(ag_ref[...])).astype(BF16), woa_ref[...])
    ys = yf_ref[...] + yb_ref[...] + dskip_ref[...] * xs_ref[...]
    t = ys * _silu(z_ref[...])
    t = t * lax.rsqrt(jnp.mean(t * t, axis=-1, keepdims=True) + EPS) * snw_ref[...]
    y_s = _dot(t.astype(BF16), wos_ref[...])
    y = (jax.nn.sigmoid(g0_ref[...]) * y_f + jax.nn.sigmoid(g1_ref[...]) * y_a
         + jax.nn.sigmoid(g2_ref[...]) * y_s)
    out = _dot(y.astype(BF16), wout_ref[...])
    xn = x_ref[...] + mod_ref[0][:, 2 * D_MODEL:3 * D_MODEL] * out
    if final_norm:
        xn = xn * lax.rsqrt(jnp.mean(xn * xn, axis=-1, keepdims=True) + EPS) * nf_ref[...]
    o_ref[...] = xn


def _merge(four, att, yf, yb, act, p, x2, mod, dskip_w, snw, wof, woa, wos, wout, norm_f,
           n, tm, final_norm):
    rows = x2.shape[0]
    tiles_per_b = n // tm
    gcol = COL_GATES * CB // D_MODEL
    row_blk = lambda w, cidx: pl.BlockSpec((tm, w), lambda i: (i, cidx))
    const = lambda shape: pl.BlockSpec(shape, lambda i: (0,) * len(shape))
    return pl.pallas_call(
        functools.partial(_merge_kernel, final_norm=final_norm),
        grid=(rows // tm,),
        in_specs=[
            row_blk(F_W, 0), row_blk(CB, 0), row_blk(D_INNER, 0), row_blk(D_INNER, 0),
            row_blk(CB, 0),
            row_blk(CB, COL_FG), row_blk(CB, COL_AG), row_blk(CB, COL_Z),
            row_blk(D_MODEL, gcol), row_blk(D_MODEL, gcol + 1), row_blk(D_MODEL, gcol + 2),
            row_blk(D_MODEL, 0),
            pl.BlockSpec((1, 1, 3 * D_MODEL), lambda i: (i // tiles_per_b, 0, 0)),
            const((1, D_INNER)), const((1, D_INNER)),
            const((F_W, D_MODEL)), const((CB, D_MODEL)), const((D_INNER, D_MODEL)),
            const((D_MODEL, D_MODEL)), const((1, D_MODEL)),
        ],
        out_specs=pl.BlockSpec((tm, D_MODEL), lambda i: (i, 0)),
        out_shape=jax.ShapeDtypeStruct((rows, D_MODEL), F32),
        compiler_params=_params("parallel"),
        name="branch_merge",
    )(four, att, yf, yb, act, p, p, p, p, p, p, x2, mod, dskip_w, snw, wof, woa, wos, wout, norm_f)


def _rope_tables(n):
    rows = n // GRID_W
    row = jnp.repeat(jnp.arange(rows, dtype=F32), GRID_W)
    col = jnp.tile(jnp.arange(GRID_W, dtype=F32), rows)
    freqs = ROPE_BASE ** (-jnp.arange(0, ROPE_AXIS, 2, dtype=F32) / ROPE_AXIS)
    ang_r = row[:, None] * freqs
    ang_c = col[:, None] * freqs
    ang = jnp.concatenate([ang_r, ang_r, ang_c, ang_c], axis=-1)
    reps = CB // ATT_QK
    return jnp.tile(jnp.cos(ang), (1, reps)), jnp.tile(jnp.sin(ang), (1, reps))


def _head_expanders():
    e = np.zeros((2, LANE, D_INNER), np.float32)
    for d in range(2):
        for h in range(SSD_HEADS):
            e[d, d * SSD_HEADS + h, h * SSD_P:(h + 1) * SSD_P] = 1.0
    return jnp.asarray(e, BF16), jnp.asarray(e.transpose(0, 2, 1), BF16)


def _group_dft(w):
    cw, sw = _dft_tables(F_GROUP_W)
    eye = jnp.eye(w // F_GROUP_W, dtype=F32)
    return jnp.kron(eye, cw).astype(BF16), jnp.kron(eye, sw).astype(BF16)


def kernel(x, c, ctx, c_ctx, w_mod, b_mod, norm_w, w_in, conv_w, conv_b, a_log, dt_bias, d_skip,
           ssd_norm_w, lam, subln_w, w_of, w_oa, w_os, w_out, norm_f):
    batch, n, _ = x.shape
    n_ctx = ctx.shape[1]
    depth = w_mod.shape[0]
    assert n % GRID_W == 0 and n % CHUNK == 0 and n_ctx % CHUNK == 0

    o_q = 2 * F_W
    o_ag = o_q + 3 * CB
    o_xbc = o_ag + CB + D_INNER
    o_dt = o_xbc + XBC_W
    o_gt = o_dt + DT_W
    w_main = jnp.concatenate([w_in[:, :, o_xbc:o_dt], w_in[:, :, o_gt:], w_in[:, :, :o_q],
                              w_in[:, :, o_ag:o_xbc], w_in[:, :, o_q:o_ag]], axis=-1).astype(BF16)
    w_dt = jnp.pad(w_in[:, :, o_dt:o_gt], ((0, 0), (0, 0), (0, LANE - DT_W))).astype(BF16)
    conv_w8 = jnp.pad(conv_w, ((0, 0), (0, SUBLANE - CONV_W), (0, 0)))
    pad_lanes = lambda a: jnp.pad(a.reshape(depth, 1, DT_W), ((0, 0), (0, 0), (0, LANE - DT_W)))
    alog_p, dtbias_p = pad_lanes(a_log), pad_lanes(dt_bias)
    dskip_w = jnp.repeat(d_skip, SSD_P, axis=-1).reshape(depth, 1, D_INNER)
    wof_b, woa_b, wos_b, wout_b = (w.astype(BF16) for w in (w_of, w_oa, w_os, w_out))
    lam_inits = [0.8 - 0.6 * math.exp(-0.3 * l) for l in range(depth)]
    linit = jnp.asarray(np.broadcast_to(np.asarray(lam_inits, np.float32)[:, None, None],
                                        (depth, SUBLANE, LANE)))

    cos_t, sin_t = _rope_tables(n)
    cn_l, sn_l = (t.astype(BF16) for t in _dft_tables(n))
    cn_c, sn_c = (t.astype(BF16) for t in _dft_tables(n_ctx))
    ccd, scd = _group_dft(F_W)
    emat, emat_t = _head_expanders()
    zero_state = jnp.zeros((batch, 2, D_INNER, D_STATE), F32)

    mod_rows = -(-(batch + 1) // SUBLANE) * SUBLANE
    cc = jnp.concatenate([c, c_ctx[None, :], jnp.zeros((mod_rows - batch - 1, D_MODEL), F32)], axis=0)
    mod_all, lam_all = _modulation(cc, w_mod, b_mod.reshape(depth, 1, 3 * D_MODEL), lam, linit)

    tm_l = min(n, 1024)
    tq_l = min(n, 512)
    tk_l = min(n, 1024)
    tf_l = min(n, 512)
    tmerge_l = min(n, 256)
    xl = x.reshape(batch * n, D_MODEL)
    xc = ctx.reshape(batch * n_ctx, D_MODEL)
    for l in range(depth):
        last = l == depth - 1
        mod_l = mod_all[l, :batch].reshape(batch, 1, 3 * D_MODEL)
        mod_c = jnp.broadcast_to(mod_all[l, batch].reshape(1, 1, 3 * D_MODEL), (batch, 1, 3 * D_MODEL))
        nw = norm_w[l].reshape(1, D_MODEL)
        p_c, qkv_c, dt_c = _inproj(xc, mod_c, nw, w_main[l], w_dt[l], None, None, n_ctx, n_ctx)
        p_l, qkv_l, dt_l = _inproj(xl, mod_l, nw, w_main[l], w_dt[l], cos_t, sin_t, n, tm_l)
        cb_ = conv_b[l].reshape(1, XBC_W)
        act_c = _conv_silu(p_c, conv_w8[l], cb_, n_ctx, n_ctx)
        act_l = _conv_silu(p_l, conv_w8[l], cb_, n, min(n, 512))
        yf_c, yb_c, st_c = _ssd(act_c, dt_c, alog_p[l], dtbias_p[l], emat, emat_t, zero_state,
                                batch, n_ctx)
        yf_l, yb_l, _ = _ssd(act_l, dt_l, alog_p[l], dtbias_p[l], emat, emat_t, st_c, batch, n)
        sw = subln_w[l].reshape(1, ATT_V)
        att_l = _attention(qkv_l, qkv_c, qkv_l, lam_all[l], sw, lam_inits[l], batch, n, n_ctx, n,
                           tq_l, tk_l)
        four_l = _fourier(p_l, cn_l, sn_l, ccd, scd, batch, n, tf_l)
        merge_w = (dskip_w[l], ssd_norm_w[l].reshape(1, D_INNER), wof_b[l], woa_b[l], wos_b[l],
                   wout_b[l], norm_f.reshape(1, D_MODEL))
        if not last:
            att_c = _attention(qkv_c, qkv_c, None, lam_all[l], sw, lam_inits[l], batch, n_ctx, n_ctx, 0,
                               n_ctx, 0)
            four_c = _fourier(p_c, cn_c, sn_c, ccd, scd, batch, n_ctx, n_ctx)
            xc = _merge(four_c, att_c, yf_c, yb_c, act_c, p_c, xc, mod_c, *merge_w,
                        n_ctx, n_ctx, False)
        xl = _merge(four_l, att_l, yf_l, yb_l, act_l, p_l, xl, mod_l, *merge_w, n, tmerge_l, last)
    return xl.reshape(batch, n, D_MODEL)
```

```python
import functools
import math

import numpy as np
import jax
import jax.numpy as jnp
from jax import lax
from jax.experimental import pallas as pl
from jax.experimental.pallas import tpu as pltpu

F32 = jnp.float32
BF16 = jnp.bfloat16
EPS = 1e-6

D_MODEL = 1024
GRID_W = 64
F_GROUP_W = 128
F_W = 512
ATT_HEADS = 4
ATT_QK = 64
ATT_V = 128
ROPE_AXIS = 32
ROPE_BASE = 10000.0
D_INNER = 512
SSD_P = 64
SSD_HEADS = 8
SSD_GROUPS = 2
D_STATE = 128
CONV_W = 5
CHUNK = 128
XBC_W = 1024
DT_W = 16
MERGE_W = 3072

LANE = 128
SUBLANE = 8
VMEM_LIMIT = 56 * 1024 * 1024

SUBLANE_BF16 = 16
CB = 512
COL_XBC, COL_GATES, COL_FU, COL_FG, COL_AG, COL_Z = 0, 2, 8, 9, 10, 11
COL_Q, COL_K, COL_V = 12, 13, 14
N_COLB = 15
N_MAIN = N_COLB * CB
PROJ_STEP_B = 3
PROJ_STEPS = N_COLB // PROJ_STEP_B
assert COL_Q == (PROJ_STEPS - 1) * PROJ_STEP_B and COL_V == N_COLB - 1
Q_SCALE = ATT_QK ** -0.5 * math.log2(math.e)


def _dot(a, b):
    return jnp.dot(a, b, preferred_element_type=F32)


def _dot_nt(a, b):
    return lax.dot_general(a, b, (((1,), (1,)), ((), ())), preferred_element_type=F32)


def _split3(x):
    x1 = x.astype(BF16)
    r1 = x - x1.astype(F32)
    x2 = r1.astype(BF16)
    x3 = (r1 - x2.astype(F32)).astype(BF16)
    return x1, x2, x3


def _dot_sel_r(x, sel):
    x1, x2, x3 = _split3(x)
    return _dot(x1, sel) + _dot(x2, sel) + _dot(x3, sel)


def _dot_sel_l(sel, x):
    x1, x2, x3 = _split3(x)
    return _dot(sel, x1) + _dot(sel, x2) + _dot(sel, x3)


def _silu(x):
    return x * jax.nn.sigmoid(x)


def _params(*sem):
    return pltpu.CompilerParams(dimension_semantics=sem, vmem_limit_bytes=VMEM_LIMIT)


def _mod_kernel(cc_ref, w_ref, b_ref, lam_ref, linit_ref, mod_ref, lam_out_ref):
    s = _silu(cc_ref[...])
    mod_ref[0] = jnp.dot(s, w_ref[0], precision=lax.Precision.HIGHEST,
                         preferred_element_type=F32) + b_ref[0]
    lp = lam_ref[0]
    s1 = jnp.sum(lp[0:1] * lp[1:2], axis=-1, keepdims=True)
    s2 = jnp.sum(lp[2:3] * lp[3:4], axis=-1, keepdims=True)
    lam_out_ref[0] = jnp.broadcast_to(jnp.exp(s1) - jnp.exp(s2), (SUBLANE, LANE)) + linit_ref[0]


def _modulation(cc, w_mod, b_mod, lam, linit):
    depth = w_mod.shape[0]
    rows = cc.shape[0]
    tn = D_MODEL
    return pl.pallas_call(
        _mod_kernel,
        grid=(depth, 3 * D_MODEL // tn),
        in_specs=[
            pl.BlockSpec((rows, D_MODEL), lambda l, j: (0, 0)),
            pl.BlockSpec((1, D_MODEL, tn), lambda l, j: (l, 0, j)),
            pl.BlockSpec((1, 1, tn), lambda l, j: (l, 0, j)),
            pl.BlockSpec((1, 4, ATT_QK), lambda l, j: (l, 0, 0)),
            pl.BlockSpec((1, SUBLANE, LANE), lambda l, j: (l, 0, 0)),
        ],
        out_specs=[
            pl.BlockSpec((1, rows, tn), lambda l, j: (l, 0, j)),
            pl.BlockSpec((1, SUBLANE, LANE), lambda l, j: (l, 0, 0)),
        ],
        out_shape=[
            jax.ShapeDtypeStruct((depth, rows, 3 * D_MODEL), F32),
            jax.ShapeDtypeStruct((depth, SUBLANE, LANE), F32),
        ],
        compiler_params=_params("arbitrary", "arbitrary"),
        name="adaln_mod",
    )(cc, w_mod, b_mod, lam, linit)


def _rope(t, cos, sin):
    w = t.shape[-1]
    lane = lax.broadcasted_iota(jnp.int32, t.shape, 1)
    first = (lane % ROPE_AXIS) < (ROPE_AXIS // 2)
    rot = jnp.where(first, -pltpu.roll(t, w - ROPE_AXIS // 2, 1), pltpu.roll(t, ROPE_AXIS // 2, 1))
    return t * cos + rot * sin


def _inproj_kernel(*refs, rope):
    if rope:
        x_ref, mod_ref, nw_ref, w_ref, wdt_ref, cos_ref, sin_ref, p_ref, dt_ref, h_ref = refs
    else:
        x_ref, mod_ref, nw_ref, w_ref, wdt_ref, p_ref, dt_ref, h_ref = refs
    j = pl.program_id(1)

    @pl.when(j == 0)
    def _():
        x = x_ref[...]
        y = x * lax.rsqrt(jnp.mean(x * x, axis=-1, keepdims=True) + EPS) * nw_ref[...]
        m = mod_ref[0]
        h = y * (1.0 + m[:, D_MODEL:2 * D_MODEL]) + m[:, 0:D_MODEL]
        hb = h.astype(BF16)
        h_ref[...] = hb
        dt_ref[...] = _dot(hb, wdt_ref[...])

    acc = _dot(h_ref[...], w_ref[...])

    @pl.when(j < PROJ_STEPS - 1)
    def _():
        p_ref[...] = acc.astype(BF16)

    @pl.when(j == PROJ_STEPS - 1)
    def _():
        q, k, v = (acc[:, b * CB:(b + 1) * CB] for b in range(PROJ_STEP_B))
        if rope:
            cos, sin = cos_ref[...], sin_ref[...]
            q, k = _rope(q, cos, sin), _rope(k, cos, sin)
        p_ref[:, 0:CB] = (q * Q_SCALE).astype(BF16)
        p_ref[:, CB:2 * CB] = k.astype(BF16)
        p_ref[:, 2 * CB:3 * CB] = v.astype(BF16)


def _inproj(x2, mod, norm_w, w_main, w_dt, cos, sin, n, tm):
    rows = x2.shape[0]
    tiles_per_b = n // tm
    rope = cos is not None
    ws = PROJ_STEP_B * CB
    in_specs = [
        pl.BlockSpec((tm, D_MODEL), lambda i, j: (i, 0)),
        pl.BlockSpec((1, 1, 3 * D_MODEL), lambda i, j: (i // tiles_per_b, 0, 0)),
        pl.BlockSpec((1, D_MODEL), lambda i, j: (0, 0)),
        pl.BlockSpec((D_MODEL, ws), lambda i, j: (0, j)),
        pl.BlockSpec((D_MODEL, LANE), lambda i, j: (0, 0)),
    ]
    args = [x2, mod, norm_w, w_main, w_dt]
    if rope:
        in_specs += [pl.BlockSpec((tm, CB), lambda i, j: (i % tiles_per_b, 0))] * 2
        args += [cos, sin]
    return pl.pallas_call(
        functools.partial(_inproj_kernel, rope=rope),
        grid=(rows // tm, PROJ_STEPS),
        in_specs=in_specs,
        out_specs=[
            pl.BlockSpec((tm, ws), lambda i, j: (i, j)),
            pl.BlockSpec((tm, LANE), lambda i, j: (i, 0)),
        ],
        out_shape=[
            jax.ShapeDtypeStruct((rows, N_MAIN), BF16),
            jax.ShapeDtypeStruct((rows, LANE), F32),
        ],
        scratch_shapes=[pltpu.VMEM((tm, D_MODEL), BF16)],
        compiler_params=_params("parallel", "arbitrary"),
        name="inproj",
    )(*args)


HALO = SUBLANE_BF16


def _conv_kernel(cur_ref, prev_ref, next_ref, w_ref, b_ref, o_ref, ext_ref, *, tiles_per_b):
    i = pl.program_id(0)
    tc = cur_ref.shape[0]
    t = i % tiles_per_b
    ext_ref[0:HALO, :] = jnp.where(t == 0, 0.0, prev_ref[...].astype(F32))
    ext_ref[HALO:HALO + tc, :] = cur_ref[...].astype(F32)
    ext_ref[HALO + tc:2 * HALO + tc, :] = jnp.where(t == tiles_per_b - 1, 0.0,
                                                    next_ref[...].astype(F32))
    acc = jnp.broadcast_to(b_ref[...], o_ref.shape)
    for k in range(CONV_W):
        acc = acc + ext_ref[pl.ds(HALO - CONV_W // 2 + k, tc), :] * w_ref[k:k + 1, :]
    o_ref[...] = _silu(acc).astype(o_ref.dtype)


def _conv_silu(p, conv_w8, conv_b, n, tc):
    rows = p.shape[0]
    tiles_per_b = n // tc
    hb = tc // HALO
    last_hb = rows // HALO - 1
    return pl.pallas_call(
        functools.partial(_conv_kernel, tiles_per_b=tiles_per_b),
        grid=(rows // tc, XBC_W // CB),
        in_specs=[
            pl.BlockSpec((tc, CB), lambda i, j: (i, COL_XBC + j)),
            pl.BlockSpec((HALO, CB), lambda i, j: (jnp.maximum(i * hb - 1, 0), COL_XBC + j)),
            pl.BlockSpec((HALO, CB), lambda i, j: (jnp.minimum((i + 1) * hb, last_hb), COL_XBC + j)),
            pl.BlockSpec((SUBLANE, CB), lambda i, j: (0, j)),
            pl.BlockSpec((1, CB), lambda i, j: (0, j)),
        ],
        out_specs=pl.BlockSpec((tc, CB), lambda i, j: (i, j)),
        out_shape=jax.ShapeDtypeStruct((rows, XBC_W), BF16),
        scratch_shapes=[pltpu.VMEM((tc + 2 * HALO, CB), F32)],
        compiler_params=_params("parallel", "parallel"),
        name="conv_silu",
    )(p, p, p, conv_w8, conv_b)


def _attn_kernel(*refs, with_latent, one_minus_lam_init, n_ctx, n_lat, tk):
    if with_latent:
        (q_ref, kc_ref, vc_ref, kl_ref, vl_ref, lam_ref, sw_ref, o_ref,
         k_scr, v_scr, m_scr, acc_scr) = refs
    else:
        q_ref, kc_ref, vc_ref, lam_ref, sw_ref, o_ref, k_scr, v_scr, m_scr, acc_scr = refs
    tq = q_ref.shape[0]

    @pl.when(pl.program_id(2) == 0)
    def _():
        def put(k_ref, v_ref, off):
            rows = k_ref.shape[0]
            k = k_ref[...]
            lane = lax.broadcasted_iota(jnp.int32, k.shape, 1)
            zero = jnp.zeros_like(k)
            k_scr[0, off:off + rows, :] = jnp.where(lane < ATT_QK, k, zero)
            k_scr[1, off:off + rows, :] = jnp.where(lane >= ATT_QK, k, zero)
            v_scr[off:off + rows, 0:ATT_V] = v_ref[...]
            v_scr[off:off + rows, ATT_V:2 * ATT_V] = jnp.ones((rows, ATT_V), BF16)
        put(kc_ref, vc_ref, 0)
        if with_latent:
            put(kl_ref, vl_ref, n_ctx)

    m_scr[...] = jnp.full(m_scr.shape, -jnp.inf, F32)
    acc_scr[...] = jnp.zeros(acc_scr.shape, F32)
    q = q_ref[...]

    def chunk(off, size):
        v = v_scr[pl.ds(off, size), :]
        nt = size // LANE
        for mi in range(2):
            s = _dot_nt(q, k_scr[mi, pl.ds(off, size), :])
            tiles = [s[:, t * LANE:(t + 1) * LANE] for t in range(nt)]
            mx = functools.reduce(jnp.maximum, tiles)
            m_prev = m_scr[mi]
            m_new = jnp.maximum(m_prev, jnp.max(mx, axis=-1, keepdims=True))
            p = jnp.concatenate([jnp.exp2(t - m_new) for t in tiles], axis=1).astype(BF16)
            alpha = jnp.exp2(m_prev - m_new)
            acc_scr[mi] = jnp.concatenate([alpha, alpha], axis=1) * acc_scr[mi] + _dot(p, v)
            m_scr[mi] = m_new

    chunk(0, n_ctx)
    if with_latent:
        for t in range(n_lat // tk):
            chunk(n_ctx + t * tk, tk)

    lam = lam_ref[0:1, :]
    a0, a1 = acc_scr[0], acc_scr[1]
    o = a0[:, 0:ATT_V] / a0[:, ATT_V:] - lam * (a1[:, 0:ATT_V] / a1[:, ATT_V:])
    y = o * lax.rsqrt(jnp.mean(o * o, axis=-1, keepdims=True) + EPS) * sw_ref[...]
    o_ref[...] = (y * one_minus_lam_init).astype(o_ref.dtype)


def _attention(qkv_q, qkv_c, qkv_l, lam_l, subln_w, lam_init, batch, nq_len, nc_len, nl_len, tq, tk):
    with_latent = qkv_l is not None
    nq = nq_len // tq
    hq, hk, hv = (c * (CB // LANE) for c in (COL_Q, COL_K, COL_V))
    in_specs = [
        pl.BlockSpec((tq, LANE), lambda b, h, i: (b * nq + i, hq + h)),
        pl.BlockSpec((nc_len, LANE), lambda b, h, i: (b, hk + h)),
        pl.BlockSpec((nc_len, LANE), lambda b, h, i: (b, hv + h)),
    ]
    args = [qkv_q, qkv_c, qkv_c]
    if with_latent:
        in_specs += [
            pl.BlockSpec((nl_len, LANE), lambda b, h, i: (b, hk + h)),
            pl.BlockSpec((nl_len, LANE), lambda b, h, i: (b, hv + h)),
        ]
        args += [qkv_l, qkv_l]
    in_specs += [
        pl.BlockSpec((SUBLANE, LANE), lambda b, h, i: (0, 0)),
        pl.BlockSpec((1, ATT_V), lambda b, h, i: (0, 0)),
    ]
    args += [lam_l, subln_w]
    n_keys = nc_len + nl_len
    return pl.pallas_call(
        functools.partial(_attn_kernel, with_latent=with_latent,
                          one_minus_lam_init=1.0 - lam_init, n_ctx=nc_len, n_lat=nl_len, tk=tk),
        grid=(batch, ATT_HEADS, nq),
        in_specs=in_specs,
        out_specs=pl.BlockSpec((tq, ATT_V), lambda b, h, i: (b * nq + i, h)),
        out_shape=jax.ShapeDtypeStruct((batch * nq_len, ATT_HEADS * ATT_V), BF16),
        scratch_shapes=[
            pltpu.VMEM((2, n_keys, LANE), BF16),
            pltpu.VMEM((n_keys, 2 * ATT_V), BF16),
            pltpu.VMEM((2, tq, LANE), F32),
            pltpu.VMEM((2, tq, 2 * ATT_V), F32),
        ],
        compiler_params=_params("parallel", "parallel", "arbitrary"),
        name="diff_attention",
    )(*args)


def _fourier_kernel(x_ref, cn_ref, sn_ref, cc_ref, sc_ref, o_ref, acc_ref, *, scale):
    k = pl.program_id(1)
    xb = x_ref[...]
    xc = _dot(xb, cc_ref[...]).astype(BF16)
    xs = _dot(xb, sc_ref[...]).astype(BF16)
    part = _dot(cn_ref[...], xc) - _dot(sn_ref[...], xs)

    @pl.when(k == 0)
    def _():
        acc_ref[...] = part

    @pl.when(k > 0)
    def _():
        acc_ref[...] += part

    @pl.when(k == pl.num_programs(1) - 1)
    def _():
        o_ref[...] = (acc_ref[...] * scale).astype(o_ref.dtype)


def _dft_tables(n):
    idx = jnp.arange(n, dtype=jnp.int32)
    ang = ((idx[:, None] * idx[None, :]) % n).astype(F32) * (2.0 * math.pi / n)
    return jnp.cos(ang), jnp.sin(ang)


def _fourier(p, cn, sn, ccd, scd, batch, n, tk):
    nk = n // tk
    scale = 1.0 / math.sqrt(n * F_GROUP_W)
    return pl.pallas_call(
        functools.partial(_fourier_kernel, scale=scale),
        grid=(batch, nk),
        in_specs=[
            pl.BlockSpec((tk, CB), lambda b, k: (b * nk + k, COL_FU)),
            pl.BlockSpec((n, tk), lambda b, k: (0, k)),
            pl.BlockSpec((n, tk), lambda b, k: (0, k)),
            pl.BlockSpec((F_W, F_W), lambda b, k: (0, 0)),
            pl.BlockSpec((F_W, F_W), lambda b, k: (0, 0)),
        ],
        out_specs=pl.BlockSpec((n, F_W), lambda b, k: (b, 0)),
        out_shape=jax.ShapeDtypeStruct((batch * n, F_W), BF16),
        scratch_shapes=[pltpu.VMEM((n, F_W), F32)],
        compiler_params=_params("parallel", "arbitrary"),
        name="fourier_mix",
    )(p, cn, sn, ccd, scd)


def _softplus(x):
    return jnp.maximum(x, 0.0) + jnp.log1p(jnp.exp(-jnp.abs(x)))


def _ssd_kernel(xsf_ref, bcf_ref, dtf_ref, xsb_ref, bcb_ref, dtb_ref, alog_ref, dtbias_ref,
                e_ref, et_ref, init_ref, yf_ref, yb_ref, fin_ref, state_ref):
    c = pl.program_id(1)
    q = xsf_ref.shape[0]
    gw = D_INNER // SSD_GROUPS
    hpg = SSD_HEADS // SSD_GROUPS

    @pl.when(c == 0)
    def _():
        state_ref[...] = init_ref[0]

    row = lax.broadcasted_iota(jnp.int32, (q, q), 0)
    col = lax.broadcasted_iota(jnp.int32, (q, q), 1)
    head_of_lane = lax.broadcasted_iota(jnp.int32, (q, gw), 1) // SSD_P
    neg_a = -jnp.exp(alog_ref[...])

    dirs = ((xsf_ref, bcf_ref, dtf_ref, yf_ref), (xsb_ref, bcb_ref, dtb_ref, yb_ref))
    for d, (xs_ref, bc_ref, dt_ref, y_ref) in enumerate(dirs):
        fwd = d == 0
        tri = (row >= col) if fwd else (row <= col)
        tri_b = tri.astype(F32).astype(BF16)
        dt = _softplus(dt_ref[...] + dtbias_ref[...])
        acs = _dot_sel_l(tri_b, dt * neg_a)
        acs_t = acs.T
        last = q - 1 if fwd else 0
        tot_row = acs[last:last + 1, :]
        tot_col = acs_t[:, last:last + 1]
        e = e_ref[d]
        dt_w = _dot_sel_r(dt, e)
        eacs_w = _dot_sel_r(jnp.exp(acs), e)
        dec_w = _dot_sel_r(jnp.exp(tot_row - acs), e)
        sdec = _dot_sel_l(et_ref[d], jnp.broadcast_to(jnp.exp(tot_col), (LANE, D_STATE)))
        xd = xs_ref[...].astype(F32) * dt_w
        xdb = xd.astype(BF16)
        xdd_t = (xd * dec_w).T.astype(BF16)
        bc = bc_ref[...]
        for g in range(SSD_GROUPS):
            bg = bc[:, g * D_STATE:(g + 1) * D_STATE].astype(BF16)
            cg = bc[:, (SSD_GROUPS + g) * D_STATE:(SSD_GROUPS + g + 1) * D_STATE].astype(BF16)
            cb = _dot_nt(cg, bg)
            lanes = slice(g * gw, (g + 1) * gw)
            s_g = state_ref[d, lanes, :]
            y = _dot_nt(cg, s_g.astype(BF16)) * eacs_w[:, lanes]
            xg = xdb[:, lanes]
            for r in range(hpg):
                jl = d * SSD_HEADS + g * hpg + r
                seg = jnp.where(tri, acs[:, jl:jl + 1] - acs_t[jl:jl + 1, :], -jnp.inf)
                mat = (cb * jnp.exp(seg)).astype(BF16)
                y = y + _dot(mat, jnp.where(head_of_lane == r, xg, jnp.zeros_like(xg)))
            y_ref[:, lanes] = y.astype(y_ref.dtype)
            state_ref[d, lanes, :] = s_g * sdec[lanes, :] + _dot(xdd_t[lanes, :], bg)

    @pl.when(c == pl.num_programs(1) - 1)
    def _():
        fin_ref[0] = state_ref[...]


def _ssd(act, dt, alog, dtbias, emat, emat_t, init, batch, n):
    nc = n // CHUNK
    fidx = lambda b, c: b * nc + c
    bidx = lambda b, c: b * nc + (nc - 1 - c)
    st_shape = (2, D_INNER, D_STATE)
    return pl.pallas_call(
        _ssd_kernel,
        grid=(batch, nc),
        in_specs=[
            pl.BlockSpec((CHUNK, CB), lambda b, c: (fidx(b, c), 0)),
            pl.BlockSpec((CHUNK, CB), lambda b, c: (fidx(b, c), 1)),
            pl.BlockSpec((CHUNK, LANE), lambda b, c: (fidx(b, c), 0)),
            pl.BlockSpec((CHUNK, CB), lambda b, c: (bidx(b, c), 0)),
            pl.BlockSpec((CHUNK, CB), lambda b, c: (bidx(b, c), 1)),
            pl.BlockSpec((CHUNK, LANE), lambda b, c: (bidx(b, c), 0)),
            pl.BlockSpec((1, LANE), lambda b, c: (0, 0)),
            pl.BlockSpec((1, LANE), lambda b, c: (0, 0)),
            pl.BlockSpec((2, LANE, D_INNER), lambda b, c: (0, 0, 0)),
            pl.BlockSpec((2, D_INNER, LANE), lambda b, c: (0, 0, 0)),
            pl.BlockSpec((1,) + st_shape, lambda b, c: (b, 0, 0, 0)),
        ],
        out_specs=[
            pl.BlockSpec((CHUNK, D_INNER), lambda b, c: (fidx(b, c), 0)),
            pl.BlockSpec((CHUNK, D_INNER), lambda b, c: (bidx(b, c), 0)),
            pl.BlockSpec((1,) + st_shape, lambda b, c: (b, 0, 0, 0)),
        ],
        out_shape=[
            jax.ShapeDtypeStruct((batch * n, D_INNER), BF16),
            jax.ShapeDtypeStruct((batch * n, D_INNER), BF16),
            jax.ShapeDtypeStruct((batch,) + st_shape, F32),
        ],
        scratch_shapes=[pltpu.VMEM(st_shape, F32)],
        compiler_params=_params("parallel", "arbitrary"),
        name="ssd_scan",
    )(act, act, dt, act, act, dt, alog, dtbias, emat, emat_t, init)


def _merge_kernel(four_ref, att_ref, yf_ref, yb_ref, xs_ref, fg_ref, ag_ref, z_ref,
                  g0_ref, g1_ref, g2_ref, x_ref, mod_ref, dskip_ref, snw_ref,
                  wof_ref, woa_ref, wos_ref, wout_ref, nf_ref, o_ref, *, final_norm):
    f = lambda ref: ref[...].astype(F32)
    y_f = _dot((f(four_ref) * _silu(f(fg_ref))).astype(BF16), wof_ref[...])
    y_a = _dot((f(att_ref) * _silu(f(ag_ref))).astype(BF16), woa_ref[...])
    ys = f(yf_ref) + f(yb_ref) + dskip_ref[...] * f(xs_ref)
    t = ys * _silu(f(z_ref))
    t = t * lax.rsqrt(jnp.mean(t * t, axis=-1, keepdims=True) + EPS) * snw_ref[...]
    y_s = _dot(t.astype(BF16), wos_ref[...])
    y = (jax.nn.sigmoid(f(g0_ref)) * y_f + jax.nn.sigmoid(f(g1_ref)) * y_a
         + jax.nn.sigmoid(f(g2_ref)) * y_s)
    out = _dot(y.astype(BF16), wout_ref[...])
    xn = x_ref[...] + mod_ref[0][:, 2 * D_MODEL:3 * D_MODEL] * out
    if final_norm:
        xn = xn * lax.rsqrt(jnp.mean(xn * xn, axis=-1, keepdims=True) + EPS) * nf_ref[...]
    o_ref[...] = xn


def _merge(four, att, yf, yb, act, p, x2, mod, dskip_w, snw, wof, woa, wos, wout, norm_f,
           n, tm, final_norm):
    rows = x2.shape[0]
    tiles_per_b = n // tm
    gcol = COL_GATES * CB // D_MODEL
    row_blk = lambda w, cidx: pl.BlockSpec((tm, w), lambda i: (i, cidx))
    const = lambda shape: pl.BlockSpec(shape, lambda i: (0,) * len(shape))
    return pl.pallas_call(
        functools.partial(_merge_kernel, final_norm=final_norm),
        grid=(rows // tm,),
        in_specs=[
            row_blk(F_W, 0), row_blk(CB, 0), row_blk(D_INNER, 0), row_blk(D_INNER, 0),
            row_blk(CB, 0),
            row_blk(CB, COL_FG), row_blk(CB, COL_AG), row_blk(CB, COL_Z),
            row_blk(D_MODEL, gcol), row_blk(D_MODEL, gcol + 1), row_blk(D_MODEL, gcol + 2),
            row_blk(D_MODEL, 0),
            pl.BlockSpec((1, 1, 3 * D_MODEL), lambda i: (i // tiles_per_b, 0, 0)),
            const((1, D_INNER)), const((1, D_INNER)),
            const((F_W, D_MODEL)), const((CB, D_MODEL)), const((D_INNER, D_MODEL)),
            const((D_MODEL, D_MODEL)), const((1, D_MODEL)),
        ],
        out_specs=pl.BlockSpec((tm, D_MODEL), lambda i: (i, 0)),
        out_shape=jax.ShapeDtypeStruct((rows, D_MODEL), F32),
        compiler_params=_params("parallel"),
        name="branch_merge",
    )(four, att, yf, yb, act, p, p, p, p, p, p, x2, mod, dskip_w, snw, wof, woa, wos, wout, norm_f)


def _rope_tables(n):
    rows = n // GRID_W
    row = jnp.repeat(jnp.arange(rows, dtype=F32), GRID_W)
    col = jnp.tile(jnp.arange(GRID_W, dtype=F32), rows)
    freqs = ROPE_BASE ** (-jnp.arange(0, ROPE_AXIS, 2, dtype=F32) / ROPE_AXIS)
    ang_r = row[:, None] * freqs
    ang_c = col[:, None] * freqs
    ang = jnp.concatenate([ang_r, ang_r, ang_c, ang_c], axis=-1)
    reps = CB // ATT_QK
    return jnp.tile(jnp.cos(ang), (1, reps)), jnp.tile(jnp.sin(ang), (1, reps))


def _head_expanders():
    e = np.zeros((2, LANE, D_INNER), np.float32)
    for d in range(2):
        for h in range(SSD_HEADS):
            e[d, d * SSD_HEADS + h, h * SSD_P:(h + 1) * SSD_P] = 1.0
    return jnp.asarray(e, BF16), jnp.asarray(e.transpose(0, 2, 1), BF16)


def _group_dft(w):
    cw, sw = _dft_tables(F_GROUP_W)
    eye = jnp.eye(w // F_GROUP_W, dtype=F32)
    return jnp.kron(eye, cw).astype(BF16), jnp.kron(eye, sw).astype(BF16)


def kernel(x, c, ctx, c_ctx, w_mod, b_mod, norm_w, w_in, conv_w, conv_b, a_log, dt_bias, d_skip,
           ssd_norm_w, lam, subln_w, w_of, w_oa, w_os, w_out, norm_f):
    batch, n, _ = x.shape
    n_ctx = ctx.shape[1]
    depth = w_mod.shape[0]
    assert n % GRID_W == 0 and n % CHUNK == 0 and n_ctx % CHUNK == 0

    o_q = 2 * F_W
    o_ag = o_q + 3 * CB
    o_xbc = o_ag + CB + D_INNER
    o_dt = o_xbc + XBC_W
    o_gt = o_dt + DT_W
    w_main = jnp.concatenate([w_in[:, :, o_xbc:o_dt], w_in[:, :, o_gt:], w_in[:, :, :o_q],
                              w_in[:, :, o_ag:o_xbc], w_in[:, :, o_q:o_ag]], axis=-1).astype(BF16)
    w_dt = jnp.pad(w_in[:, :, o_dt:o_gt], ((0, 0), (0, 0), (0, LANE - DT_W))).astype(BF16)
    conv_w8 = jnp.pad(conv_w, ((0, 0), (0, SUBLANE - CONV_W), (0, 0)))
    pad_lanes = lambda a: jnp.pad(a.reshape(depth, 1, DT_W), ((0, 0), (0, 0), (0, LANE - DT_W)))
    alog_p, dtbias_p = pad_lanes(a_log), pad_lanes(dt_bias)
    dskip_w = jnp.repeat(d_skip, SSD_P, axis=-1).reshape(depth, 1, D_INNER)
    wof_b, woa_b, wos_b, wout_b = (w.astype(BF16) for w in (w_of, w_oa, w_os, w_out))
    lam_inits = [0.8 - 0.6 * math.exp(-0.3 * l) for l in range(depth)]
    linit = jnp.asarray(np.broadcast_to(np.asarray(lam_inits, np.float32)[:, None, None],
                                        (depth, SUBLANE, LANE)))

    cos_t, sin_t = _rope_tables(n)
    cn_l, sn_l = (t.astype(BF16) for t in _dft_tables(n))
    cn_c, sn_c = (t.astype(BF16) for t in _dft_tables(n_ctx))
    ccd, scd = _group_dft(F_W)
    emat, emat_t = _head_expanders()
    zero_state = jnp.zeros((batch, 2, D_INNER, D_STATE), F32)

    mod_rows = -(-(batch + 1) // SUBLANE) * SUBLANE
    cc = jnp.concatenate([c, c_ctx[None, :], jnp.zeros((mod_rows - batch - 1, D_MODEL), F32)], axis=0)
    mod_all, lam_all = _modulation(cc, w_mod, b_mod.reshape(depth, 1, 3 * D_MODEL), lam, linit)

    tm_l = min(n, 1024)
    tq_l = min(n, 512)
    tk_l = min(n, 1024)
    tf_l = min(n, 512)
    tmerge_l = min(n, 256)
    rows_c = batch * n_ctx
    tm_c = math.gcd(rows_c, 1024)
    xl = x.reshape(batch * n, D_MODEL)
    xc = ctx.reshape(batch * n_ctx, D_MODEL)
    for l in range(depth):
        last = l == depth - 1
        mod_l = mod_all[l, :batch].reshape(batch, 1, 3 * D_MODEL)
        mod_c = mod_all[l, batch].reshape(1, 1, 3 * D_MODEL)
        nw = norm_w[l].reshape(1, D_MODEL)
        p_c, dt_c = _inproj(xc, mod_c, nw, w_main[l], w_dt[l], None, None, rows_c, tm_c)
        p_l, dt_l = _inproj(xl, mod_l, nw, w_main[l], w_dt[l], cos_t, sin_t, n, tm_l)
        cb_ = conv_b[l].reshape(1, XBC_W)
        act_c = _conv_silu(p_c, conv_w8[l], cb_, n_ctx, n_ctx)
        act_l = _conv_silu(p_l, conv_w8[l], cb_, n, min(n, 512))
        yf_c, yb_c, st_c = _ssd(act_c, dt_c, alog_p[l], dtbias_p[l], emat, emat_t, zero_state,
                                batch, n_ctx)
        yf_l, yb_l, _ = _ssd(act_l, dt_l, alog_p[l], dtbias_p[l], emat, emat_t, st_c, batch, n)
        sw = subln_w[l].reshape(1, ATT_V)
        att_l = _attention(p_l, p_c, p_l, lam_all[l], sw, lam_inits[l], batch, n, n_ctx, n,
                           tq_l, tk_l)
        four_l = _fourier(p_l, cn_l, sn_l, ccd, scd, batch, n, tf_l)
        merge_w = (dskip_w[l], ssd_norm_w[l].reshape(1, D_INNER), wof_b[l], woa_b[l], wos_b[l],
                   wout_b[l], norm_f.reshape(1, D_MODEL))
        if not last:
            att_c = _attention(p_c, p_c, None, lam_all[l], sw, lam_inits[l], batch, n_ctx, n_ctx, 0,
                               n_ctx, 0)
            four_c = _fourier(p_c, cn_c, sn_c, ccd, scd, batch, n_ctx, n_ctx)
            xc = _merge(four_c, att_c, yf_c, yb_c, act_c, p_c, xc, mod_c, *merge_w,
                        rows_c, n_ctx, False)
        xl = _merge(four_l, att_l, yf_l, yb_l, act_l, p_l, xl, mod_l, *merge_w, n, tmerge_l, last)
    return xl.reshape(batch, n, D_MODEL)
```

```python
import functools
import math

import numpy as np
import jax
import jax.numpy as jnp
from jax import lax
from jax.experimental import pallas as pl
from jax.experimental.pallas import tpu as pltpu

F32 = jnp.float32
BF16 = jnp.bfloat16
EPS = 1e-6

D_MODEL = 1024
GRID_W = 64
F_GROUP_W = 128
F_W = 512
ATT_HEADS = 4
ATT_QK = 64
ATT_V = 128
ROPE_AXIS = 32
ROPE_BASE = 10000.0
D_INNER = 512
SSD_P = 64
SSD_HEADS = 8
SSD_GROUPS = 2
D_STATE = 128
CONV_W = 5
CHUNK = 128
XBC_W = 1024
DT_W = 16
MERGE_W = 3072

LANE = 128
SUBLANE = 8
VMEM_LIMIT = 56 * 1024 * 1024

SUBLANE_BF16 = 16
CB = 512
COL_XBC, COL_GATES, COL_FU, COL_FG, COL_AG, COL_Z = 0, 2, 8, 9, 10, 11
COL_Q, COL_K, COL_V = 12, 13, 14
N_COLB = 15
N_MAIN = N_COLB * CB
PROJ_STEP_B = 3
PROJ_STEPS = N_COLB // PROJ_STEP_B
assert COL_Q == (PROJ_STEPS - 1) * PROJ_STEP_B and COL_V == N_COLB - 1
Q_SCALE = ATT_QK ** -0.5 * math.log2(math.e)


def _dot(a, b):
    return jnp.dot(a, b, preferred_element_type=F32)


def _dot_nt(a, b):
    return lax.dot_general(a, b, (((1,), (1,)), ((), ())), preferred_element_type=F32)


def _split3(x):
    x1 = x.astype(BF16)
    r1 = x - x1.astype(F32)
    x2 = r1.astype(BF16)
    x3 = (r1 - x2.astype(F32)).astype(BF16)
    return x1, x2, x3


def _dot_sel_r(x, sel):
    x1, x2, x3 = _split3(x)
    return _dot(x1, sel) + _dot(x2, sel) + _dot(x3, sel)


def _dot_sel_l(sel, x):
    x1, x2, x3 = _split3(x)
    return _dot(sel, x1) + _dot(sel, x2) + _dot(sel, x3)


def _dot_sel_r2(x, sel):
    x1 = x.astype(BF16)
    x2 = (x - x1.astype(F32)).astype(BF16)
    return _dot(x1, sel) + _dot(x2, sel)


def _sigmoid(x):
    return 0.5 * jnp.tanh(0.5 * x) + 0.5


def _silu(x):
    return x * _sigmoid(x)


def _params(*sem):
    return pltpu.CompilerParams(dimension_semantics=sem, vmem_limit_bytes=VMEM_LIMIT)


def _mod_kernel(cc_ref, w_ref, b_ref, lam_ref, linit_ref, mod_ref, lam_out_ref):
    s = _silu(cc_ref[...])
    mod_ref[0] = jnp.dot(s, w_ref[0], precision=lax.Precision.HIGHEST,
                         preferred_element_type=F32) + b_ref[0]
    lp = lam_ref[0]
    s1 = jnp.sum(lp[0:1] * lp[1:2], axis=-1, keepdims=True)
    s2 = jnp.sum(lp[2:3] * lp[3:4], axis=-1, keepdims=True)
    lam_out_ref[0] = jnp.broadcast_to(jnp.exp(s1) - jnp.exp(s2), (SUBLANE, LANE)) + linit_ref[0]


def _modulation(cc, w_mod, b_mod, lam, linit):
    depth = w_mod.shape[0]
    rows = cc.shape[0]
    tn = D_MODEL
    return pl.pallas_call(
        _mod_kernel,
        grid=(depth, 3 * D_MODEL // tn),
        in_specs=[
            pl.BlockSpec((rows, D_MODEL), lambda l, j: (0, 0)),
            pl.BlockSpec((1, D_MODEL, tn), lambda l, j: (l, 0, j)),
            pl.BlockSpec((1, 1, tn), lambda l, j: (l, 0, j)),
            pl.BlockSpec((1, 4, ATT_QK), lambda l, j: (l, 0, 0)),
            pl.BlockSpec((1, SUBLANE, LANE), lambda l, j: (l, 0, 0)),
        ],
        out_specs=[
            pl.BlockSpec((1, rows, tn), lambda l, j: (l, 0, j)),
            pl.BlockSpec((1, SUBLANE, LANE), lambda l, j: (l, 0, 0)),
        ],
        out_shape=[
            jax.ShapeDtypeStruct((depth, rows, 3 * D_MODEL), F32),
            jax.ShapeDtypeStruct((depth, SUBLANE, LANE), F32),
        ],
        compiler_params=_params("arbitrary", "arbitrary"),
        name="adaln_mod",
    )(cc, w_mod, b_mod, lam, linit)


def _rope(t, cos, sin):
    w = t.shape[-1]
    lane = lax.broadcasted_iota(jnp.int32, t.shape, 1)
    first = (lane % ROPE_AXIS) < (ROPE_AXIS // 2)
    rot = jnp.where(first, -pltpu.roll(t, w - ROPE_AXIS // 2, 1), pltpu.roll(t, ROPE_AXIS // 2, 1))
    return t * cos + rot * sin


def _inproj_kernel(*refs, rope):
    if rope:
        x_ref, mod_ref, nw_ref, w_ref, wdt_ref, cos_ref, sin_ref, p_ref, dt_ref, h_ref = refs
    else:
        x_ref, mod_ref, nw_ref, w_ref, wdt_ref, p_ref, dt_ref, h_ref = refs
    j = pl.program_id(1)

    @pl.when(j == 0)
    def _():
        x = x_ref[...]
        y = x * lax.rsqrt(jnp.mean(x * x, axis=-1, keepdims=True) + EPS) * nw_ref[...]
        m = mod_ref[0]
        h = y * (1.0 + m[:, D_MODEL:2 * D_MODEL]) + m[:, 0:D_MODEL]
        hb = h.astype(BF16)
        h_ref[...] = hb
        dt_ref[...] = _dot(hb, wdt_ref[...])

    acc = _dot(h_ref[...], w_ref[...])

    @pl.when(j < PROJ_STEPS - 1)
    def _():
        p_ref[...] = acc.astype(BF16)

    @pl.when(j == PROJ_STEPS - 1)
    def _():
        q, k, v = (acc[:, b * CB:(b + 1) * CB] for b in range(PROJ_STEP_B))
        if rope:
            cos, sin = cos_ref[...], sin_ref[...]
            q, k = _rope(q, cos, sin), _rope(k, cos, sin)
        p_ref[:, 0:CB] = (q * Q_SCALE).astype(BF16)
        p_ref[:, CB:2 * CB] = k.astype(BF16)
        p_ref[:, 2 * CB:3 * CB] = v.astype(BF16)


def _inproj(x2, mod, norm_w, w_main, w_dt, cos, sin, n, tm):
    rows = x2.shape[0]
    tiles_per_b = n // tm
    rope = cos is not None
    ws = PROJ_STEP_B * CB
    in_specs = [
        pl.BlockSpec((tm, D_MODEL), lambda i, j: (i, 0)),
        pl.BlockSpec((1, 1, 3 * D_MODEL), lambda i, j: (i // tiles_per_b, 0, 0)),
        pl.BlockSpec((1, D_MODEL), lambda i, j: (0, 0)),
        pl.BlockSpec((D_MODEL, ws), lambda i, j: (0, j)),
        pl.BlockSpec((D_MODEL, LANE), lambda i, j: (0, 0)),
    ]
    args = [x2, mod, norm_w, w_main, w_dt]
    if rope:
        in_specs += [pl.BlockSpec((tm, CB), lambda i, j: (i % tiles_per_b, 0))] * 2
        args += [cos, sin]
    return pl.pallas_call(
        functools.partial(_inproj_kernel, rope=rope),
        grid=(rows // tm, PROJ_STEPS),
        in_specs=in_specs,
        out_specs=[
            pl.BlockSpec((tm, ws), lambda i, j: (i, j)),
            pl.BlockSpec((tm, LANE), lambda i, j: (i, 0)),
        ],
        out_shape=[
            jax.ShapeDtypeStruct((rows, N_MAIN), BF16),
            jax.ShapeDtypeStruct((rows, LANE), F32),
        ],
        scratch_shapes=[pltpu.VMEM((tm, D_MODEL), BF16)],
        compiler_params=_params("parallel", "arbitrary"),
        name="inproj",
    )(*args)


HALO = SUBLANE_BF16


def _conv_kernel(cur_ref, prev_ref, next_ref, w_ref, b_ref, o_ref, ext_ref, *, tiles_per_b):
    i = pl.program_id(0)
    tc = cur_ref.shape[0]
    t = i % tiles_per_b
    ext_ref[0:HALO, :] = jnp.where(t == 0, 0.0, prev_ref[...].astype(F32))
    ext_ref[HALO:HALO + tc, :] = cur_ref[...].astype(F32)
    ext_ref[HALO + tc:2 * HALO + tc, :] = jnp.where(t == tiles_per_b - 1, 0.0,
                                                    next_ref[...].astype(F32))
    acc = jnp.broadcast_to(b_ref[...], o_ref.shape)
    for k in range(CONV_W):
        acc = acc + ext_ref[pl.ds(HALO - CONV_W // 2 + k, tc), :] * w_ref[k:k + 1, :]
    o_ref[...] = _silu(acc).astype(o_ref.dtype)


def _conv_silu(p, conv_w8, conv_b, n, tc):
    rows = p.shape[0]
    tiles_per_b = n // tc
    hb = tc // HALO
    last_hb = rows // HALO - 1
    return pl.pallas_call(
        functools.partial(_conv_kernel, tiles_per_b=tiles_per_b),
        grid=(rows // tc, XBC_W // CB),
        in_specs=[
            pl.BlockSpec((tc, CB), lambda i, j: (i, COL_XBC + j)),
            pl.BlockSpec((HALO, CB), lambda i, j: (jnp.maximum(i * hb - 1, 0), COL_XBC + j)),
            pl.BlockSpec((HALO, CB), lambda i, j: (jnp.minimum((i + 1) * hb, last_hb), COL_XBC + j)),
            pl.BlockSpec((SUBLANE, CB), lambda i, j: (0, j)),
            pl.BlockSpec((1, CB), lambda i, j: (0, j)),
        ],
        out_specs=pl.BlockSpec((tc, CB), lambda i, j: (i, j)),
        out_shape=jax.ShapeDtypeStruct((rows, XBC_W), BF16),
        scratch_shapes=[pltpu.VMEM((tc + 2 * HALO, CB), F32)],
        compiler_params=_params("parallel", "parallel"),
        name="conv_silu",
    )(p, p, p, conv_w8, conv_b)


def _attn_kernel(*refs, with_latent, one_minus_lam_init, n_ctx, chunks):
    if with_latent:
        (q_ref, kc_ref, vc_ref, kl_ref, vl_ref, lam_ref, sw_ref, o_ref,
         k_scr, v_scr, m_scr, acc_scr) = refs
    else:
        q_ref, kc_ref, vc_ref, lam_ref, sw_ref, o_ref, k_scr, v_scr, m_scr, acc_scr = refs
    tq = q_ref.shape[0]

    @pl.when(pl.program_id(2) == 0)
    def _():
        def put(k_ref, v_ref, off):
            rows = k_ref.shape[0]
            k = k_ref[...]
            lane = lax.broadcasted_iota(jnp.int32, k.shape, 1)
            zero = jnp.zeros_like(k)
            k_scr[0, off:off + rows, :] = jnp.where(lane < ATT_QK, k, zero)
            k_scr[1, off:off + rows, :] = jnp.where(lane >= ATT_QK, k, zero)
            v_scr[off:off + rows, 0:ATT_V] = v_ref[...]
            v_scr[off:off + rows, ATT_V:2 * ATT_V] = jnp.ones((rows, ATT_V), BF16)
        put(kc_ref, vc_ref, 0)
        if with_latent:
            put(kl_ref, vl_ref, n_ctx)

    m_scr[...] = jnp.full(m_scr.shape, -jnp.inf, F32)
    acc_scr[...] = jnp.zeros(acc_scr.shape, F32)
    q = q_ref[...]

    def chunk(off, size):
        v = v_scr[pl.ds(off, size), :]
        nt = size // LANE
        for mi in range(2):
            s = _dot_nt(q, k_scr[mi, pl.ds(off, size), :])
            tiles = [s[:, t * LANE:(t + 1) * LANE] for t in range(nt)]
            mx = functools.reduce(jnp.maximum, tiles)
            m_prev = m_scr[mi]
            m_new = jnp.maximum(m_prev, jnp.max(mx, axis=-1, keepdims=True))
            p = jnp.concatenate([jnp.exp2(t - m_new) for t in tiles], axis=1).astype(BF16)
            alpha = jnp.exp2(m_prev - m_new)
            acc_scr[mi] = jnp.concatenate([alpha, alpha], axis=1) * acc_scr[mi] + _dot(p, v)
            m_scr[mi] = m_new

    for off, size in chunks:
        chunk(off, size)

    lam = lam_ref[0:1, :]
    a0, a1 = acc_scr[0], acc_scr[1]
    o = a0[:, 0:ATT_V] / a0[:, ATT_V:] - lam * (a1[:, 0:ATT_V] / a1[:, ATT_V:])
    y = o * lax.rsqrt(jnp.mean(o * o, axis=-1, keepdims=True) + EPS) * sw_ref[...]
    o_ref[...] = (y * one_minus_lam_init).astype(o_ref.dtype)


def _attention(qkv_q, qkv_c, qkv_l, lam_l, subln_w, lam_init, batch, nq_len, nc_len, nl_len, tq, tk):
    with_latent = qkv_l is not None
    nq = nq_len // tq
    hq, hk, hv = (c * (CB // LANE) for c in (COL_Q, COL_K, COL_V))
    in_specs = [
        pl.BlockSpec((tq, LANE), lambda b, h, i: (b * nq + i, hq + h)),
        pl.BlockSpec((nc_len, LANE), lambda b, h, i: (b, hk + h)),
        pl.BlockSpec((nc_len, LANE), lambda b, h, i: (b, hv + h)),
    ]
    args = [qkv_q, qkv_c, qkv_c]
    if with_latent:
        in_specs += [
            pl.BlockSpec((nl_len, LANE), lambda b, h, i: (b, hk + h)),
            pl.BlockSpec((nl_len, LANE), lambda b, h, i: (b, hv + h)),
        ]
        args += [qkv_l, qkv_l]
    in_specs += [
        pl.BlockSpec((SUBLANE, LANE), lambda b, h, i: (0, 0)),
        pl.BlockSpec((1, ATT_V), lambda b, h, i: (0, 0)),
    ]
    args += [lam_l, subln_w]
    n_keys = nc_len + nl_len
    sizes = [nc_len + tk] + [tk] * (nl_len // tk - 1) if with_latent else [nc_len]
    chunks = tuple((sum(sizes[:t]), sizes[t]) for t in range(len(sizes)))
    assert sum(sizes) == n_keys
    return pl.pallas_call(
        functools.partial(_attn_kernel, with_latent=with_latent,
                          one_minus_lam_init=1.0 - lam_init, n_ctx=nc_len, chunks=chunks),
        grid=(batch, ATT_HEADS, nq),
        in_specs=in_specs,
        out_specs=pl.BlockSpec((tq, ATT_V), lambda b, h, i: (b * nq + i, h)),
        out_shape=jax.ShapeDtypeStruct((batch * nq_len, ATT_HEADS * ATT_V), BF16),
        scratch_shapes=[
            pltpu.VMEM((2, n_keys, LANE), BF16),
            pltpu.VMEM((n_keys, 2 * ATT_V), BF16),
            pltpu.VMEM((2, tq, LANE), F32),
            pltpu.VMEM((2, tq, 2 * ATT_V), F32),
        ],
        compiler_params=_params("parallel", "parallel", "arbitrary"),
        name="diff_attention",
    )(*args)


def _fourier_kernel(x_ref, cn_ref, sn_ref, cc_ref, sc_ref, o_ref, acc_ref, *, scale):
    k = pl.program_id(1)
    xb = x_ref[...]
    xc = _dot(xb, cc_ref[...]).astype(BF16)
    xs = _dot(xb, sc_ref[...]).astype(BF16)
    part = _dot(cn_ref[...], xc) - _dot(sn_ref[...], xs)

    @pl.when(k == 0)
    def _():
        acc_ref[...] = part

    @pl.when(k > 0)
    def _():
        acc_ref[...] += part

    @pl.when(k == pl.num_programs(1) - 1)
    def _():
        o_ref[...] = (acc_ref[...] * scale).astype(o_ref.dtype)


def _dft_tables(n):
    idx = jnp.arange(n, dtype=jnp.int32)
    ang = ((idx[:, None] * idx[None, :]) % n).astype(F32) * (2.0 * math.pi / n)
    return jnp.cos(ang), jnp.sin(ang)


def _fourier(p, cn, sn, ccd, scd, batch, n, tk):
    nk = n // tk
    scale = 1.0 / math.sqrt(n * F_GROUP_W)
    return pl.pallas_call(
        functools.partial(_fourier_kernel, scale=scale),
        grid=(batch, nk),
        in_specs=[
            pl.BlockSpec((tk, CB), lambda b, k: (b * nk + k, COL_FU)),
            pl.BlockSpec((n, tk), lambda b, k: (0, k)),
            pl.BlockSpec((n, tk), lambda b, k: (0, k)),
            pl.BlockSpec((F_W, F_W), lambda b, k: (0, 0)),
            pl.BlockSpec((F_W, F_W), lambda b, k: (0, 0)),
        ],
        out_specs=pl.BlockSpec((n, F_W), lambda b, k: (b, 0)),
        out_shape=jax.ShapeDtypeStruct((batch * n, F_W), BF16),
        scratch_shapes=[pltpu.VMEM((n, F_W), F32)],
        compiler_params=_params("parallel", "arbitrary"),
        name="fourier_mix",
    )(p, cn, sn, ccd, scd)


def _softplus(x):
    return jnp.maximum(x, 0.0) + jnp.log1p(jnp.exp(-jnp.abs(x)))


def _ssd_kernel(xsf_ref, bcf_ref, dtf_ref, xsb_ref, bcb_ref, dtb_ref, alog_ref, dtbias_ref,
                e_ref, init_ref, yf_ref, yb_ref, fin_ref, state_ref):
    c = pl.program_id(1)
    q = xsf_ref.shape[0]
    gw = D_INNER // SSD_GROUPS
    hpg = SSD_HEADS // SSD_GROUPS

    @pl.when(c == 0)
    def _():
        state_ref[...] = init_ref[0]

    row = lax.broadcasted_iota(jnp.int32, (q, q), 0)
    col = lax.broadcasted_iota(jnp.int32, (q, q), 1)
    head_of_lane = lax.broadcasted_iota(jnp.int32, (q, gw), 1) // SSD_P
    neg_a = -jnp.exp(alog_ref[...])

    dirs = ((xsf_ref, bcf_ref, dtf_ref, yf_ref), (xsb_ref, bcb_ref, dtb_ref, yb_ref))
    for d, (xs_ref, bc_ref, dt_ref, y_ref) in enumerate(dirs):
        fwd = d == 0
        tri = (row >= col) if fwd else (row <= col)
        tri_b = tri.astype(F32).astype(BF16)
        dt = _softplus(dt_ref[...] + dtbias_ref[...])
        acs = _dot_sel_l(tri_b, dt * neg_a)
        acs_t = acs.T
        last = q - 1 if fwd else 0
        tot_row = acs[last:last + 1, :]
        e = e_ref[d]
        dt_w = _dot_sel_r2(dt, e)
        eacs_w = _dot_sel_r2(jnp.exp(acs), e)
        dec_w = _dot_sel_r2(jnp.exp(tot_row - acs), e)
        sdec = _dot_sel_r(jnp.broadcast_to(jnp.exp(tot_row), (SUBLANE, LANE)), e)[0:1, :]
        xd = xs_ref[...].astype(F32) * dt_w
        xdb = xd.astype(BF16)
        xdd = (xd * dec_w).astype(BF16)
        bc = bc_ref[...]
        for g in range(SSD_GROUPS):
            bg = bc[:, g * D_STATE:(g + 1) * D_STATE]
            cg = bc[:, (SSD_GROUPS + g) * D_STATE:(SSD_GROUPS + g + 1) * D_STATE]
            cb = _dot_nt(cg, bg)
            lanes = slice(g * gw, (g + 1) * gw)
            s_g = state_ref[d, :, lanes]
            y = _dot(cg, s_g.astype(BF16)) * eacs_w[:, lanes]
            xg = xdb[:, lanes]
            for r in range(hpg):
                jl = d * SSD_HEADS + g * hpg + r
                seg = jnp.where(tri, acs[:, jl:jl + 1] - acs_t[jl:jl + 1, :], -jnp.inf)
                mat = (cb * jnp.exp(seg)).astype(BF16)
                y = y + _dot(mat, jnp.where(head_of_lane == r, xg, jnp.zeros_like(xg)))
            y_ref[:, lanes] = y.astype(y_ref.dtype)
            bg_t = bg.astype(F32).T.astype(BF16)
            state_ref[d, :, lanes] = s_g * sdec[:, lanes] + _dot(bg_t, xdd[:, lanes])

    @pl.when(c == pl.num_programs(1) - 1)
    def _():
        fin_ref[0] = state_ref[...]


def _ssd(act, dt, alog, dtbias, emat, init, batch, n):
    nc = n // CHUNK
    fidx = lambda b, c: b * nc + c
    bidx = lambda b, c: b * nc + (nc - 1 - c)
    st_shape = (2, D_STATE, D_INNER)
    return pl.pallas_call(
        _ssd_kernel,
        grid=(batch, nc),
        in_specs=[
            pl.BlockSpec((CHUNK, CB), lambda b, c: (fidx(b, c), 0)),
            pl.BlockSpec((CHUNK, CB), lambda b, c: (fidx(b, c), 1)),
            pl.BlockSpec((CHUNK, LANE), lambda b, c: (fidx(b, c), 0)),
            pl.BlockSpec((CHUNK, CB), lambda b, c: (bidx(b, c), 0)),
            pl.BlockSpec((CHUNK, CB), lambda b, c: (bidx(b, c), 1)),
            pl.BlockSpec((CHUNK, LANE), lambda b, c: (bidx(b, c), 0)),
            pl.BlockSpec((1, LANE), lambda b, c: (0, 0)),
            pl.BlockSpec((1, LANE), lambda b, c: (0, 0)),
            pl.BlockSpec((2, LANE, D_INNER), lambda b, c: (0, 0, 0)),
            pl.BlockSpec((1,) + st_shape, lambda b, c: (b, 0, 0, 0)),
        ],
        out_specs=[
            pl.BlockSpec((CHUNK, D_INNER), lambda b, c: (fidx(b, c), 0)),
            pl.BlockSpec((CHUNK, D_INNER), lambda b, c: (bidx(b, c), 0)),
            pl.BlockSpec((1,) + st_shape, lambda b, c: (b, 0, 0, 0)),
        ],
        out_shape=[
            jax.ShapeDtypeStruct((batch * n, D_INNER), BF16),
            jax.ShapeDtypeStruct((batch * n, D_INNER), BF16),
            jax.ShapeDtypeStruct((batch,) + st_shape, F32),
        ],
        scratch_shapes=[pltpu.VMEM(st_shape, F32)],
        compiler_params=_params("parallel", "arbitrary"),
        name="ssd_scan",
    )(act, act, dt, act, act, dt, alog, dtbias, emat, init)


def _merge_kernel(four_ref, att_ref, yf_ref, yb_ref, xs_ref, fg_ref, ag_ref, z_ref,
                  g0_ref, g1_ref, g2_ref, x_ref, mod_ref, dskip_ref, snw_ref,
                  wof_ref, woa_ref, wos_ref, wout_ref, nf_ref, o_ref, *, final_norm):
    f = lambda ref: ref[...].astype(F32)
    y_f = _dot((f(four_ref) * _silu(f(fg_ref))).astype(BF16), wof_ref[...])
    y_a = _dot((f(att_ref) * _silu(f(ag_ref))).astype(BF16), woa_ref[...])
    ys = f(yf_ref) + f(yb_ref) + dskip_ref[...] * f(xs_ref)
    t = ys * _silu(f(z_ref))
    t = t * lax.rsqrt(jnp.mean(t * t, axis=-1, keepdims=True) + EPS) * snw_ref[...]
    y_s = _dot(t.astype(BF16), wos_ref[...])
    y = _sigmoid(f(g0_ref)) * y_f + _sigmoid(f(g1_ref)) * y_a + _sigmoid(f(g2_ref)) * y_s
    out = _dot(y.astype(BF16), wout_ref[...])
    xn = x_ref[...] + mod_ref[0][:, 2 * D_MODEL:3 * D_MODEL] * out
    if final_norm:
        xn = xn * lax.rsqrt(jnp.mean(xn * xn, axis=-1, keepdims=True) + EPS) * nf_ref[...]
    o_ref[...] = xn


def _merge(four, att, yf, yb, act, p, x2, mod, dskip_w, snw, wof, woa, wos, wout, norm_f,
           n, tm, final_norm):
    rows = x2.shape[0]
    tiles_per_b = n // tm
    gcol = COL_GATES * CB // D_MODEL
    row_blk = lambda w, cidx: pl.BlockSpec((tm, w), lambda i: (i, cidx))
    const = lambda shape: pl.BlockSpec(shape, lambda i: (0,) * len(shape))
    return pl.pallas_call(
        functools.partial(_merge_kernel, final_norm=final_norm),
        grid=(rows // tm,),
        in_specs=[
            row_blk(F_W, 0), row_blk(CB, 0), row_blk(D_INNER, 0), row_blk(D_INNER, 0),
            row_blk(CB, 0),
            row_blk(CB, COL_FG), row_blk(CB, COL_AG), row_blk(CB, COL_Z),
            row_blk(D_MODEL, gcol), row_blk(D_MODEL, gcol + 1), row_blk(D_MODEL, gcol + 2),
            row_blk(D_MODEL, 0),
            pl.BlockSpec((1, 1, 3 * D_MODEL), lambda i: (i // tiles_per_b, 0, 0)),
            const((1, D_INNER)), const((1, D_INNER)),
            const((F_W, D_MODEL)), const((CB, D_MODEL)), const((D_INNER, D_MODEL)),
            const((D_MODEL, D_MODEL)), const((1, D_MODEL)),
        ],
        out_specs=pl.BlockSpec((tm, D_MODEL), lambda i: (i, 0)),
        out_shape=jax.ShapeDtypeStruct((rows, D_MODEL), F32),
        compiler_params=_params("parallel"),
        name="branch_merge",
    )(four, att, yf, yb, act, p, p, p, p, p, p, x2, mod, dskip_w, snw, wof, woa, wos, wout, norm_f)


def _rope_tables(n):
    rows = n // GRID_W
    row = jnp.repeat(jnp.arange(rows, dtype=F32), GRID_W)
    col = jnp.tile(jnp.arange(GRID_W, dtype=F32), rows)
    freqs = ROPE_BASE ** (-jnp.arange(0, ROPE_AXIS, 2, dtype=F32) / ROPE_AXIS)
    ang_r = row[:, None] * freqs
    ang_c = col[:, None] * freqs
    ang = jnp.concatenate([ang_r, ang_r, ang_c, ang_c], axis=-1)
    reps = CB // ATT_QK
    return jnp.tile(jnp.cos(ang), (1, reps)), jnp.tile(jnp.sin(ang), (1, reps))


def _head_expanders():
    e = np.zeros((2, LANE, D_INNER), np.float32)
    for d in range(2):
        for h in range(SSD_HEADS):
            e[d, d * SSD_HEADS + h, h * SSD_P:(h + 1) * SSD_P] = 1.0
    return jnp.asarray(e, BF16)


def _group_dft(w):
    cw, sw = _dft_tables(F_GROUP_W)
    eye = jnp.eye(w // F_GROUP_W, dtype=F32)
    return jnp.kron(eye, cw).astype(BF16), jnp.kron(eye, sw).astype(BF16)


def kernel(x, c, ctx, c_ctx, w_mod, b_mod, norm_w, w_in, conv_w, conv_b, a_log, dt_bias, d_skip,
           ssd_norm_w, lam, subln_w, w_of, w_oa, w_os, w_out, norm_f):
    batch, n, _ = x.shape
    n_ctx = ctx.shape[1]
    depth = w_mod.shape[0]
    assert n % GRID_W == 0 and n % CHUNK == 0 and n_ctx % CHUNK == 0

    o_q = 2 * F_W
    o_ag = o_q + 3 * CB
    o_xbc = o_ag + CB + D_INNER
    o_dt = o_xbc + XBC_W
    o_gt = o_dt + DT_W
    w_main = jnp.concatenate([w_in[:, :, o_xbc:o_dt], w_in[:, :, o_gt:], w_in[:, :, :o_q],
                              w_in[:, :, o_ag:o_xbc], w_in[:, :, o_q:o_ag]], axis=-1).astype(BF16)
    w_dt = jnp.pad(w_in[:, :, o_dt:o_gt], ((0, 0), (0, 0), (0, LANE - DT_W))).astype(BF16)
    conv_w8 = jnp.pad(conv_w, ((0, 0), (0, SUBLANE - CONV_W), (0, 0)))
    pad_lanes = lambda a: jnp.pad(a.reshape(depth, 1, DT_W), ((0, 0), (0, 0), (0, LANE - DT_W)))
    alog_p, dtbias_p = pad_lanes(a_log), pad_lanes(dt_bias)
    dskip_w = jnp.repeat(d_skip, SSD_P, axis=-1).reshape(depth, 1, D_INNER)
    wof_b, woa_b, wos_b, wout_b = (w.astype(BF16) for w in (w_of, w_oa, w_os, w_out))
    lam_inits = [0.8 - 0.6 * math.exp(-0.3 * l) for l in range(depth)]
    linit = jnp.asarray(np.broadcast_to(np.asarray(lam_inits, np.float32)[:, None, None],
                                        (depth, SUBLANE, LANE)))

    cos_t, sin_t = _rope_tables(n)
    cn_l, sn_l = (t.astype(BF16) for t in _dft_tables(n))
    cn_c, sn_c = (t.astype(BF16) for t in _dft_tables(n_ctx))
    ccd, scd = _group_dft(F_W)
    emat = _head_expanders()
    zero_state = jnp.zeros((batch, 2, D_STATE, D_INNER), F32)

    mod_rows = -(-(batch + 1) // SUBLANE) * SUBLANE
    cc = jnp.concatenate([c, c_ctx[None, :], jnp.zeros((mod_rows - batch - 1, D_MODEL), F32)], axis=0)
    mod_all, lam_all = _modulation(cc, w_mod, b_mod.reshape(depth, 1, 3 * D_MODEL), lam, linit)

    tm_l = min(n, 1024)
    tq_l = min(n, 512)
    tk_l = min(n, 1024)
    tf_l = min(n, 512)
    tmerge_l = min(n, 256)
    rows_c = batch * n_ctx
    tm_c = math.gcd(rows_c, 1024)
    xl = x.reshape(batch * n, D_MODEL)
    xc = ctx.reshape(batch * n_ctx, D_MODEL)
    for l in range(depth):
        last = l == depth - 1
        mod_l = mod_all[l, :batch].reshape(batch, 1, 3 * D_MODEL)
        mod_c = mod_all[l, batch].reshape(1, 1, 3 * D_MODEL)
        nw = norm_w[l].reshape(1, D_MODEL)
        p_c, dt_c = _inproj(xc, mod_c, nw, w_main[l], w_dt[l], None, None, rows_c, tm_c)
        p_l, dt_l = _inproj(xl, mod_l, nw, w_main[l], w_dt[l], cos_t, sin_t, n, tm_l)
        cb_ = conv_b[l].reshape(1, XBC_W)
        act_c = _conv_silu(p_c, conv_w8[l], cb_, n_ctx, n_ctx)
        act_l = _conv_silu(p_l, conv_w8[l], cb_, n, min(n, 1024))
        yf_c, yb_c, st_c = _ssd(act_c, dt_c, alog_p[l], dtbias_p[l], emat, zero_state, batch, n_ctx)
        yf_l, yb_l, _ = _ssd(act_l, dt_l, alog_p[l], dtbias_p[l], emat, st_c, batch, n)
        sw = subln_w[l].reshape(1, ATT_V)
        att_l = _attention(p_l, p_c, p_l, lam_all[l], sw, lam_inits[l], batch, n, n_ctx, n,
                           tq_l, tk_l)
        four_l = _fourier(p_l, cn_l, sn_l, ccd, scd, batch, n, tf_l)
        merge_w = (dskip_w[l], ssd_norm_w[l].reshape(1, D_INNER), wof_b[l], woa_b[l], wos_b[l],
                   wout_b[l], norm_f.reshape(1, D_MODEL))
        if not last:
            att_c = _attention(p_c, p_c, None, lam_all[l], sw, lam_inits[l], batch, n_ctx, n_ctx, 0,
                               n_ctx, 0)
            four_c = _fourier(p_c, cn_c, sn_c, ccd, scd, batch, n_ctx, n_ctx)
            xc = _merge(four_c, att_c, yf_c, yb_c, act_c, p_c, xc, mod_c, *merge_w,
                        rows_c, n_ctx, False)
        xl = _merge(four_l, att_l, yf_l, yb_l, act_l, p_l, xl, mod_l, *merge_w, n, tmerge_l, last)
    return xl.reshape(batch, n, D_MODEL)
```

```python
import functools
import math

import numpy as np
import jax
import jax.numpy as jnp
from jax import lax
from jax.experimental import pallas as pl
from jax.experimental.pallas import tpu as pltpu

F32 = jnp.float32
BF16 = jnp.bfloat16
EPS = 1e-6

D_MODEL = 1024
GRID_W = 64
F_GROUP_W = 128
F_W = 512
ATT_HEADS = 4
ATT_QK = 64
ATT_V = 128
ROPE_AXIS = 32
ROPE_BASE = 10000.0
D_INNER = 512
SSD_P = 64
SSD_HEADS = 8
SSD_GROUPS = 2
D_STATE = 128
CONV_W = 5
CHUNK = 128
XBC_W = 1024
DT_W = 16
MERGE_W = 3072

LANE = 128
SUBLANE = 8
VMEM_LIMIT = 56 * 1024 * 1024

SUBLANE_BF16 = 16
CB = 512
COL_XBC, COL_GATES, COL_FU, COL_FG, COL_AG, COL_Z = 0, 2, 8, 9, 10, 11
COL_Q, COL_K, COL_V = 12, 13, 14
N_COLB = 15
N_MAIN = N_COLB * CB
PROJ_STEP_B = 3
PROJ_STEPS = N_COLB // PROJ_STEP_B
assert COL_Q == (PROJ_STEPS - 1) * PROJ_STEP_B and COL_V == N_COLB - 1
Q_SCALE = ATT_QK ** -0.5 * math.log2(math.e)


def _dot(a, b):
    return jnp.dot(a, b, preferred_element_type=F32)


def _dot_nt(a, b):
    return lax.dot_general(a, b, (((1,), (1,)), ((), ())), preferred_element_type=F32)


def _split3(x):
    x1 = x.astype(BF16)
    r1 = x - x1.astype(F32)
    x2 = r1.astype(BF16)
    x3 = (r1 - x2.astype(F32)).astype(BF16)
    return x1, x2, x3


def _dot_sel_r(x, sel):
    x1, x2, x3 = _split3(x)
    return _dot(x1, sel) + _dot(x2, sel) + _dot(x3, sel)


def _dot_sel_l(sel, x):
    x1, x2, x3 = _split3(x)
    return _dot(sel, x1) + _dot(sel, x2) + _dot(sel, x3)


def _dot_sel_r2(x, sel):
    x1 = x.astype(BF16)
    x2 = (x - x1.astype(F32)).astype(BF16)
    return _dot(x1, sel) + _dot(x2, sel)


def _sigmoid(x):
    return 0.5 * jnp.tanh(0.5 * x) + 0.5


def _silu(x):
    return x * _sigmoid(x)


def _params(*sem):
    return pltpu.CompilerParams(dimension_semantics=sem, vmem_limit_bytes=VMEM_LIMIT)


def _mod_kernel(cc_ref, w_ref, b_ref, lam_ref, linit_ref, mod_ref, lam_out_ref):
    s = _silu(cc_ref[...])
    mod_ref[0] = jnp.dot(s, w_ref[0], precision=lax.Precision.HIGHEST,
                         preferred_element_type=F32) + b_ref[0]
    lp = lam_ref[0]
    s1 = jnp.sum(lp[0:1] * lp[1:2], axis=-1, keepdims=True)
    s2 = jnp.sum(lp[2:3] * lp[3:4], axis=-1, keepdims=True)
    lam_out_ref[0] = jnp.broadcast_to(jnp.exp(s1) - jnp.exp(s2), (SUBLANE, LANE)) + linit_ref[0]


def _modulation(cc, w_mod, b_mod, lam, linit):
    depth = w_mod.shape[0]
    rows = cc.shape[0]
    tn = D_MODEL
    return pl.pallas_call(
        _mod_kernel,
        grid=(depth, 3 * D_MODEL // tn),
        in_specs=[
            pl.BlockSpec((rows, D_MODEL), lambda l, j: (0, 0)),
            pl.BlockSpec((1, D_MODEL, tn), lambda l, j: (l, 0, j)),
            pl.BlockSpec((1, 1, tn), lambda l, j: (l, 0, j)),
            pl.BlockSpec((1, 4, ATT_QK), lambda l, j: (l, 0, 0)),
            pl.BlockSpec((1, SUBLANE, LANE), lambda l, j: (l, 0, 0)),
        ],
        out_specs=[
            pl.BlockSpec((1, rows, tn), lambda l, j: (l, 0, j)),
            pl.BlockSpec((1, SUBLANE, LANE), lambda l, j: (l, 0, 0)),
        ],
        out_shape=[
            jax.ShapeDtypeStruct((depth, rows, 3 * D_MODEL), F32),
            jax.ShapeDtypeStruct((depth, SUBLANE, LANE), F32),
        ],
        compiler_params=_params("arbitrary", "arbitrary"),
        name="adaln_mod",
    )(cc, w_mod, b_mod, lam, linit)


def _rope(t, cos, sin):
    w = t.shape[-1]
    lane = lax.broadcasted_iota(jnp.int32, t.shape, 1)
    first = (lane % ROPE_AXIS) < (ROPE_AXIS // 2)
    rot = jnp.where(first, -pltpu.roll(t, w - ROPE_AXIS // 2, 1), pltpu.roll(t, ROPE_AXIS // 2, 1))
    return t * cos + rot * sin


def _inproj_kernel(*refs, rope):
    if rope:
        x_ref, mod_ref, nw_ref, w_ref, wdt_ref, cos_ref, sin_ref, p_ref, dt_ref = refs
    else:
        x_ref, mod_ref, nw_ref, w_ref, wdt_ref, p_ref, dt_ref = refs
    x = x_ref[...]
    y = x * lax.rsqrt(jnp.mean(x * x, axis=-1, keepdims=True) + EPS) * nw_ref[...]
    m = mod_ref[0]
    hb = (y * (1.0 + m[:, D_MODEL:2 * D_MODEL]) + m[:, 0:D_MODEL]).astype(BF16)
    dt_ref[...] = _dot(hb, wdt_ref[...])
    ws = PROJ_STEP_B * CB
    for j in range(PROJ_STEPS):
        acc = _dot(hb, w_ref[:, j * ws:(j + 1) * ws])
        if j < PROJ_STEPS - 1:
            p_ref[:, j * ws:(j + 1) * ws] = acc.astype(BF16)
        else:
            q, k, v = (acc[:, b * CB:(b + 1) * CB] for b in range(PROJ_STEP_B))
            if rope:
                cos, sin = cos_ref[...], sin_ref[...]
                q, k = _rope(q, cos, sin), _rope(k, cos, sin)
            p_ref[:, j * ws:j * ws + CB] = (q * Q_SCALE).astype(BF16)
            p_ref[:, j * ws + CB:j * ws + 2 * CB] = k.astype(BF16)
            p_ref[:, j * ws + 2 * CB:(j + 1) * ws] = v.astype(BF16)


def _inproj(x2, mod, norm_w, w_main, w_dt, cos, sin, n, tm):
    rows = x2.shape[0]
    tiles_per_b = n // tm
    rope = cos is not None
    resident = lambda shape: pl.BlockSpec(shape, lambda i: (0, 0), pipeline_mode=pl.Buffered(1))
    in_specs = [
        pl.BlockSpec((tm, D_MODEL), lambda i: (i, 0)),
        pl.BlockSpec((1, 1, 3 * D_MODEL), lambda i: (i // tiles_per_b, 0, 0)),
        pl.BlockSpec((1, D_MODEL), lambda i: (0, 0)),
        resident((D_MODEL, N_MAIN)),
        resident((D_MODEL, LANE)),
    ]
    args = [x2, mod, norm_w, w_main, w_dt]
    if rope:
        in_specs += [pl.BlockSpec((tm, CB), lambda i: (i % tiles_per_b, 0))] * 2
        args += [cos, sin]
    return pl.pallas_call(
        functools.partial(_inproj_kernel, rope=rope),
        grid=(rows // tm,),
        in_specs=in_specs,
        out_specs=[
            pl.BlockSpec((tm, N_MAIN), lambda i: (i, 0)),
            pl.BlockSpec((tm, LANE), lambda i: (i, 0)),
        ],
        out_shape=[
            jax.ShapeDtypeStruct((rows, N_MAIN), BF16),
            jax.ShapeDtypeStruct((rows, LANE), F32),
        ],
        compiler_params=_params("parallel"),
        name="inproj",
    )(*args)


HALO = SUBLANE_BF16


def _conv_kernel(cur_ref, prev_ref, next_ref, w_ref, b_ref, o_ref, ext_ref, *, tiles_per_b):
    i = pl.program_id(0)
    tc = cur_ref.shape[0]
    t = i % tiles_per_b
    ext_ref[0:HALO, :] = jnp.where(t == 0, 0.0, prev_ref[...].astype(F32))
    ext_ref[HALO:HALO + tc, :] = cur_ref[...].astype(F32)
    ext_ref[HALO + tc:2 * HALO + tc, :] = jnp.where(t == tiles_per_b - 1, 0.0,
                                                    next_ref[...].astype(F32))
    acc = jnp.broadcast_to(b_ref[...], o_ref.shape)
    for k in range(CONV_W):
        acc = acc + ext_ref[pl.ds(HALO - CONV_W // 2 + k, tc), :] * w_ref[k:k + 1, :]
    o_ref[...] = _silu(acc).astype(o_ref.dtype)


def _conv_silu(p, conv_w8, conv_b, n, tc):
    rows = p.shape[0]
    tiles_per_b = n // tc
    hb = tc // HALO
    last_hb = rows // HALO - 1
    return pl.pallas_call(
        functools.partial(_conv_kernel, tiles_per_b=tiles_per_b),
        grid=(rows // tc, XBC_W // CB),
        in_specs=[
            pl.BlockSpec((tc, CB), lambda i, j: (i, COL_XBC + j)),
            pl.BlockSpec((HALO, CB), lambda i, j: (jnp.maximum(i * hb - 1, 0), COL_XBC + j)),
            pl.BlockSpec((HALO, CB), lambda i, j: (jnp.minimum((i + 1) * hb, last_hb), COL_XBC + j)),
            pl.BlockSpec((SUBLANE, CB), lambda i, j: (0, j)),
            pl.BlockSpec((1, CB), lambda i, j: (0, j)),
        ],
        out_specs=pl.BlockSpec((tc, CB), lambda i, j: (i, j)),
        out_shape=jax.ShapeDtypeStruct((rows, XBC_W), BF16),
        scratch_shapes=[pltpu.VMEM((tc + 2 * HALO, CB), F32)],
        compiler_params=_params("parallel", "parallel"),
        name="conv_silu",
    )(p, p, p, conv_w8, conv_b)


def _attn_kernel(*refs, with_latent, one_minus_lam_init, n_ctx, chunks):
    if with_latent:
        (q_ref, kc_ref, vc_ref, kl_ref, vl_ref, lam_ref, sw_ref, o_ref,
         k_scr, v_scr, m_scr, acc_scr) = refs
    else:
        q_ref, kc_ref, vc_ref, lam_ref, sw_ref, o_ref, k_scr, v_scr, m_scr, acc_scr = refs
    tq = q_ref.shape[0]

    @pl.when(pl.program_id(2) == 0)
    def _():
        def put(k_ref, v_ref, off):
            rows = k_ref.shape[0]
            k = k_ref[...]
            lane = lax.broadcasted_iota(jnp.int32, k.shape, 1)
            zero = jnp.zeros_like(k)
            k_scr[0, off:off + rows, :] = jnp.where(lane < ATT_QK, k, zero)
            k_scr[1, off:off + rows, :] = jnp.where(lane >= ATT_QK, k, zero)
            v_scr[off:off + rows, 0:ATT_V] = v_ref[...]
            v_scr[off:off + rows, ATT_V:2 * ATT_V] = jnp.ones((rows, ATT_V), BF16)
        put(kc_ref, vc_ref, 0)
        if with_latent:
            put(kl_ref, vl_ref, n_ctx)

    m_scr[...] = jnp.full(m_scr.shape, -jnp.inf, F32)
    acc_scr[...] = jnp.zeros(acc_scr.shape, F32)
    q = q_ref[...]

    def chunk(off, size):
        v = v_scr[pl.ds(off, size), :]
        nt = size // LANE
        for mi in range(2):
            s = _dot_nt(q, k_scr[mi, pl.ds(off, size), :])
            tiles = [s[:, t * LANE:(t + 1) * LANE] for t in range(nt)]
            mx = functools.reduce(jnp.maximum, tiles)
            m_prev = m_scr[mi]
            m_new = jnp.maximum(m_prev, jnp.max(mx, axis=-1, keepdims=True))
            p = jnp.concatenate([jnp.exp2(t - m_new) for t in tiles], axis=1).astype(BF16)
            alpha = jnp.exp2(m_prev - m_new)
            acc_scr[mi] = jnp.concatenate([alpha, alpha], axis=1) * acc_scr[mi] + _dot(p, v)
            m_scr[mi] = m_new

    for off, size in chunks:
        chunk(off, size)

    lam = lam_ref[0:1, :]
    a0, a1 = acc_scr[0], acc_scr[1]
    o = a0[:, 0:ATT_V] / a0[:, ATT_V:] - lam * (a1[:, 0:ATT_V] / a1[:, ATT_V:])
    y = o * lax.rsqrt(jnp.mean(o * o, axis=-1, keepdims=True) + EPS) * sw_ref[...]
    o_ref[...] = (y * one_minus_lam_init).astype(o_ref.dtype)


def _attention(qkv_q, qkv_c, qkv_l, lam_l, subln_w, lam_init, batch, nq_len, nc_len, nl_len, tq, tk):
    with_latent = qkv_l is not None
    nq = nq_len // tq
    hq, hk, hv = (c * (CB // LANE) for c in (COL_Q, COL_K, COL_V))
    in_specs = [
        pl.BlockSpec((tq, LANE), lambda b, h, i: (b * nq + i, hq + h)),
        pl.BlockSpec((nc_len, LANE), lambda b, h, i: (b, hk + h)),
        pl.BlockSpec((nc_len, LANE), lambda b, h, i: (b, hv + h)),
    ]
    args = [qkv_q, qkv_c, qkv_c]
    if with_latent:
        in_specs += [
            pl.BlockSpec((nl_len, LANE), lambda b, h, i: (b, hk + h)),
            pl.BlockSpec((nl_len, LANE), lambda b, h, i: (b, hv + h)),
        ]
        args += [qkv_l, qkv_l]
    in_specs += [
        pl.BlockSpec((SUBLANE, LANE), lambda b, h, i: (0, 0)),
        pl.BlockSpec((1, ATT_V), lambda b, h, i: (0, 0)),
    ]
    args += [lam_l, subln_w]
    n_keys = nc_len + nl_len
    sizes = [nc_len + tk] + [tk] * (nl_len // tk - 1) if with_latent else [nc_len]
    chunks = tuple((sum(sizes[:t]), sizes[t]) for t in range(len(sizes)))
    assert sum(sizes) == n_keys
    return pl.pallas_call(
        functools.partial(_attn_kernel, with_latent=with_latent,
                          one_minus_lam_init=1.0 - lam_init, n_ctx=nc_len, chunks=chunks),
        grid=(batch, ATT_HEADS, nq),
        in_specs=in_specs,
        out_specs=pl.BlockSpec((tq, ATT_V), lambda b, h, i: (b * nq + i, h)),
        out_shape=jax.ShapeDtypeStruct((batch * nq_len, ATT_HEADS * ATT_V), BF16),
        scratch_shapes=[
            pltpu.VMEM((2, n_keys, LANE), BF16),
            pltpu.VMEM((n_keys, 2 * ATT_V), BF16),
            pltpu.VMEM((2, tq, LANE), F32),
            pltpu.VMEM((2, tq, 2 * ATT_V), F32),
        ],
        compiler_params=_params("parallel", "parallel", "arbitrary"),
        name="diff_attention",
    )(*args)


def _fourier_kernel(x_ref, cn_ref, sn_ref, cc_ref, sc_ref, o_ref, acc_ref, *, scale):
    k = pl.program_id(1)
    xb = x_ref[...]
    xc = _dot(xb, cc_ref[...]).astype(BF16)
    xs = _dot(xb, sc_ref[...]).astype(BF16)
    part = _dot(cn_ref[...], xc) - _dot(sn_ref[...], xs)

    @pl.when(k == 0)
    def _():
        acc_ref[...] = part

    @pl.when(k > 0)
    def _():
        acc_ref[...] += part

    @pl.when(k == pl.num_programs(1) - 1)
    def _():
        o_ref[...] = (acc_ref[...] * scale).astype(o_ref.dtype)


DFT_SPLIT = 64


def _dft_tables(n):
    idx = jnp.arange(n, dtype=jnp.int32)

    def cos_sin(rows):
        ang = ((rows[:, None] * idx[None, :]) % n).astype(F32) * (2.0 * math.pi / n)
        return jnp.cos(ang), jnp.sin(ang)

    if n <= DFT_SPLIT * DFT_SPLIT // 4 or n % DFT_SPLIT:
        return cos_sin(idx)
    ca, sa = cos_sin(jnp.arange(n // DFT_SPLIT, dtype=jnp.int32) * DFT_SPLIT)
    cb, sb = cos_sin(jnp.arange(DFT_SPLIT, dtype=jnp.int32))
    ca, sa, cb, sb = ca[:, None, :], sa[:, None, :], cb[None, :, :], sb[None, :, :]
    return (ca * cb - sa * sb).reshape(n, n), (sa * cb + ca * sb).reshape(n, n)


def _fourier(p, cn, sn, ccd, scd, batch, n, tk):
    nk = n // tk
    scale = 1.0 / math.sqrt(n * F_GROUP_W)
    return pl.pallas_call(
        functools.partial(_fourier_kernel, scale=scale),
        grid=(batch, nk),
        in_specs=[
            pl.BlockSpec((tk, CB), lambda b, k: (b * nk + k, COL_FU)),
            pl.BlockSpec((n, tk), lambda b, k: (0, k)),
            pl.BlockSpec((n, tk), lambda b, k: (0, k)),
            pl.BlockSpec((F_W, F_W), lambda b, k: (0, 0)),
            pl.BlockSpec((F_W, F_W), lambda b, k: (0, 0)),
        ],
        out_specs=pl.BlockSpec((n, F_W), lambda b, k: (b, 0)),
        out_shape=jax.ShapeDtypeStruct((batch * n, F_W), BF16),
        scratch_shapes=[pltpu.VMEM((n, F_W), F32)],
        compiler_params=_params("parallel", "arbitrary"),
        name="fourier_mix",
    )(p, cn, sn, ccd, scd)


def _softplus(x):
    return jnp.maximum(x, 0.0) + jnp.log1p(jnp.exp(-jnp.abs(x)))


def _ssd_kernel(xsf_ref, bcf_ref, dtf_ref, xsb_ref, bcb_ref, dtb_ref, alog_ref, dtbias_ref,
                e_ref, init_ref, yf_ref, yb_ref, fin_ref, state_ref):
    c = pl.program_id(1)
    q = CHUNK
    cps = xsf_ref.shape[0] // q
    gw = D_INNER // SSD_GROUPS
    hpg = SSD_HEADS // SSD_GROUPS

    @pl.when(c == 0)
    def _():
        state_ref[...] = init_ref[0]

    row = lax.broadcasted_iota(jnp.int32, (q, q), 0)
    col = lax.broadcasted_iota(jnp.int32, (q, q), 1)
    head_of_lane = lax.broadcasted_iota(jnp.int32, (q, gw), 1) // SSD_P
    neg_a = -jnp.exp(alog_ref[...])
    dtbias = dtbias_ref[...]

    def expand(x, e2):
        x1 = x.astype(BF16)
        x2 = (x - x1.astype(F32)).astype(BF16)
        return _dot(jnp.concatenate([x1, x2], axis=1), e2)

    def one_chunk(d, rows, xs_ref, bc_ref, dt_ref, y_ref):
        fwd = d == 0
        tri = (row >= col) if fwd else (row <= col)
        tri_b = tri.astype(F32).astype(BF16)
        dt = _softplus(dt_ref[rows, :] + dtbias)
        a1, a2, a3 = _split3(dt * neg_a)
        acs = (_dot(jnp.concatenate([tri_b, tri_b], axis=1), jnp.concatenate([a1, a2], axis=0))
               + _dot(tri_b, a3))
        acs_t = acs.T
        last = q - 1 if fwd else 0
        tot_row = acs[last:last + 1, :]
        e2 = e_ref[d]
        dt_w = expand(dt, e2)
        eacs_w = expand(jnp.exp(acs), e2)
        dec_w = expand(jnp.exp(tot_row - acs), e2)
        sdec = _dot_sel_r(jnp.broadcast_to(jnp.exp(tot_row), (SUBLANE, LANE)), e2[0:LANE])[0:1, :]
        xd = xs_ref[rows, :].astype(F32) * dt_w
        xdb = xd.astype(BF16)
        xdd = (xd * dec_w).astype(BF16)
        bc = bc_ref[rows, :]
        for g in range(SSD_GROUPS):
            bg = bc[:, g * D_STATE:(g + 1) * D_STATE]
            cg = bc[:, (SSD_GROUPS + g) * D_STATE:(SSD_GROUPS + g + 1) * D_STATE]
            cb = _dot_nt(cg, bg)
            lanes = slice(g * gw, (g + 1) * gw)
            s_g = state_ref[d, :, lanes]
            y = _dot(cg, s_g.astype(BF16)) * eacs_w[:, lanes]
            xg = xdb[:, lanes]
            zero = jnp.zeros_like(xg)
            for r in range(0, hpg, 2):
                mats, xms = [], []
                for rr in (r, r + 1):
                    jl = d * SSD_HEADS + g * hpg + rr
                    seg = jnp.where(tri, acs[:, jl:jl + 1] - acs_t[jl:jl + 1, :], -jnp.inf)
                    mats.append((cb * jnp.exp(seg)).astype(BF16))
                    xms.append(jnp.where(head_of_lane == rr, xg, zero))
                y = y + _dot(jnp.concatenate(mats, axis=1), jnp.concatenate(xms, axis=0))
            y_ref[rows, lanes] = y.astype(y_ref.dtype)
            bg_t = bg.astype(F32).T.astype(BF16)
            state_ref[d, :, lanes] = s_g * sdec[:, lanes] + _dot(bg_t, xdd[:, lanes])

    for i in range(cps):
        one_chunk(0, pl.ds(i * q, q), xsf_ref, bcf_ref, dtf_ref, yf_ref)
        one_chunk(1, pl.ds((cps - 1 - i) * q, q), xsb_ref, bcb_ref, dtb_ref, yb_ref)

    @pl.when(c == pl.num_programs(1) - 1)
    def _():
        fin_ref[0] = state_ref[...]


def _ssd(act, dt, alog, dtbias, emat2, init, batch, n, cps):
    nc = n // (CHUNK * cps)
    tr = CHUNK * cps
    fidx = lambda b, c: b * nc + c
    bidx = lambda b, c: b * nc + (nc - 1 - c)
    st_shape = (2, D_STATE, D_INNER)
    return pl.pallas_call(
        _ssd_kernel,
        grid=(batch, nc),
        in_specs=[
            pl.BlockSpec((tr, CB), lambda b, c: (fidx(b, c), 0)),
            pl.BlockSpec((tr, CB), lambda b, c: (fidx(b, c), 1)),
            pl.BlockSpec((tr, LANE), lambda b, c: (fidx(b, c), 0)),
            pl.BlockSpec((tr, CB), lambda b, c: (bidx(b, c), 0)),
            pl.BlockSpec((tr, CB), lambda b, c: (bidx(b, c), 1)),
            pl.BlockSpec((tr, LANE), lambda b, c: (bidx(b, c), 0)),
            pl.BlockSpec((1, LANE), lambda b, c: (0, 0)),
            pl.BlockSpec((1, LANE), lambda b, c: (0, 0)),
            pl.BlockSpec((2, 2 * LANE, D_INNER), lambda b, c: (0, 0, 0)),
            pl.BlockSpec((1,) + st_shape, lambda b, c: (b, 0, 0, 0)),
        ],
        out_specs=[
            pl.BlockSpec((tr, D_INNER), lambda b, c: (fidx(b, c), 0)),
            pl.BlockSpec((tr, D_INNER), lambda b, c: (bidx(b, c), 0)),
            pl.BlockSpec((1,) + st_shape, lambda b, c: (b, 0, 0, 0)),
        ],
        out_shape=[
            jax.ShapeDtypeStruct((batch * n, D_INNER), BF16),
            jax.ShapeDtypeStruct((batch * n, D_INNER), BF16),
            jax.ShapeDtypeStruct((batch,) + st_shape, F32),
        ],
        scratch_shapes=[pltpu.VMEM(st_shape, F32)],
        compiler_params=_params("parallel", "arbitrary"),
        name="ssd_scan",
    )(act, act, dt, act, act, dt, alog, dtbias, emat2, init)


def _merge_kernel(four_ref, att_ref, yf_ref, yb_ref, xs_ref, fg_ref, ag_ref, z_ref,
                  g0_ref, g1_ref, g2_ref, x_ref, mod_ref, dskip_ref, snw_ref,
                  wof_ref, woa_ref, wos_ref, wout_ref, nf_ref, o_ref, *, final_norm):
    f = lambda ref: ref[...].astype(F32)
    y_f = _dot((f(four_ref) * _silu(f(fg_ref))).astype(BF16), wof_ref[...])
    y_a = _dot((f(att_ref) * _silu(f(ag_ref))).astype(BF16), woa_ref[...])
    ys = f(yf_ref) + f(yb_ref) + dskip_ref[...] * f(xs_ref)
    t = ys * _silu(f(z_ref))
    t = t * lax.rsqrt(jnp.mean(t * t, axis=-1, keepdims=True) + EPS) * snw_ref[...]
    y_s = _dot(t.astype(BF16), wos_ref[...])
    y = _sigmoid(f(g0_ref)) * y_f + _sigmoid(f(g1_ref)) * y_a + _sigmoid(f(g2_ref)) * y_s
    out = _dot(y.astype(BF16), wout_ref[...])
    xn = x_ref[...] + mod_ref[0][:, 2 * D_MODEL:3 * D_MODEL] * out
    if final_norm:
        xn = xn * lax.rsqrt(jnp.mean(xn * xn, axis=-1, keepdims=True) + EPS) * nf_ref[...]
    o_ref[...] = xn


def _merge(four, att, yf, yb, act, p, x2, mod, dskip_w, snw, wof, woa, wos, wout, norm_f,
           n, tm, final_norm):
    rows = x2.shape[0]
    tiles_per_b = n // tm
    gcol = COL_GATES * CB // D_MODEL
    row_blk = lambda w, cidx: pl.BlockSpec((tm, w), lambda i: (i, cidx))
    const = lambda shape: pl.BlockSpec(shape, lambda i: (0,) * len(shape))
    return pl.pallas_call(
        functools.partial(_merge_kernel, final_norm=final_norm),
        grid=(rows // tm,),
        in_specs=[
            row_blk(F_W, 0), row_blk(CB, 0), row_blk(D_INNER, 0), row_blk(D_INNER, 0),
            row_blk(CB, 0),
            row_blk(CB, COL_FG), row_blk(CB, COL_AG), row_blk(CB, COL_Z),
            row_blk(D_MODEL, gcol), row_blk(D_MODEL, gcol + 1), row_blk(D_MODEL, gcol + 2),
            row_blk(D_MODEL, 0),
            pl.BlockSpec((1, 1, 3 * D_MODEL), lambda i: (i // tiles_per_b, 0, 0)),
            const((1, D_INNER)), const((1, D_INNER)),
            const((F_W, D_MODEL)), const((CB, D_MODEL)), const((D_INNER, D_MODEL)),
            const((D_MODEL, D_MODEL)), const((1, D_MODEL)),
        ],
        out_specs=pl.BlockSpec((tm, D_MODEL), lambda i: (i, 0)),
        out_shape=jax.ShapeDtypeStruct((rows, D_MODEL), F32),
        compiler_params=_params("parallel"),
        name="branch_merge",
    )(four, att, yf, yb, act, p, p, p, p, p, p, x2, mod, dskip_w, snw, wof, woa, wos, wout, norm_f)


def _rope_tables(n):
    rows = n // GRID_W
    row = jnp.repeat(jnp.arange(rows, dtype=F32), GRID_W)
    col = jnp.tile(jnp.arange(GRID_W, dtype=F32), rows)
    freqs = ROPE_BASE ** (-jnp.arange(0, ROPE_AXIS, 2, dtype=F32) / ROPE_AXIS)
    ang_r = row[:, None] * freqs
    ang_c = col[:, None] * freqs
    ang = jnp.concatenate([ang_r, ang_r, ang_c, ang_c], axis=-1)
    reps = CB // ATT_QK
    return jnp.tile(jnp.cos(ang), (1, reps)), jnp.tile(jnp.sin(ang), (1, reps))


def _head_expanders():
    e = np.zeros((2, LANE, D_INNER), np.float32)
    for d in range(2):
        for h in range(SSD_HEADS):
            e[d, d * SSD_HEADS + h, h * SSD_P:(h + 1) * SSD_P] = 1.0
    return jnp.asarray(np.concatenate([e, e], axis=1), BF16)


def _group_dft(w):
    cw, sw = _dft_tables(F_GROUP_W)
    eye = jnp.eye(w // F_GROUP_W, dtype=F32)
    return jnp.kron(eye, cw).astype(BF16), jnp.kron(eye, sw).astype(BF16)


def kernel(x, c, ctx, c_ctx, w_mod, b_mod, norm_w, w_in, conv_w, conv_b, a_log, dt_bias, d_skip,
           ssd_norm_w, lam, subln_w, w_of, w_oa, w_os, w_out, norm_f):
    batch, n, _ = x.shape
    n_ctx = ctx.shape[1]
    depth = w_mod.shape[0]
    assert n % GRID_W == 0 and n % CHUNK == 0 and n_ctx % CHUNK == 0

    o_q = 2 * F_W
    o_ag = o_q + 3 * CB
    o_xbc = o_ag + CB + D_INNER
    o_dt = o_xbc + XBC_W
    o_gt = o_dt + DT_W
    w_main = jnp.concatenate([w_in[:, :, o_xbc:o_dt], w_in[:, :, o_gt:], w_in[:, :, :o_q],
                              w_in[:, :, o_ag:o_xbc], w_in[:, :, o_q:o_ag]], axis=-1).astype(BF16)
    w_dt = jnp.pad(w_in[:, :, o_dt:o_gt], ((0, 0), (0, 0), (0, LANE - DT_W))).astype(BF16)
    conv_w8 = jnp.pad(conv_w, ((0, 0), (0, SUBLANE - CONV_W), (0, 0)))
    pad_lanes = lambda a: jnp.pad(a.reshape(depth, 1, DT_W), ((0, 0), (0, 0), (0, LANE - DT_W)))
    alog_p, dtbias_p = pad_lanes(a_log), pad_lanes(dt_bias)
    dskip_w = jnp.repeat(d_skip, SSD_P, axis=-1).reshape(depth, 1, D_INNER)
    wof_b, woa_b, wos_b, wout_b = (w.astype(BF16) for w in (w_of, w_oa, w_os, w_out))
    lam_inits = [0.8 - 0.6 * math.exp(-0.3 * l) for l in range(depth)]
    linit = jnp.asarray(np.broadcast_to(np.asarray(lam_inits, np.float32)[:, None, None],
                                        (depth, SUBLANE, LANE)))

    cos_t, sin_t = _rope_tables(n)
    cn_l, sn_l = (t.astype(BF16) for t in _dft_tables(n))
    cn_c, sn_c = (t.astype(BF16) for t in _dft_tables(n_ctx))
    ccd, scd = _group_dft(F_W)
    emat = _head_expanders()
    zero_state = jnp.zeros((batch, 2, D_STATE, D_INNER), F32)

    mod_rows = -(-(batch + 1) // SUBLANE) * SUBLANE
    cc = jnp.concatenate([c, c_ctx[None, :], jnp.zeros((mod_rows - batch - 1, D_MODEL), F32)], axis=0)
    mod_all, lam_all = _modulation(cc, w_mod, b_mod.reshape(depth, 1, 3 * D_MODEL), lam, linit)

    tm_l = min(n, 512)
    tq_l = min(n, 512)
    tk_l = min(n, 1024)
    tf_l = min(n, 512)
    tmerge_l = min(n, 256)
    rows_c = batch * n_ctx
    tm_c = math.gcd(rows_c, 512)
    cps_l = math.gcd(n // CHUNK, 4)
    cps_c = math.gcd(n_ctx // CHUNK, 2)
    xl = x.reshape(batch * n, D_MODEL)
    xc = ctx.reshape(batch * n_ctx, D_MODEL)
    for l in range(depth):
        last = l == depth - 1
        mod_l = mod_all[l, :batch].reshape(batch, 1, 3 * D_MODEL)
        mod_c = mod_all[l, batch].reshape(1, 1, 3 * D_MODEL)
        nw = norm_w[l].reshape(1, D_MODEL)
        p_c, dt_c = _inproj(xc, mod_c, nw, w_main[l], w_dt[l], None, None, rows_c, tm_c)
        p_l, dt_l = _inproj(xl, mod_l, nw, w_main[l], w_dt[l], cos_t, sin_t, n, tm_l)
        cb_ = conv_b[l].reshape(1, XBC_W)
        act_c = _conv_silu(p_c, conv_w8[l], cb_, n_ctx, n_ctx)
        act_l = _conv_silu(p_l, conv_w8[l], cb_, n, min(n, 1024))
        yf_c, yb_c, st_c = _ssd(act_c, dt_c, alog_p[l], dtbias_p[l], emat, zero_state, batch, n_ctx,
                                cps_c)
        yf_l, yb_l, _ = _ssd(act_l, dt_l, alog_p[l], dtbias_p[l], emat, st_c, batch, n, cps_l)
        sw = subln_w[l].reshape(1, ATT_V)
        att_l = _attention(p_l, p_c, p_l, lam_all[l], sw, lam_inits[l], batch, n, n_ctx, n,
                           tq_l, tk_l)
        four_l = _fourier(p_l, cn_l, sn_l, ccd, scd, batch, n, tf_l)
        merge_w = (dskip_w[l], ssd_norm_w[l].reshape(1, D_INNER), wof_b[l], woa_b[l], wos_b[l],
                   wout_b[l], norm_f.reshape(1, D_MODEL))
        if not last:
            att_c = _attention(p_c, p_c, None, lam_all[l], sw, lam_inits[l], batch, n_ctx, n_ctx, 0,
                               n_ctx, 0)
            four_c = _fourier(p_c, cn_c, sn_c, ccd, scd, batch, n_ctx, n_ctx)
            xc = _merge(four_c, att_c, yf_c, yb_c, act_c, p_c, xc, mod_c, *merge_w,
                        rows_c, n_ctx, False)
        xl = _merge(four_l, att_l, yf_l, yb_l, act_l, p_l, xl, mod_l, *merge_w, n, tmerge_l, last)
    return xl.reshape(batch, n, D_MODEL)
```

```python
import functools
import math

import numpy as np
import jax
import jax.numpy as jnp
from jax import lax
from jax.experimental import pallas as pl
from jax.experimental.pallas import tpu as pltpu

F32 = jnp.float32
BF16 = jnp.bfloat16
EPS = 1e-6

D_MODEL = 1024
GRID_W = 64
F_GROUP_W = 128
F_W = 512
ATT_HEADS = 4
ATT_QK = 64
ATT_V = 128
ROPE_AXIS = 32
ROPE_BASE = 10000.0
D_INNER = 512
SSD_P = 64
SSD_HEADS = 8
SSD_GROUPS = 2
D_STATE = 128
CONV_W = 5
CHUNK = 128
XBC_W = 1024
DT_W = 16
MERGE_W = 3072

LANE = 128
SUBLANE = 8
VMEM_LIMIT = 56 * 1024 * 1024

SUBLANE_BF16 = 16
CB = 512
COL_XBC, COL_GATES, COL_FU, COL_FG, COL_AG, COL_Z = 0, 2, 8, 9, 10, 11
COL_Q, COL_K, COL_V = 12, 13, 14
N_COLB = 15
N_MAIN = N_COLB * CB
PROJ_STEP_B = 3
PROJ_STEPS = N_COLB // PROJ_STEP_B
assert COL_Q == (PROJ_STEPS - 1) * PROJ_STEP_B and COL_V == N_COLB - 1
Q_SCALE = ATT_QK ** -0.5 * math.log2(math.e)


def _dot(a, b):
    return jnp.dot(a, b, preferred_element_type=F32)


def _dot_nt(a, b):
    return lax.dot_general(a, b, (((1,), (1,)), ((), ())), preferred_element_type=F32)


def _split3(x):
    x1 = x.astype(BF16)
    r1 = x - x1.astype(F32)
    x2 = r1.astype(BF16)
    x3 = (r1 - x2.astype(F32)).astype(BF16)
    return x1, x2, x3


def _dot_sel_r(x, sel):
    x1, x2, x3 = _split3(x)
    return _dot(x1, sel) + _dot(x2, sel) + _dot(x3, sel)


def _dot_sel_l(sel, x):
    x1, x2, x3 = _split3(x)
    return _dot(sel, x1) + _dot(sel, x2) + _dot(sel, x3)


def _dot_sel_r2(x, sel):
    x1 = x.astype(BF16)
    x2 = (x - x1.astype(F32)).astype(BF16)
    return _dot(x1, sel) + _dot(x2, sel)


def _sigmoid(x):
    return 0.5 * jnp.tanh(0.5 * x) + 0.5


def _silu(x):
    return x * _sigmoid(x)


def _params(*sem):
    return pltpu.CompilerParams(dimension_semantics=sem, vmem_limit_bytes=VMEM_LIMIT)


def _mod_kernel(cc_ref, w_ref, b_ref, lam_ref, linit_ref, mod_ref, lam_out_ref):
    s = _silu(cc_ref[...])
    mod_ref[0] = jnp.dot(s, w_ref[0], precision=lax.Precision.HIGHEST,
                         preferred_element_type=F32) + b_ref[0]
    lp = lam_ref[0]
    s1 = jnp.sum(lp[0:1] * lp[1:2], axis=-1, keepdims=True)
    s2 = jnp.sum(lp[2:3] * lp[3:4], axis=-1, keepdims=True)
    lam_out_ref[0] = jnp.broadcast_to(jnp.exp(s1) - jnp.exp(s2), (SUBLANE, LANE)) + linit_ref[0]


def _modulation(cc, w_mod, b_mod, lam, linit):
    depth = w_mod.shape[0]
    rows = cc.shape[0]
    tn = D_MODEL
    return pl.pallas_call(
        _mod_kernel,
        grid=(depth, 3 * D_MODEL // tn),
        in_specs=[
            pl.BlockSpec((rows, D_MODEL), lambda l, j: (0, 0)),
            pl.BlockSpec((1, D_MODEL, tn), lambda l, j: (l, 0, j)),
            pl.BlockSpec((1, 1, tn), lambda l, j: (l, 0, j)),
            pl.BlockSpec((1, 4, ATT_QK), lambda l, j: (l, 0, 0)),
            pl.BlockSpec((1, SUBLANE, LANE), lambda l, j: (l, 0, 0)),
        ],
        out_specs=[
            pl.BlockSpec((1, rows, tn), lambda l, j: (l, 0, j)),
            pl.BlockSpec((1, SUBLANE, LANE), lambda l, j: (l, 0, 0)),
        ],
        out_shape=[
            jax.ShapeDtypeStruct((depth, rows, 3 * D_MODEL), F32),
            jax.ShapeDtypeStruct((depth, SUBLANE, LANE), F32),
        ],
        compiler_params=_params("arbitrary", "arbitrary"),
        name="adaln_mod",
    )(cc, w_mod, b_mod, lam, linit)


def _rope(t, cos, sin):
    w = t.shape[-1]
    lane = lax.broadcasted_iota(jnp.int32, t.shape, 1)
    first = (lane % ROPE_AXIS) < (ROPE_AXIS // 2)
    rot = jnp.where(first, -pltpu.roll(t, w - ROPE_AXIS // 2, 1), pltpu.roll(t, ROPE_AXIS // 2, 1))
    return t * cos + rot * sin


def _inproj_kernel(*refs, rope):
    if rope:
        x_ref, mod_ref, nw_ref, w_ref, wdt_ref, cos_ref, sin_ref, p_ref, dt_ref = refs
    else:
        x_ref, mod_ref, nw_ref, w_ref, wdt_ref, p_ref, dt_ref = refs
    x = x_ref[...]
    y = x * lax.rsqrt(jnp.mean(x * x, axis=-1, keepdims=True) + EPS) * nw_ref[...]
    m = mod_ref[0]
    hb = (y * (1.0 + m[:, D_MODEL:2 * D_MODEL]) + m[:, 0:D_MODEL]).astype(BF16)
    dt_ref[...] = _dot(hb, wdt_ref[...])
    ws = PROJ_STEP_B * CB
    for j in range(PROJ_STEPS):
        acc = _dot(hb, w_ref[:, j * ws:(j + 1) * ws])
        if j < PROJ_STEPS - 1:
            for b in range(PROJ_STEP_B):
                blk, cols = j * PROJ_STEP_B + b, slice(b * CB, (b + 1) * CB)
                act = (_sigmoid if COL_GATES <= blk < COL_FU else
                       _silu if blk in (COL_FG, COL_AG, COL_Z) else (lambda t: t))
                p_ref[:, blk * CB:(blk + 1) * CB] = act(acc[:, cols]).astype(BF16)
        else:
            q, k, v = (acc[:, b * CB:(b + 1) * CB] for b in range(PROJ_STEP_B))
            if rope:
                cos, sin = cos_ref[...], sin_ref[...]
                q, k = _rope(q, cos, sin), _rope(k, cos, sin)
            p_ref[:, j * ws:j * ws + CB] = (q * Q_SCALE).astype(BF16)
            p_ref[:, j * ws + CB:j * ws + 2 * CB] = k.astype(BF16)
            p_ref[:, j * ws + 2 * CB:(j + 1) * ws] = v.astype(BF16)


def _inproj(x2, mod, norm_w, w_main, w_dt, cos, sin, n, tm):
    rows = x2.shape[0]
    tiles_per_b = n // tm
    rope = cos is not None
    resident = lambda shape: pl.BlockSpec(shape, lambda i: (0, 0), pipeline_mode=pl.Buffered(1))
    in_specs = [
        pl.BlockSpec((tm, D_MODEL), lambda i: (i, 0)),
        pl.BlockSpec((1, 1, 3 * D_MODEL), lambda i: (i // tiles_per_b, 0, 0)),
        pl.BlockSpec((1, D_MODEL), lambda i: (0, 0)),
        resident((D_MODEL, N_MAIN)),
        resident((D_MODEL, LANE)),
    ]
    args = [x2, mod, norm_w, w_main, w_dt]
    if rope:
        in_specs += [pl.BlockSpec((tm, CB), lambda i: (i % tiles_per_b, 0))] * 2
        args += [cos, sin]
    return pl.pallas_call(
        functools.partial(_inproj_kernel, rope=rope),
        grid=(rows // tm,),
        in_specs=in_specs,
        out_specs=[
            pl.BlockSpec((tm, N_MAIN), lambda i: (i, 0)),
            pl.BlockSpec((tm, LANE), lambda i: (i, 0)),
        ],
        out_shape=[
            jax.ShapeDtypeStruct((rows, N_MAIN), BF16),
            jax.ShapeDtypeStruct((rows, LANE), F32),
        ],
        compiler_params=_params("parallel"),
        name="inproj",
    )(*args)


HALO = SUBLANE_BF16


def _conv_kernel(cur_ref, prev_ref, next_ref, w_ref, b_ref, o_ref, ext_ref, *, tiles_per_b):
    i = pl.program_id(0)
    tc = cur_ref.shape[0]
    t = i % tiles_per_b
    ext_ref[0:HALO, :] = jnp.where(t == 0, 0.0, prev_ref[...].astype(F32))
    ext_ref[HALO:HALO + tc, :] = cur_ref[...].astype(F32)
    ext_ref[HALO + tc:2 * HALO + tc, :] = jnp.where(t == tiles_per_b - 1, 0.0,
                                                    next_ref[...].astype(F32))
    acc = jnp.broadcast_to(b_ref[...], o_ref.shape)
    for k in range(CONV_W):
        acc = acc + ext_ref[pl.ds(HALO - CONV_W // 2 + k, tc), :] * w_ref[k:k + 1, :]
    o_ref[...] = _silu(acc).astype(o_ref.dtype)


def _conv_silu(p, conv_w8, conv_b, n, tc):
    rows = p.shape[0]
    tiles_per_b = n // tc
    hb = tc // HALO
    last_hb = rows // HALO - 1
    return pl.pallas_call(
        functools.partial(_conv_kernel, tiles_per_b=tiles_per_b),
        grid=(rows // tc, XBC_W // CB),
        in_specs=[
            pl.BlockSpec((tc, CB), lambda i, j: (i, COL_XBC + j)),
            pl.BlockSpec((HALO, CB), lambda i, j: (jnp.maximum(i * hb - 1, 0), COL_XBC + j)),
            pl.BlockSpec((HALO, CB), lambda i, j: (jnp.minimum((i + 1) * hb, last_hb), COL_XBC + j)),
            pl.BlockSpec((SUBLANE, CB), lambda i, j: (0, j)),
            pl.BlockSpec((1, CB), lambda i, j: (0, j)),
        ],
        out_specs=pl.BlockSpec((tc, CB), lambda i, j: (i, j)),
        out_shape=jax.ShapeDtypeStruct((rows, XBC_W), BF16),
        scratch_shapes=[pltpu.VMEM((tc + 2 * HALO, CB), F32)],
        compiler_params=_params("parallel", "parallel"),
        name="conv_silu",
    )(p, p, p, conv_w8, conv_b)


def _attn_kernel(*refs, with_latent, one_minus_lam_init, n_ctx, chunks):
    if with_latent:
        (q_ref, kc_ref, vc_ref, kl_ref, vl_ref, lam_ref, sw_ref, o_ref,
         k_scr, v_scr, m_scr, acc_scr) = refs
    else:
        q_ref, kc_ref, vc_ref, lam_ref, sw_ref, o_ref, k_scr, v_scr, m_scr, acc_scr = refs
    tq = q_ref.shape[0]

    @pl.when(pl.program_id(2) == 0)
    def _():
        def put(k_ref, v_ref, off):
            rows = k_ref.shape[0]
            k = k_ref[...]
            lane = lax.broadcasted_iota(jnp.int32, k.shape, 1)
            zero = jnp.zeros_like(k)
            k_scr[0, off:off + rows, :] = jnp.where(lane < ATT_QK, k, zero)
            k_scr[1, off:off + rows, :] = jnp.where(lane >= ATT_QK, k, zero)
            v_scr[off:off + rows, 0:ATT_V] = v_ref[...]
            v_scr[off:off + rows, ATT_V:2 * ATT_V] = jnp.ones((rows, ATT_V), BF16)
        put(kc_ref, vc_ref, 0)
        if with_latent:
            put(kl_ref, vl_ref, n_ctx)

    m_scr[...] = jnp.full(m_scr.shape, -jnp.inf, F32)
    acc_scr[...] = jnp.zeros(acc_scr.shape, F32)
    q = q_ref[...]

    def chunk(off, size):
        v = v_scr[pl.ds(off, size), :]
        nt = size // LANE
        for mi in range(2):
            s = _dot_nt(q, k_scr[mi, pl.ds(off, size), :])
            tiles = [s[:, t * LANE:(t + 1) * LANE] for t in range(nt)]
            mx = functools.reduce(jnp.maximum, tiles)
            m_prev = m_scr[mi]
            m_new = jnp.maximum(m_prev, jnp.max(mx, axis=-1, keepdims=True))
            p = jnp.concatenate([jnp.exp2(t - m_new) for t in tiles], axis=1).astype(BF16)
            alpha = jnp.exp2(m_prev - m_new)
            acc_scr[mi] = jnp.concatenate([alpha, alpha], axis=1) * acc_scr[mi] + _dot(p, v)
            m_scr[mi] = m_new

    for off, size in chunks:
        chunk(off, size)

    lam = lam_ref[0:1, :]
    a0, a1 = acc_scr[0], acc_scr[1]
    o = a0[:, 0:ATT_V] / a0[:, ATT_V:] - lam * (a1[:, 0:ATT_V] / a1[:, ATT_V:])
    y = o * lax.rsqrt(jnp.mean(o * o, axis=-1, keepdims=True) + EPS) * sw_ref[...]
    o_ref[...] = (y * one_minus_lam_init).astype(o_ref.dtype)


def _attention(qkv_q, qkv_c, qkv_l, lam_l, subln_w, lam_init, batch, nq_len, nc_len, nl_len, tq, tk):
    with_latent = qkv_l is not None
    nq = nq_len // tq
    hq, hk, hv = (c * (CB // LANE) for c in (COL_Q, COL_K, COL_V))
    in_specs = [
        pl.BlockSpec((tq, LANE), lambda b, h, i: (b * nq + i, hq + h)),
        pl.BlockSpec((nc_len, LANE), lambda b, h, i: (b, hk + h)),
        pl.BlockSpec((nc_len, LANE), lambda b, h, i: (b, hv + h)),
    ]
    args = [qkv_q, qkv_c, qkv_c]
    if with_latent:
        in_specs += [
            pl.BlockSpec((nl_len, LANE), lambda b, h, i: (b, hk + h)),
            pl.BlockSpec((nl_len, LANE), lambda b, h, i: (b, hv + h)),
        ]
        args += [qkv_l, qkv_l]
    in_specs += [
        pl.BlockSpec((SUBLANE, LANE), lambda b, h, i: (0, 0)),
        pl.BlockSpec((1, ATT_V), lambda b, h, i: (0, 0)),
    ]
    args += [lam_l, subln_w]
    n_keys = nc_len + nl_len
    sizes = [nc_len + tk] + [tk] * (nl_len // tk - 1) if with_latent else [nc_len]
    chunks = tuple((sum(sizes[:t]), sizes[t]) for t in range(len(sizes)))
    assert sum(sizes) == n_keys
    return pl.pallas_call(
        functools.partial(_attn_kernel, with_latent=with_latent,
                          one_minus_lam_init=1.0 - lam_init, n_ctx=nc_len, chunks=chunks),
        grid=(batch, ATT_HEADS, nq),
        in_specs=in_specs,
        out_specs=pl.BlockSpec((tq, ATT_V), lambda b, h, i: (b * nq + i, h)),
        out_shape=jax.ShapeDtypeStruct((batch * nq_len, ATT_HEADS * ATT_V), BF16),
        scratch_shapes=[
            pltpu.VMEM((2, n_keys, LANE), BF16),
            pltpu.VMEM((n_keys, 2 * ATT_V), BF16),
            pltpu.VMEM((2, tq, LANE), F32),
            pltpu.VMEM((2, tq, 2 * ATT_V), F32),
        ],
        compiler_params=_params("parallel", "parallel", "arbitrary"),
        name="diff_attention",
    )(*args)


def _fourier_kernel(x_ref, cs_ref, cc_ref, sc_ref, o_ref, acc_ref, *, scale):
    k = pl.program_id(1)

    @pl.when(k == 0)
    def _():
        acc_ref[...] = jnp.zeros(acc_ref.shape, F32)

    xb = x_ref[...]
    xc = _dot(xb, cc_ref[...]).astype(BF16)
    xs = _dot(xb, sc_ref[...]).astype(BF16)
    acc_ref[...] += _dot(cs_ref[...], jnp.concatenate([xc, xs], axis=0))

    @pl.when(k == pl.num_programs(1) - 1)
    def _():
        o_ref[...] = (acc_ref[...] * scale).astype(o_ref.dtype)


DFT_SPLIT = 64


def _dft_tables(n):
    idx = jnp.arange(n, dtype=jnp.int32)

    def cos_sin(rows):
        ang = ((rows[:, None] * idx[None, :]) % n).astype(F32) * (2.0 * math.pi / n)
        return jnp.cos(ang), jnp.sin(ang)

    if n <= DFT_SPLIT * DFT_SPLIT // 4 or n % DFT_SPLIT:
        return cos_sin(idx)
    ca, sa = cos_sin(jnp.arange(n // DFT_SPLIT, dtype=jnp.int32) * DFT_SPLIT)
    cb, sb = cos_sin(jnp.arange(DFT_SPLIT, dtype=jnp.int32))
    ca, sa, cb, sb = ca[:, None, :], sa[:, None, :], cb[None, :, :], sb[None, :, :]
    return (ca * cb - sa * sb).reshape(n, n), (sa * cb + ca * sb).reshape(n, n)


def _seq_dft_table(n, tk):
    cn, sn = _dft_tables(n)
    nk = n // tk
    cs = jnp.stack([cn.reshape(n, nk, tk), -sn.reshape(n, nk, tk)], axis=2)
    return cs.reshape(n, 2 * n).astype(BF16)


def _fourier(p, cs, ccd, scd, batch, n, tk):
    nk = n // tk
    scale = 1.0 / math.sqrt(n * F_GROUP_W)
    return pl.pallas_call(
        functools.partial(_fourier_kernel, scale=scale),
        grid=(batch, nk),
        in_specs=[
            pl.BlockSpec((tk, CB), lambda b, k: (b * nk + k, COL_FU)),
            pl.BlockSpec((n, 2 * tk), lambda b, k: (0, k)),
            pl.BlockSpec((F_W, F_W), lambda b, k: (0, 0)),
            pl.BlockSpec((F_W, F_W), lambda b, k: (0, 0)),
        ],
        out_specs=pl.BlockSpec((n, F_W), lambda b, k: (b, 0)),
        out_shape=jax.ShapeDtypeStruct((batch * n, F_W), BF16),
        scratch_shapes=[pltpu.VMEM((n, F_W), F32)],
        compiler_params=_params("parallel", "arbitrary"),
        name="fourier_mix",
    )(p, cs, ccd, scd)


def _softplus(x):
    return jnp.maximum(x, 0.0) + jnp.log1p(jnp.exp(-jnp.abs(x)))


def _ssd_kernel(xsf_ref, bcf_ref, dtf_ref, xsb_ref, bcb_ref, dtb_ref, alog_ref, dtbias_ref,
                e_ref, init_ref, yf_ref, yb_ref, fin_ref, state_ref):
    c = pl.program_id(1)
    q = CHUNK
    cps = xsf_ref.shape[0] // q
    gw = D_INNER // SSD_GROUPS
    hpg = SSD_HEADS // SSD_GROUPS

    @pl.when(c == 0)
    def _():
        state_ref[...] = init_ref[0]

    row = lax.broadcasted_iota(jnp.int32, (q, q), 0)
    col = lax.broadcasted_iota(jnp.int32, (q, q), 1)
    head_of_lane = lax.broadcasted_iota(jnp.int32, (q, gw), 1) // SSD_P
    neg_a = -jnp.exp(alog_ref[...])
    dtbias = dtbias_ref[...]

    def expand(x, e2):
        x1 = x.astype(BF16)
        x2 = (x - x1.astype(F32)).astype(BF16)
        return _dot(jnp.concatenate([x1, x2], axis=1), e2)

    def one_chunk(d, rows, xs_ref, bc_ref, dt_ref, y_ref):
        fwd = d == 0
        tri = (row >= col) if fwd else (row <= col)
        tri_b = tri.astype(F32).astype(BF16)
        dt = _softplus(dt_ref[rows, :] + dtbias)
        a1, a2, a3 = _split3(dt * neg_a)
        acs = (_dot(jnp.concatenate([tri_b, tri_b], axis=1), jnp.concatenate([a1, a2], axis=0))
               + _dot(tri_b, a3))
        acs_t = acs.T
        last = q - 1 if fwd else 0
        tot_row = acs[last:last + 1, :]
        e2 = e_ref[d]
        dt_w = expand(dt, e2)
        eacs_w = expand(jnp.exp(acs), e2)
        dec_w = expand(jnp.exp(tot_row - acs), e2)
        sdec = _dot_sel_r(jnp.broadcast_to(jnp.exp(tot_row), (SUBLANE, LANE)), e2[0:LANE])[0:1, :]
        xd = xs_ref[rows, :].astype(F32) * dt_w
        xdb = xd.astype(BF16)
        xdd = (xd * dec_w).astype(BF16)
        bc = bc_ref[rows, :]
        for g in range(SSD_GROUPS):
            bg = bc[:, g * D_STATE:(g + 1) * D_STATE]
            cg = bc[:, (SSD_GROUPS + g) * D_STATE:(SSD_GROUPS + g + 1) * D_STATE]
            cb = _dot_nt(cg, bg)
            lanes = slice(g * gw, (g + 1) * gw)
            s_g = state_ref[d, :, lanes]
            y = _dot(cg, s_g.astype(BF16)) * eacs_w[:, lanes]
            xg = xdb[:, lanes]
            zero = jnp.zeros_like(xg)
            for r in range(0, hpg, 2):
                mats, xms = [], []
                for rr in (r, r + 1):
                    jl = d * SSD_HEADS + g * hpg + rr
                    seg = jnp.where(tri, acs[:, jl:jl + 1] - acs_t[jl:jl + 1, :], -jnp.inf)
                    mats.append((cb * jnp.exp(seg)).astype(BF16))
                    xms.append(jnp.where(head_of_lane == rr, xg, zero))
                y = y + _dot(jnp.concatenate(mats, axis=1), jnp.concatenate(xms, axis=0))
            y_ref[rows, lanes] = y.astype(y_ref.dtype)
            bg_t = bg.astype(F32).T.astype(BF16)
            state_ref[d, :, lanes] = s_g * sdec[:, lanes] + _dot(bg_t, xdd[:, lanes])

    for i in range(cps):
        one_chunk(0, pl.ds(i * q, q), xsf_ref, bcf_ref, dtf_ref, yf_ref)
        one_chunk(1, pl.ds((cps - 1 - i) * q, q), xsb_ref, bcb_ref, dtb_ref, yb_ref)

    @pl.when(c == pl.num_programs(1) - 1)
    def _():
        fin_ref[0] = state_ref[...]


def _ssd(act, dt, alog, dtbias, emat2, init, batch, n, cps):
    nc = n // (CHUNK * cps)
    tr = CHUNK * cps
    fidx = lambda b, c: b * nc + c
    bidx = lambda b, c: b * nc + (nc - 1 - c)
    st_shape = (2, D_STATE, D_INNER)
    return pl.pallas_call(
        _ssd_kernel,
        grid=(batch, nc),
        in_specs=[
            pl.BlockSpec((tr, CB), lambda b, c: (fidx(b, c), 0)),
            pl.BlockSpec((tr, CB), lambda b, c: (fidx(b, c), 1)),
            pl.BlockSpec((tr, LANE), lambda b, c: (fidx(b, c), 0)),
            pl.BlockSpec((tr, CB), lambda b, c: (bidx(b, c), 0)),
            pl.BlockSpec((tr, CB), lambda b, c: (bidx(b, c), 1)),
            pl.BlockSpec((tr, LANE), lambda b, c: (bidx(b, c), 0)),
            pl.BlockSpec((1, LANE), lambda b, c: (0, 0)),
            pl.BlockSpec((1, LANE), lambda b, c: (0, 0)),
            pl.BlockSpec((2, 2 * LANE, D_INNER), lambda b, c: (0, 0, 0)),
            pl.BlockSpec((1,) + st_shape, lambda b, c: (b, 0, 0, 0)),
        ],
        out_specs=[
            pl.BlockSpec((tr, D_INNER), lambda b, c: (fidx(b, c), 0)),
            pl.BlockSpec((tr, D_INNER), lambda b, c: (bidx(b, c), 0)),
            pl.BlockSpec((1,) + st_shape, lambda b, c: (b, 0, 0, 0)),
        ],
        out_shape=[
            jax.ShapeDtypeStruct((batch * n, D_INNER), BF16),
            jax.ShapeDtypeStruct((batch * n, D_INNER), BF16),
            jax.ShapeDtypeStruct((batch,) + st_shape, F32),
        ],
        scratch_shapes=[pltpu.VMEM(st_shape, F32)],
        compiler_params=_params("parallel", "arbitrary"),
        name="ssd_scan",
    )(act, act, dt, act, act, dt, alog, dtbias, emat2, init)


def _merge_kernel(four_ref, att_ref, yf_ref, yb_ref, xs_ref, fg_ref, ag_ref, z_ref,
                  g0_ref, g1_ref, g2_ref, x_ref, mod_ref, dskip_ref, snw_ref,
                  wof_ref, woa_ref, wos_ref, wout_ref, nf_ref, o_ref, *, final_norm):
    f = lambda ref: ref[...].astype(F32)
    y_f = _dot((f(four_ref) * f(fg_ref)).astype(BF16), wof_ref[...])
    y_a = _dot((f(att_ref) * f(ag_ref)).astype(BF16), woa_ref[...])
    ys = f(yf_ref) + f(yb_ref) + dskip_ref[...] * f(xs_ref)
    t = ys * f(z_ref)
    t = t * lax.rsqrt(jnp.mean(t * t, axis=-1, keepdims=True) + EPS) * snw_ref[...]
    y_s = _dot(t.astype(BF16), wos_ref[...])
    y = f(g0_ref) * y_f + f(g1_ref) * y_a + f(g2_ref) * y_s
    out = _dot(y.astype(BF16), wout_ref[...])
    xn = x_ref[...] + mod_ref[0][:, 2 * D_MODEL:3 * D_MODEL] * out
    if final_norm:
        xn = xn * lax.rsqrt(jnp.mean(xn * xn, axis=-1, keepdims=True) + EPS) * nf_ref[...]
    o_ref[...] = xn


def _merge(four, att, yf, yb, act, p, x2, mod, dskip_w, snw, wof, woa, wos, wout, norm_f,
           n, tm, final_norm):
    rows = x2.shape[0]
    tiles_per_b = n // tm
    gcol = COL_GATES * CB // D_MODEL
    row_blk = lambda w, cidx: pl.BlockSpec((tm, w), lambda i: (i, cidx))
    const = lambda shape: pl.BlockSpec(shape, lambda i: (0,) * len(shape))
    return pl.pallas_call(
        functools.partial(_merge_kernel, final_norm=final_norm),
        grid=(rows // tm,),
        in_specs=[
            row_blk(F_W, 0), row_blk(CB, 0), row_blk(D_INNER, 0), row_blk(D_INNER, 0),
            row_blk(CB, 0),
            row_blk(CB, COL_FG), row_blk(CB, COL_AG), row_blk(CB, COL_Z),
            row_blk(D_MODEL, gcol), row_blk(D_MODEL, gcol + 1), row_blk(D_MODEL, gcol + 2),
            row_blk(D_MODEL, 0),
            pl.BlockSpec((1, 1, 3 * D_MODEL), lambda i: (i // tiles_per_b, 0, 0)),
            const((1, D_INNER)), const((1, D_INNER)),
            const((F_W, D_MODEL)), const((CB, D_MODEL)), const((D_INNER, D_MODEL)),
            const((D_MODEL, D_MODEL)), const((1, D_MODEL)),
        ],
        out_specs=pl.BlockSpec((tm, D_MODEL), lambda i: (i, 0)),
        out_shape=jax.ShapeDtypeStruct((rows, D_MODEL), F32),
        compiler_params=_params("parallel"),
        name="branch_merge",
    )(four, att, yf, yb, act, p, p, p, p, p, p, x2, mod, dskip_w, snw, wof, woa, wos, wout, norm_f)


def _rope_tables(n):
    rows = n // GRID_W
    row = jnp.repeat(jnp.arange(rows, dtype=F32), GRID_W)
    col = jnp.tile(jnp.arange(GRID_W, dtype=F32), rows)
    freqs = ROPE_BASE ** (-jnp.arange(0, ROPE_AXIS, 2, dtype=F32) / ROPE_AXIS)
    ang_r = row[:, None] * freqs
    ang_c = col[:, None] * freqs
    ang = jnp.concatenate([ang_r, ang_r, ang_c, ang_c], axis=-1)
    reps = CB // ATT_QK
    return jnp.tile(jnp.cos(ang), (1, reps)), jnp.tile(jnp.sin(ang), (1, reps))


def _head_expanders():
    e = np.zeros((2, LANE, D_INNER), np.float32)
    for d in range(2):
        for h in range(SSD_HEADS):
            e[d, d * SSD_HEADS + h, h * SSD_P:(h + 1) * SSD_P] = 1.0
    return jnp.asarray(np.concatenate([e, e], axis=1), BF16)


def _group_dft(w):
    cw, sw = _dft_tables(F_GROUP_W)
    eye = jnp.eye(w // F_GROUP_W, dtype=F32)
    return jnp.kron(eye, cw).astype(BF16), jnp.kron(eye, sw).astype(BF16)


def kernel(x, c, ctx, c_ctx, w_mod, b_mod, norm_w, w_in, conv_w, conv_b, a_log, dt_bias, d_skip,
           ssd_norm_w, lam, subln_w, w_of, w_oa, w_os, w_out, norm_f):
    batch, n, _ = x.shape
    n_ctx = ctx.shape[1]
    depth = w_mod.shape[0]
    assert n % GRID_W == 0 and n % CHUNK == 0 and n_ctx % CHUNK == 0

    o_q = 2 * F_W
    o_ag = o_q + 3 * CB
    o_xbc = o_ag + CB + D_INNER
    o_dt = o_xbc + XBC_W
    o_gt = o_dt + DT_W
    w_main = jnp.concatenate([w_in[:, :, o_xbc:o_dt], w_in[:, :, o_gt:], w_in[:, :, :o_q],
                              w_in[:, :, o_ag:o_xbc], w_in[:, :, o_q:o_ag]], axis=-1).astype(BF16)
    w_dt = jnp.pad(w_in[:, :, o_dt:o_gt], ((0, 0), (0, 0), (0, LANE - DT_W))).astype(BF16)
    conv_w8 = jnp.pad(conv_w, ((0, 0), (0, SUBLANE - CONV_W), (0, 0)))
    pad_lanes = lambda a: jnp.pad(a.reshape(depth, 1, DT_W), ((0, 0), (0, 0), (0, LANE - DT_W)))
    alog_p, dtbias_p = pad_lanes(a_log), pad_lanes(dt_bias)
    dskip_w = jnp.repeat(d_skip, SSD_P, axis=-1).reshape(depth, 1, D_INNER)
    wof_b, woa_b, wos_b, wout_b = (w.astype(BF16) for w in (w_of, w_oa, w_os, w_out))
    lam_inits = [0.8 - 0.6 * math.exp(-0.3 * l) for l in range(depth)]
    linit = jnp.asarray(np.broadcast_to(np.asarray(lam_inits, np.float32)[:, None, None],
                                        (depth, SUBLANE, LANE)))

    cos_t, sin_t = _rope_tables(n)
    tf_l = min(n, 512)
    cs_l = _seq_dft_table(n, tf_l)
    cs_c = _seq_dft_table(n_ctx, n_ctx)
    ccd, scd = _group_dft(F_W)
    emat = _head_expanders()
    zero_state = jnp.zeros((batch, 2, D_STATE, D_INNER), F32)

    mod_rows = -(-(batch + 1) // SUBLANE) * SUBLANE
    cc = jnp.concatenate([c, c_ctx[None, :], jnp.zeros((mod_rows - batch - 1, D_MODEL), F32)], axis=0)
    mod_all, lam_all = _modulation(cc, w_mod, b_mod.reshape(depth, 1, 3 * D_MODEL), lam, linit)

    tm_l = min(n, 512)
    tq_l = min(n, 1024)
    tk_l = min(n, 1024)
    tmerge_l = min(n, 256)
    rows_c = batch * n_ctx
    tm_c = math.gcd(rows_c, 512)
    cps_l = math.gcd(n // CHUNK, 4)
    cps_c = math.gcd(n_ctx // CHUNK, 2)
    xl = x.reshape(batch * n, D_MODEL)
    xc = ctx.reshape(batch * n_ctx, D_MODEL)
    for l in range(depth):
        last = l == depth - 1
        mod_l = mod_all[l, :batch].reshape(batch, 1, 3 * D_MODEL)
        mod_c = mod_all[l, batch].reshape(1, 1, 3 * D_MODEL)
        nw = norm_w[l].reshape(1, D_MODEL)
        p_c, dt_c = _inproj(xc, mod_c, nw, w_main[l], w_dt[l], None, None, rows_c, tm_c)
        p_l, dt_l = _inproj(xl, mod_l, nw, w_main[l], w_dt[l], cos_t, sin_t, n, tm_l)
        cb_ = conv_b[l].reshape(1, XBC_W)
        act_c = _conv_silu(p_c, conv_w8[l], cb_, n_ctx, n_ctx)
        act_l = _conv_silu(p_l, conv_w8[l], cb_, n, min(n, 1024))
        yf_c, yb_c, st_c = _ssd(act_c, dt_c, alog_p[l], dtbias_p[l], emat, zero_state, batch, n_ctx,
                                cps_c)
        yf_l, yb_l, _ = _ssd(act_l, dt_l, alog_p[l], dtbias_p[l], emat, st_c, batch, n, cps_l)
        sw = subln_w[l].reshape(1, ATT_V)
        att_l = _attention(p_l, p_c, p_l, lam_all[l], sw, lam_inits[l], batch, n, n_ctx, n,
                           tq_l, tk_l)
        four_l = _fourier(p_l, cs_l, ccd, scd, batch, n, tf_l)
        merge_w = (dskip_w[l], ssd_norm_w[l].reshape(1, D_INNER), wof_b[l], woa_b[l], wos_b[l],
                   wout_b[l], norm_f.reshape(1, D_MODEL))
        if not last:
            att_c = _attention(p_c, p_c, None, lam_all[l], sw, lam_inits[l], batch, n_ctx, n_ctx, 0,
                               n_ctx, 0)
            four_c = _fourier(p_c, cs_c, ccd, scd, batch, n_ctx, n_ctx)
            xc = _merge(four_c, att_c, yf_c, yb_c, act_c, p_c, xc, mod_c, *merge_w,
                        rows_c, n_ctx, False)
        xl = _merge(four_l, att_l, yf_l, yb_l, act_l, p_l, xl, mod_l, *merge_w, n, tmerge_l, last)
    return xl.reshape(batch, n, D_MODEL)
```

```python
import functools
import math

import numpy as np
import jax
import jax.numpy as jnp
from jax import lax
from jax.experimental import pallas as pl
from jax.experimental.pallas import tpu as pltpu

F32 = jnp.float32
BF16 = jnp.bfloat16
EPS = 1e-6

D_MODEL = 1024
GRID_W = 64
F_GROUP_W = 128
F_W = 512
ATT_HEADS = 4
ATT_QK = 64
ATT_V = 128
ROPE_AXIS = 32
ROPE_BASE = 10000.0
D_INNER = 512
SSD_P = 64
SSD_HEADS = 8
SSD_GROUPS = 2
D_STATE = 128
CONV_W = 5
CHUNK = 128
XBC_W = 1024
DT_W = 16
MERGE_W = 3072

LANE = 128
SUBLANE = 8
VMEM_LIMIT = 56 * 1024 * 1024

SUBLANE_BF16 = 16
CB = 512
COL_XBC, COL_GATES, COL_FU, COL_FG, COL_AG, COL_Z = 0, 2, 8, 9, 10, 11
COL_Q, COL_K, COL_V = 12, 13, 14
N_COLB = 15
N_MAIN = N_COLB * CB
PROJ_STEP_B = 3
PROJ_STEPS = N_COLB // PROJ_STEP_B
assert COL_Q == (PROJ_STEPS - 1) * PROJ_STEP_B and COL_V == N_COLB - 1
Q_SCALE = ATT_QK ** -0.5 * math.log2(math.e)


def _dot(a, b):
    return jnp.dot(a, b, preferred_element_type=F32)


def _dot_nt(a, b):
    return lax.dot_general(a, b, (((1,), (1,)), ((), ())), preferred_element_type=F32)


def _split3(x):
    x1 = x.astype(BF16)
    r1 = x - x1.astype(F32)
    x2 = r1.astype(BF16)
    x3 = (r1 - x2.astype(F32)).astype(BF16)
    return x1, x2, x3


def _dot_sel_r(x, sel):
    x1, x2, x3 = _split3(x)
    return _dot(x1, sel) + _dot(x2, sel) + _dot(x3, sel)


def _dot_sel_l(sel, x):
    x1, x2, x3 = _split3(x)
    return _dot(sel, x1) + _dot(sel, x2) + _dot(sel, x3)


def _dot_sel_r2(x, sel):
    x1 = x.astype(BF16)
    x2 = (x - x1.astype(F32)).astype(BF16)
    return _dot(x1, sel) + _dot(x2, sel)


def _sigmoid(x):
    return 0.5 * jnp.tanh(0.5 * x) + 0.5


def _silu(x):
    return x * _sigmoid(x)


def _params(*sem):
    return pltpu.CompilerParams(dimension_semantics=sem, vmem_limit_bytes=VMEM_LIMIT)


def _mod_kernel(cc_ref, w_ref, b_ref, lam_ref, linit_ref, mod_ref, lam_out_ref):
    s = _silu(cc_ref[...])
    mod_ref[0] = jnp.dot(s, w_ref[0], precision=lax.Precision.HIGHEST,
                         preferred_element_type=F32) + b_ref[0]
    lp = lam_ref[0]
    s1 = jnp.sum(lp[0:1] * lp[1:2], axis=-1, keepdims=True)
    s2 = jnp.sum(lp[2:3] * lp[3:4], axis=-1, keepdims=True)
    lam_out_ref[0] = jnp.broadcast_to(jnp.exp(s1) - jnp.exp(s2), (SUBLANE, LANE)) + linit_ref[0]


def _modulation(cc, w_mod, b_mod, lam, linit):
    depth = w_mod.shape[0]
    rows = cc.shape[0]
    tn = D_MODEL
    return pl.pallas_call(
        _mod_kernel,
        grid=(depth, 3 * D_MODEL // tn),
        in_specs=[
            pl.BlockSpec((rows, D_MODEL), lambda l, j: (0, 0)),
            pl.BlockSpec((1, D_MODEL, tn), lambda l, j: (l, 0, j)),
            pl.BlockSpec((1, 1, tn), lambda l, j: (l, 0, j)),
            pl.BlockSpec((1, 4, ATT_QK), lambda l, j: (l, 0, 0)),
            pl.BlockSpec((1, SUBLANE, LANE), lambda l, j: (l, 0, 0)),
        ],
        out_specs=[
            pl.BlockSpec((1, rows, tn), lambda l, j: (l, 0, j)),
            pl.BlockSpec((1, SUBLANE, LANE), lambda l, j: (l, 0, 0)),
        ],
        out_shape=[
            jax.ShapeDtypeStruct((depth, rows, 3 * D_MODEL), F32),
            jax.ShapeDtypeStruct((depth, SUBLANE, LANE), F32),
        ],
        compiler_params=_params("arbitrary", "arbitrary"),
        name="adaln_mod",
    )(cc, w_mod, b_mod, lam, linit)


def _rope(t, cos, sin):
    w = t.shape[-1]
    lane = lax.broadcasted_iota(jnp.int32, t.shape, 1)
    first = (lane % ROPE_AXIS) < (ROPE_AXIS // 2)
    rot = jnp.where(first, -pltpu.roll(t, w - ROPE_AXIS // 2, 1), pltpu.roll(t, ROPE_AXIS // 2, 1))
    return t * cos + rot * sin


def _inproj_kernel(*refs, rope):
    if rope:
        x_ref, mod_ref, nw_ref, w_ref, wdt_ref, cos_ref, sin_ref, p_ref, dt_ref = refs
    else:
        x_ref, mod_ref, nw_ref, w_ref, wdt_ref, p_ref, dt_ref = refs
    x = x_ref[...]
    y = x * lax.rsqrt(jnp.mean(x * x, axis=-1, keepdims=True) + EPS) * nw_ref[...]
    m = mod_ref[0]
    hb = (y * (1.0 + m[:, D_MODEL:2 * D_MODEL]) + m[:, 0:D_MODEL]).astype(BF16)
    dt_ref[...] = _dot(hb, wdt_ref[...])
    ws = PROJ_STEP_B * CB
    for j in range(PROJ_STEPS):
        acc = _dot(hb, w_ref[:, j * ws:(j + 1) * ws])
        if j < PROJ_STEPS - 1:
            for b in range(PROJ_STEP_B):
                blk, cols = j * PROJ_STEP_B + b, slice(b * CB, (b + 1) * CB)
                act = (_sigmoid if COL_GATES <= blk < COL_FU else
                       _silu if blk in (COL_FG, COL_AG, COL_Z) else (lambda t: t))
                p_ref[:, blk * CB:(blk + 1) * CB] = act(acc[:, cols]).astype(BF16)
        else:
            q, k, v = (acc[:, b * CB:(b + 1) * CB] for b in range(PROJ_STEP_B))
            if rope:
                cos, sin = cos_ref[...], sin_ref[...]
                q, k = _rope(q, cos, sin), _rope(k, cos, sin)
            p_ref[:, j * ws:j * ws + CB] = (q * Q_SCALE).astype(BF16)
            p_ref[:, j * ws + CB:j * ws + 2 * CB] = k.astype(BF16)
            p_ref[:, j * ws + 2 * CB:(j + 1) * ws] = v.astype(BF16)


def _inproj(x2, mod, norm_w, w_main, w_dt, cos, sin, n, tm):
    rows = x2.shape[0]
    tiles_per_b = n // tm
    rope = cos is not None
    resident = lambda shape: pl.BlockSpec(shape, lambda i: (0, 0), pipeline_mode=pl.Buffered(1))
    in_specs = [
        pl.BlockSpec((tm, D_MODEL), lambda i: (i, 0)),
        pl.BlockSpec((1, 1, 3 * D_MODEL), lambda i: (i // tiles_per_b, 0, 0)),
        pl.BlockSpec((1, D_MODEL), lambda i: (0, 0)),
        resident((D_MODEL, N_MAIN)),
        resident((D_MODEL, LANE)),
    ]
    args = [x2, mod, norm_w, w_main, w_dt]
    if rope:
        in_specs += [pl.BlockSpec((tm, CB), lambda i: (i % tiles_per_b, 0))] * 2
        args += [cos, sin]
    return pl.pallas_call(
        functools.partial(_inproj_kernel, rope=rope),
        grid=(rows // tm,),
        in_specs=in_specs,
        out_specs=[
            pl.BlockSpec((tm, N_MAIN), lambda i: (i, 0)),
            pl.BlockSpec((tm, LANE), lambda i: (i, 0)),
        ],
        out_shape=[
            jax.ShapeDtypeStruct((rows, N_MAIN), BF16),
            jax.ShapeDtypeStruct((rows, LANE), F32),
        ],
        compiler_params=_params("parallel"),
        name="inproj",
    )(*args)


HALO = SUBLANE_BF16


def _conv_kernel(cur_ref, prev_ref, next_ref, w_ref, b_ref, o_ref, ext_ref, *, tiles_per_b):
    i = pl.program_id(0)
    tc = cur_ref.shape[0]
    t = i % tiles_per_b
    ext_ref[0:HALO, :] = jnp.where(t == 0, 0.0, prev_ref[...].astype(F32))
    ext_ref[HALO:HALO + tc, :] = cur_ref[...].astype(F32)
    ext_ref[HALO + tc:2 * HALO + tc, :] = jnp.where(t == tiles_per_b - 1, 0.0,
                                                    next_ref[...].astype(F32))
    acc = jnp.broadcast_to(b_ref[...], o_ref.shape)
    for k in range(CONV_W):
        acc = acc + ext_ref[pl.ds(HALO - CONV_W // 2 + k, tc), :] * w_ref[k:k + 1, :]
    o_ref[...] = _silu(acc).astype(o_ref.dtype)


def _conv_silu(p, conv_w8, conv_b, n, tc):
    rows = p.shape[0]
    tiles_per_b = n // tc
    hb = tc // HALO
    last_hb = rows // HALO - 1
    return pl.pallas_call(
        functools.partial(_conv_kernel, tiles_per_b=tiles_per_b),
        grid=(rows // tc, XBC_W // CB),
        in_specs=[
            pl.BlockSpec((tc, CB), lambda i, j: (i, COL_XBC + j)),
            pl.BlockSpec((HALO, CB), lambda i, j: (jnp.maximum(i * hb - 1, 0), COL_XBC + j)),
            pl.BlockSpec((HALO, CB), lambda i, j: (jnp.minimum((i + 1) * hb, last_hb), COL_XBC + j)),
            pl.BlockSpec((SUBLANE, CB), lambda i, j: (0, j)),
            pl.BlockSpec((1, CB), lambda i, j: (0, j)),
        ],
        out_specs=pl.BlockSpec((tc, CB), lambda i, j: (i, j)),
        out_shape=jax.ShapeDtypeStruct((rows, XBC_W), BF16),
        scratch_shapes=[pltpu.VMEM((tc + 2 * HALO, CB), F32)],
        compiler_params=_params("parallel", "parallel"),
        name="conv_silu",
    )(p, p, p, conv_w8, conv_b)


def _attn_kernel(*refs, with_latent, one_minus_lam_init, n_ctx, chunks):
    if with_latent:
        (q_ref, kc_ref, vc_ref, kl_ref, vl_ref, lam_ref, sw_ref, o_ref,
         k_scr, v_scr, m_scr, acc_scr) = refs
    else:
        q_ref, kc_ref, vc_ref, lam_ref, sw_ref, o_ref, k_scr, v_scr, m_scr, acc_scr = refs
    tq = q_ref.shape[0]

    @pl.when(pl.program_id(2) == 0)
    def _():
        def put(k_ref, v_ref, off):
            rows = k_ref.shape[0]
            k = k_ref[...]
            lane = lax.broadcasted_iota(jnp.int32, k.shape, 1)
            zero = jnp.zeros_like(k)
            k_scr[0, off:off + rows, :] = jnp.where(lane < ATT_QK, k, zero)
            k_scr[1, off:off + rows, :] = jnp.where(lane >= ATT_QK, k, zero)
            v_scr[off:off + rows, 0:ATT_V] = v_ref[...]
            v_scr[off:off + rows, ATT_V:2 * ATT_V] = jnp.ones((rows, ATT_V), BF16)
        put(kc_ref, vc_ref, 0)
        if with_latent:
            put(kl_ref, vl_ref, n_ctx)

    m_scr[...] = jnp.full(m_scr.shape, -jnp.inf, F32)
    acc_scr[...] = jnp.zeros(acc_scr.shape, F32)
    q = q_ref[...]

    def chunk(off, size):
        v = v_scr[pl.ds(off, size), :]
        nt = size // LANE
        for mi in range(2):
            s = _dot_nt(q, k_scr[mi, pl.ds(off, size), :])
            tiles = [s[:, t * LANE:(t + 1) * LANE] for t in range(nt)]
            mx = functools.reduce(jnp.maximum, tiles)
            m_prev = m_scr[mi]
            m_new = jnp.maximum(m_prev, jnp.max(mx, axis=-1, keepdims=True))
            p = jnp.concatenate([jnp.exp2(t - m_new) for t in tiles], axis=1).astype(BF16)
            alpha = jnp.exp2(m_prev - m_new)
            acc_scr[mi] = jnp.concatenate([alpha, alpha], axis=1) * acc_scr[mi] + _dot(p, v)
            m_scr[mi] = m_new

    for off, size in chunks:
        chunk(off, size)

    lam = lam_ref[0:1, :]
    a0, a1 = acc_scr[0], acc_scr[1]
    o = a0[:, 0:ATT_V] / a0[:, ATT_V:] - lam * (a1[:, 0:ATT_V] / a1[:, ATT_V:])
    y = o * lax.rsqrt(jnp.mean(o * o, axis=-1, keepdims=True) + EPS) * sw_ref[...]
    o_ref[...] = (y * one_minus_lam_init).astype(o_ref.dtype)


def _attention(qkv_q, qkv_c, qkv_l, lam_l, subln_w, lam_init, batch, nq_len, nc_len, nl_len, tq, tk):
    with_latent = qkv_l is not None
    nq = nq_len // tq
    hq, hk, hv = (c * (CB // LANE) for c in (COL_Q, COL_K, COL_V))
    in_specs = [
        pl.BlockSpec((tq, LANE), lambda b, h, i: (b * nq + i, hq + h)),
        pl.BlockSpec((nc_len, LANE), lambda b, h, i: (b, hk + h)),
        pl.BlockSpec((nc_len, LANE), lambda b, h, i: (b, hv + h)),
    ]
    args = [qkv_q, qkv_c, qkv_c]
    if with_latent:
        in_specs += [
            pl.BlockSpec((nl_len, LANE), lambda b, h, i: (b, hk + h)),
            pl.BlockSpec((nl_len, LANE), lambda b, h, i: (b, hv + h)),
        ]
        args += [qkv_l, qkv_l]
    in_specs += [
        pl.BlockSpec((SUBLANE, LANE), lambda b, h, i: (0, 0)),
        pl.BlockSpec((1, ATT_V), lambda b, h, i: (0, 0)),
    ]
    args += [lam_l, subln_w]
    n_keys = nc_len + nl_len
    sizes = [nc_len + tk] + [tk] * (nl_len // tk - 1) if with_latent else [nc_len]
    chunks = tuple((sum(sizes[:t]), sizes[t]) for t in range(len(sizes)))
    assert sum(sizes) == n_keys
    return pl.pallas_call(
        functools.partial(_attn_kernel, with_latent=with_latent,
                          one_minus_lam_init=1.0 - lam_init, n_ctx=nc_len, chunks=chunks),
        grid=(batch, ATT_HEADS, nq),
        in_specs=in_specs,
        out_specs=pl.BlockSpec((tq, ATT_V), lambda b, h, i: (b * nq + i, h)),
        out_shape=jax.ShapeDtypeStruct((batch * nq_len, ATT_HEADS * ATT_V), BF16),
        scratch_shapes=[
            pltpu.VMEM((2, n_keys, LANE), BF16),
            pltpu.VMEM((n_keys, 2 * ATT_V), BF16),
            pltpu.VMEM((2, tq, LANE), F32),
            pltpu.VMEM((2, tq, 2 * ATT_V), F32),
        ],
        compiler_params=_params("parallel", "parallel", "arbitrary"),
        name="diff_attention",
    )(*args)


def _fourier_kernel(x_ref, cs_ref, cc_ref, sc_ref, o_ref, acc_ref, *, scale):
    k = pl.program_id(1)

    @pl.when(k == 0)
    def _():
        acc_ref[...] = jnp.zeros(acc_ref.shape, F32)

    xb = x_ref[...]
    xc = _dot(xb, cc_ref[...]).astype(BF16)
    xs = _dot(xb, sc_ref[...]).astype(BF16)
    acc_ref[...] += _dot(cs_ref[...], jnp.concatenate([xc, xs], axis=0))

    @pl.when(k == pl.num_programs(1) - 1)
    def _():
        o_ref[...] = (acc_ref[...] * scale).astype(o_ref.dtype)


DFT_SPLIT = 64


def _dft_tables(n, cols=None):
    cols = jnp.arange(n, dtype=jnp.int32) if cols is None else cols
    m = cols.shape[0]

    def cos_sin(rows):
        ang = ((rows[:, None] * cols[None, :]) % n).astype(F32) * (2.0 * math.pi / n)
        return jnp.cos(ang), jnp.sin(ang)

    if n <= DFT_SPLIT * DFT_SPLIT // 4 or n % DFT_SPLIT:
        return cos_sin(jnp.arange(n, dtype=jnp.int32))
    ca, sa = cos_sin(jnp.arange(n // DFT_SPLIT, dtype=jnp.int32) * DFT_SPLIT)
    cb, sb = cos_sin(jnp.arange(DFT_SPLIT, dtype=jnp.int32))
    ca, sa, cb, sb = ca[:, None, :], sa[:, None, :], cb[None, :, :], sb[None, :, :]
    return (ca * cb - sa * sb).reshape(n, m), (sa * cb + ca * sb).reshape(n, m)


def _seq_dft_table(n, tk):
    col = jnp.arange(2 * n, dtype=jnp.int32)
    k = (col // (2 * tk)) * tk + col % tk
    cn, sn = _dft_tables(n, k)
    return jnp.where(((col // tk) % 2 == 0)[None, :], cn, -sn).astype(BF16)


def _fourier(p, cs, ccd, scd, batch, n, tk):
    nk = n // tk
    scale = 1.0 / math.sqrt(n * F_GROUP_W)
    return pl.pallas_call(
        functools.partial(_fourier_kernel, scale=scale),
        grid=(batch, nk),
        in_specs=[
            pl.BlockSpec((tk, CB), lambda b, k: (b * nk + k, COL_FU)),
            pl.BlockSpec((n, 2 * tk), lambda b, k: (0, k)),
            pl.BlockSpec((F_W, F_W), lambda b, k: (0, 0)),
            pl.BlockSpec((F_W, F_W), lambda b, k: (0, 0)),
        ],
        out_specs=pl.BlockSpec((n, F_W), lambda b, k: (b, 0)),
        out_shape=jax.ShapeDtypeStruct((batch * n, F_W), BF16),
        scratch_shapes=[pltpu.VMEM((n, F_W), F32)],
        compiler_params=_params("parallel", "arbitrary"),
        name="fourier_mix",
    )(p, cs, ccd, scd)


def _softplus(x):
    return jnp.maximum(x, 0.0) + jnp.log1p(jnp.exp(-jnp.abs(x)))


def _ssd_kernel(xsf_ref, bcf_ref, dtf_ref, xsb_ref, bcb_ref, dtb_ref, alog_ref, dtbias_ref,
                e_ref, init_ref, yf_ref, yb_ref, fin_ref, state_ref):
    c = pl.program_id(1)
    q = CHUNK
    cps = xsf_ref.shape[0] // q
    gw = D_INNER // SSD_GROUPS
    hpg = SSD_HEADS // SSD_GROUPS

    @pl.when(c == 0)
    def _():
        state_ref[...] = init_ref[0]

    row = lax.broadcasted_iota(jnp.int32, (q, q), 0)
    col = lax.broadcasted_iota(jnp.int32, (q, q), 1)
    head_of_lane = lax.broadcasted_iota(jnp.int32, (q, gw), 1) // SSD_P
    neg_a = -jnp.exp(alog_ref[...])
    dtbias = dtbias_ref[...]

    def expand(x, e2):
        x1 = x.astype(BF16)
        x2 = (x - x1.astype(F32)).astype(BF16)
        return _dot(jnp.concatenate([x1, x2], axis=1), e2)

    def one_chunk(d, rows, xs_ref, bc_ref, dt_ref, y_ref):
        fwd = d == 0
        tri = (row >= col) if fwd else (row <= col)
        tri_b = tri.astype(F32).astype(BF16)
        dt = _softplus(dt_ref[rows, :] + dtbias)
        a1, a2, a3 = _split3(dt * neg_a)
        acs = (_dot(jnp.concatenate([tri_b, tri_b], axis=1), jnp.concatenate([a1, a2], axis=0))
               + _dot(tri_b, a3))
        acs_t = acs.T
        last = q - 1 if fwd else 0
        tot_row = acs[last:last + 1, :]
        e2 = e_ref[d]
        dt_w = expand(dt, e2)
        eacs_w = expand(jnp.exp(acs), e2)
        dec_w = expand(jnp.exp(tot_row - acs), e2)
        sdec = _dot_sel_r(jnp.broadcast_to(jnp.exp(tot_row), (SUBLANE, LANE)), e2[0:LANE])[0:1, :]
        xd = xs_ref[rows, :].astype(F32) * dt_w
        xdb = xd.astype(BF16)
        xdd = (xd * dec_w).astype(BF16)
        bc = bc_ref[rows, :]
        for g in range(SSD_GROUPS):
            bg = bc[:, g * D_STATE:(g + 1) * D_STATE]
            cg = bc[:, (SSD_GROUPS + g) * D_STATE:(SSD_GROUPS + g + 1) * D_STATE]
            cb = _dot_nt(cg, bg)
            lanes = slice(g * gw, (g + 1) * gw)
            s_g = state_ref[d, :, lanes]
            y = _dot(cg, s_g.astype(BF16)) * eacs_w[:, lanes]
            xg = xdb[:, lanes]
            zero = jnp.zeros_like(xg)
            for r in range(0, hpg, 2):
                mats, xms = [], []
                for rr in (r, r + 1):
                    jl = d * SSD_HEADS + g * hpg + rr
                    seg = jnp.where(tri, acs[:, jl:jl + 1] - acs_t[jl:jl + 1, :], -jnp.inf)
                    mats.append((cb * jnp.exp(seg)).astype(BF16))
                    xms.append(jnp.where(head_of_lane == rr, xg, zero))
                y = y + _dot(jnp.concatenate(mats, axis=1), jnp.concatenate(xms, axis=0))
            y_ref[rows, lanes] = y.astype(y_ref.dtype)
            bg_t = bg.astype(F32).T.astype(BF16)
            state_ref[d, :, lanes] = s_g * sdec[:, lanes] + _dot(bg_t, xdd[:, lanes])

    for i in range(cps):
        one_chunk(0, pl.ds(i * q, q), xsf_ref, bcf_ref, dtf_ref, yf_ref)
        one_chunk(1, pl.ds((cps - 1 - i) * q, q), xsb_ref, bcb_ref, dtb_ref, yb_ref)

    @pl.when(c == pl.num_programs(1) - 1)
    def _():
        fin_ref[0] = state_ref[...]


def _ssd(act, dt, alog, dtbias, emat2, init, batch, n, cps):
    nc = n // (CHUNK * cps)
    tr = CHUNK * cps
    fidx = lambda b, c: b * nc + c
    bidx = lambda b, c: b * nc + (nc - 1 - c)
    st_shape = (2, D_STATE, D_INNER)
    return pl.pallas_call(
        _ssd_kernel,
        grid=(batch, nc),
        in_specs=[
            pl.BlockSpec((tr, CB), lambda b, c: (fidx(b, c), 0)),
            pl.BlockSpec((tr, CB), lambda b, c: (fidx(b, c), 1)),
            pl.BlockSpec((tr, LANE), lambda b, c: (fidx(b, c), 0)),
            pl.BlockSpec((tr, CB), lambda b, c: (bidx(b, c), 0)),
            pl.BlockSpec((tr, CB), lambda b, c: (bidx(b, c), 1)),
            pl.BlockSpec((tr, LANE), lambda b, c: (bidx(b, c), 0)),
            pl.BlockSpec((1, LANE), lambda b, c: (0, 0)),
            pl.BlockSpec((1, LANE), lambda b, c: (0, 0)),
            pl.BlockSpec((2, 2 * LANE, D_INNER), lambda b, c: (0, 0, 0)),
            pl.BlockSpec((1,) + st_shape, lambda b, c: (b, 0, 0, 0)),
        ],
        out_specs=[
            pl.BlockSpec((tr, D_INNER), lambda b, c: (fidx(b, c), 0)),
            pl.BlockSpec((tr, D_INNER), lambda b, c: (bidx(b, c), 0)),
            pl.BlockSpec((1,) + st_shape, lambda b, c: (b, 0, 0, 0)),
        ],
        out_shape=[
            jax.ShapeDtypeStruct((batch * n, D_INNER), BF16),
            jax.ShapeDtypeStruct((batch * n, D_INNER), BF16),
            jax.ShapeDtypeStruct((batch,) + st_shape, F32),
        ],
        scratch_shapes=[pltpu.VMEM(st_shape, F32)],
        compiler_params=_params("parallel", "arbitrary"),
        name="ssd_scan",
    )(act, act, dt, act, act, dt, alog, dtbias, emat2, init)


def _merge_kernel(four_ref, att_ref, yf_ref, yb_ref, xs_ref, fg_ref, ag_ref, z_ref,
                  g0_ref, g1_ref, g2_ref, x_ref, mod_ref, dskip_ref, snw_ref,
                  wof_ref, woa_ref, wos_ref, wout_ref, nf_ref, o_ref, *, final_norm):
    f = lambda ref: ref[...].astype(F32)
    y_f = _dot((f(four_ref) * f(fg_ref)).astype(BF16), wof_ref[...])
    y_a = _dot((f(att_ref) * f(ag_ref)).astype(BF16), woa_ref[...])
    ys = f(yf_ref) + f(yb_ref) + dskip_ref[...] * f(xs_ref)
    t = ys * f(z_ref)
    t = t * lax.rsqrt(jnp.mean(t * t, axis=-1, keepdims=True) + EPS) * snw_ref[...]
    y_s = _dot(t.astype(BF16), wos_ref[...])
    y = f(g0_ref) * y_f + f(g1_ref) * y_a + f(g2_ref) * y_s
    out = _dot(y.astype(BF16), wout_ref[...])
    xn = x_ref[...] + mod_ref[0][:, 2 * D_MODEL:3 * D_MODEL] * out
    if final_norm:
        xn = xn * lax.rsqrt(jnp.mean(xn * xn, axis=-1, keepdims=True) + EPS) * nf_ref[...]
    o_ref[...] = xn


def _merge(four, att, yf, yb, act, p, x2, mod, dskip_w, snw, wof, woa, wos, wout, norm_f,
           n, tm, final_norm):
    rows = x2.shape[0]
    tiles_per_b = n // tm
    gcol = COL_GATES * CB // D_MODEL
    row_blk = lambda w, cidx: pl.BlockSpec((tm, w), lambda i: (i, cidx))
    const = lambda shape: pl.BlockSpec(shape, lambda i: (0,) * len(shape))
    return pl.pallas_call(
        functools.partial(_merge_kernel, final_norm=final_norm),
        grid=(rows // tm,),
        in_specs=[
            row_blk(F_W, 0), row_blk(CB, 0), row_blk(D_INNER, 0), row_blk(D_INNER, 0),
            row_blk(CB, 0),
            row_blk(CB, COL_FG), row_blk(CB, COL_AG), row_blk(CB, COL_Z),
            row_blk(D_MODEL, gcol), row_blk(D_MODEL, gcol + 1), row_blk(D_MODEL, gcol + 2),
            row_blk(D_MODEL, 0),
            pl.BlockSpec((1, 1, 3 * D_MODEL), lambda i: (i // tiles_per_b, 0, 0)),
            const((1, D_INNER)), const((1, D_INNER)),
            const((F_W, D_MODEL)), const((CB, D_MODEL)), const((D_INNER, D_MODEL)),
            const((D_MODEL, D_MODEL)), const((1, D_MODEL)),
        ],
        out_specs=pl.BlockSpec((tm, D_MODEL), lambda i: (i, 0)),
        out_shape=jax.ShapeDtypeStruct((rows, D_MODEL), F32),
        compiler_params=_params("parallel"),
        name="branch_merge",
    )(four, att, yf, yb, act, p, p, p, p, p, p, x2, mod, dskip_w, snw, wof, woa, wos, wout, norm_f)


def _rope_tables(n):
    rows = n // GRID_W
    row = jnp.repeat(jnp.arange(rows, dtype=F32), GRID_W)
    col = jnp.tile(jnp.arange(GRID_W, dtype=F32), rows)
    freqs = ROPE_BASE ** (-jnp.arange(0, ROPE_AXIS, 2, dtype=F32) / ROPE_AXIS)
    ang_r = row[:, None] * freqs
    ang_c = col[:, None] * freqs
    ang = jnp.concatenate([ang_r, ang_r, ang_c, ang_c], axis=-1)
    reps = CB // ATT_QK
    return jnp.tile(jnp.cos(ang), (1, reps)), jnp.tile(jnp.sin(ang), (1, reps))


def _head_expanders():
    e = np.zeros((2, LANE, D_INNER), np.float32)
    for d in range(2):
        for h in range(SSD_HEADS):
            e[d, d * SSD_HEADS + h, h * SSD_P:(h + 1) * SSD_P] = 1.0
    return jnp.asarray(np.concatenate([e, e], axis=1), BF16)


def _group_dft(w):
    cw, sw = _dft_tables(F_GROUP_W)
    eye = jnp.eye(w // F_GROUP_W, dtype=F32)
    return jnp.kron(eye, cw).astype(BF16), jnp.kron(eye, sw).astype(BF16)


def kernel(x, c, ctx, c_ctx, w_mod, b_mod, norm_w, w_in, conv_w, conv_b, a_log, dt_bias, d_skip,
           ssd_norm_w, lam, subln_w, w_of, w_oa, w_os, w_out, norm_f):
    batch, n, _ = x.shape
    n_ctx = ctx.shape[1]
    depth = w_mod.shape[0]
    assert n % GRID_W == 0 and n % CHUNK == 0 and n_ctx % CHUNK == 0

    o_q = 2 * F_W
    o_ag = o_q + 3 * CB
    o_xbc = o_ag + CB + D_INNER
    o_dt = o_xbc + XBC_W
    o_gt = o_dt + DT_W
    w_main = jnp.concatenate([w_in[:, :, o_xbc:o_dt], w_in[:, :, o_gt:], w_in[:, :, :o_q],
                              w_in[:, :, o_ag:o_xbc], w_in[:, :, o_q:o_ag]], axis=-1).astype(BF16)
    w_dt = jnp.pad(w_in[:, :, o_dt:o_gt], ((0, 0), (0, 0), (0, LANE - DT_W))).astype(BF16)
    conv_w8 = jnp.pad(conv_w, ((0, 0), (0, SUBLANE - CONV_W), (0, 0)))
    pad_lanes = lambda a: jnp.pad(a.reshape(depth, 1, DT_W), ((0, 0), (0, 0), (0, LANE - DT_W)))
    alog_p, dtbias_p = pad_lanes(a_log), pad_lanes(dt_bias)
    dskip_w = jnp.repeat(d_skip, SSD_P, axis=-1).reshape(depth, 1, D_INNER)
    wof_b, woa_b, wos_b, wout_b = (w.astype(BF16) for w in (w_of, w_oa, w_os, w_out))
    lam_inits = [0.8 - 0.6 * math.exp(-0.3 * l) for l in range(depth)]
    linit = jnp.asarray(np.broadcast_to(np.asarray(lam_inits, np.float32)[:, None, None],
                                        (depth, SUBLANE, LANE)))

    cos_t, sin_t = _rope_tables(n)
    tf_l = min(n, 512)
    cs_l = _seq_dft_table(n, tf_l)
    cs_c = _seq_dft_table(n_ctx, n_ctx)
    ccd, scd = _group_dft(F_W)
    emat = _head_expanders()
    zero_state = jnp.zeros((batch, 2, D_STATE, D_INNER), F32)

    mod_rows = -(-(batch + 1) // SUBLANE) * SUBLANE
    cc = jnp.concatenate([c, c_ctx[None, :], jnp.zeros((mod_rows - batch - 1, D_MODEL), F32)], axis=0)
    mod_all, lam_all = _modulation(cc, w_mod, b_mod.reshape(depth, 1, 3 * D_MODEL), lam, linit)

    tm_l = min(n, 512)
    tq_l = min(n, 1024)
    tk_l = min(n, 1024)
    tmerge_l = min(n, 512)
    rows_c = batch * n_ctx
    tm_c = math.gcd(rows_c, 512)
    cps_l = math.gcd(n // CHUNK, 8)
    cps_c = math.gcd(n_ctx // CHUNK, 2)
    xl = x.reshape(batch * n, D_MODEL)
    xc = ctx.reshape(batch * n_ctx, D_MODEL)
    for l in range(depth):
        last = l == depth - 1
        mod_l = mod_all[l, :batch].reshape(batch, 1, 3 * D_MODEL)
        mod_c = mod_all[l, batch].reshape(1, 1, 3 * D_MODEL)
        nw = norm_w[l].reshape(1, D_MODEL)
        p_c, dt_c = _inproj(xc, mod_c, nw, w_main[l], w_dt[l], None, None, rows_c, tm_c)
        p_l, dt_l = _inproj(xl, mod_l, nw, w_main[l], w_dt[l], cos_t, sin_t, n, tm_l)
        cb_ = conv_b[l].reshape(1, XBC_W)
        act_c = _conv_silu(p_c, conv_w8[l], cb_, n_ctx, n_ctx)
        act_l = _conv_silu(p_l, conv_w8[l], cb_, n, min(n, 1024))
        yf_c, yb_c, st_c = _ssd(act_c, dt_c, alog_p[l], dtbias_p[l], emat, zero_state, batch, n_ctx,
                                cps_c)
        yf_l, yb_l, _ = _ssd(act_l, dt_l, alog_p[l], dtbias_p[l], emat, st_c, batch, n, cps_l)
        sw = subln_w[l].reshape(1, ATT_V)
        att_l = _attention(p_l, p_c, p_l, lam_all[l], sw, lam_inits[l], batch, n, n_ctx, n,
                           tq_l, tk_l)
        four_l = _fourier(p_l, cs_l, ccd, scd, batch, n, tf_l)
        merge_w = (dskip_w[l], ssd_norm_w[l].reshape(1, D_INNER), wof_b[l], woa_b[l], wos_b[l],
                   wout_b[l], norm_f.reshape(1, D_MODEL))
        if not last:
            att_c = _attention(p_c, p_c, None, lam_all[l], sw, lam_inits[l], batch, n_ctx, n_ctx, 0,
                               n_ctx, 0)
            four_c = _fourier(p_c, cs_c, ccd, scd, batch, n_ctx, n_ctx)
            xc = _merge(four_c, att_c, yf_c, yb_c, act_c, p_c, xc, mod_c, *merge_w,
                        rows_c, n_ctx, False)
        xl = _merge(four_l, att_l, yf_l, yb_l, act_l, p_l, xl, mod_l, *merge_w, n, tmerge_l, last)
    return xl.reshape(batch, n, D_MODEL)
```

```python
import functools
import math

import numpy as np
import jax
import jax.numpy as jnp
from jax import lax
from jax.experimental import pallas as pl
from jax.experimental.pallas import tpu as pltpu

F32 = jnp.float32
BF16 = jnp.bfloat16
EPS = 1e-6

D_MODEL = 1024
GRID_W = 64
F_GROUP_W = 128
F_W = 512
ATT_HEADS = 4
ATT_QK = 64
ATT_V = 128
ROPE_AXIS = 32
ROPE_BASE = 10000.0
D_INNER = 512
SSD_P = 64
SSD_HEADS = 8
SSD_GROUPS = 2
D_STATE = 128
CONV_W = 5
CHUNK = 128
XBC_W = 1024
DT_W = 16
MERGE_W = 3072

LANE = 128
SUBLANE = 8
VMEM_LIMIT = 56 * 1024 * 1024

SUBLANE_BF16 = 16
CB = 512
COL_XBC, COL_GATES, COL_FU, COL_FG, COL_AG, COL_Z = 0, 2, 8, 9, 10, 11
COL_Q, COL_K, COL_V = 12, 13, 14
N_COLB = 15
N_MAIN = N_COLB * CB
PROJ_STEP_B = 3
PROJ_STEPS = N_COLB // PROJ_STEP_B
assert COL_Q == (PROJ_STEPS - 1) * PROJ_STEP_B and COL_V == N_COLB - 1
Q_SCALE = ATT_QK ** -0.5 * math.log2(math.e)


def _dot(a, b):
    return jnp.dot(a, b, preferred_element_type=F32)


def _dot_nt(a, b):
    return lax.dot_general(a, b, (((1,), (1,)), ((), ())), preferred_element_type=F32)


def _split3(x):
    x1 = x.astype(BF16)
    r1 = x - x1.astype(F32)
    x2 = r1.astype(BF16)
    x3 = (r1 - x2.astype(F32)).astype(BF16)
    return x1, x2, x3


def _dot_sel_r(x, sel):
    x1, x2, x3 = _split3(x)
    return _dot(x1, sel) + _dot(x2, sel) + _dot(x3, sel)


def _dot_sel_l(sel, x):
    x1, x2, x3 = _split3(x)
    return _dot(sel, x1) + _dot(sel, x2) + _dot(sel, x3)


def _dot_sel_r2(x, sel):
    x1 = x.astype(BF16)
    x2 = (x - x1.astype(F32)).astype(BF16)
    return _dot(x1, sel) + _dot(x2, sel)


def _sigmoid(x):
    return 0.5 * jnp.tanh(0.5 * x) + 0.5


def _silu(x):
    return x * _sigmoid(x)


def _params(*sem):
    return pltpu.CompilerParams(dimension_semantics=sem, vmem_limit_bytes=VMEM_LIMIT)


def _mod_kernel(cc_ref, w_ref, b_ref, lam_ref, linit_ref, mod_ref, lam_out_ref):
    s = _silu(cc_ref[...])
    mod_ref[0] = jnp.dot(s, w_ref[0], precision=lax.Precision.HIGHEST,
                         preferred_element_type=F32) + b_ref[0]
    lp = lam_ref[0]
    s1 = jnp.sum(lp[0:1] * lp[1:2], axis=-1, keepdims=True)
    s2 = jnp.sum(lp[2:3] * lp[3:4], axis=-1, keepdims=True)
    lam_out_ref[0] = jnp.broadcast_to(jnp.exp(s1) - jnp.exp(s2), (SUBLANE, LANE)) + linit_ref[0]


def _modulation(cc, w_mod, b_mod, lam, linit):
    depth = w_mod.shape[0]
    rows = cc.shape[0]
    tn = D_MODEL
    return pl.pallas_call(
        _mod_kernel,
        grid=(depth, 3 * D_MODEL // tn),
        in_specs=[
            pl.BlockSpec((rows, D_MODEL), lambda l, j: (0, 0)),
            pl.BlockSpec((1, D_MODEL, tn), lambda l, j: (l, 0, j)),
            pl.BlockSpec((1, 1, tn), lambda l, j: (l, 0, j)),
            pl.BlockSpec((1, 4, ATT_QK), lambda l, j: (l, 0, 0)),
            pl.BlockSpec((1, SUBLANE, LANE), lambda l, j: (l, 0, 0)),
        ],
        out_specs=[
            pl.BlockSpec((1, rows, tn), lambda l, j: (l, 0, j)),
            pl.BlockSpec((1, SUBLANE, LANE), lambda l, j: (l, 0, 0)),
        ],
        out_shape=[
            jax.ShapeDtypeStruct((depth, rows, 3 * D_MODEL), F32),
            jax.ShapeDtypeStruct((depth, SUBLANE, LANE), F32),
        ],
        compiler_params=_params("arbitrary", "arbitrary"),
        name="adaln_mod",
    )(cc, w_mod, b_mod, lam, linit)


def _rope(t, cos, sin):
    w = t.shape[-1]
    lane = lax.broadcasted_iota(jnp.int32, t.shape, 1)
    first = (lane % ROPE_AXIS) < (ROPE_AXIS // 2)
    rot = jnp.where(first, -pltpu.roll(t, w - ROPE_AXIS // 2, 1), pltpu.roll(t, ROPE_AXIS // 2, 1))
    return t * cos + rot * sin


HALO = SUBLANE


def _inproj_kernel(*refs, rope, seq_tiles):
    if rope:
        (x_ref, xp_ref, xn_ref, mod_ref, nw_ref, w_ref, wdt_ref, cw_ref, cb_ref, cos_ref, sin_ref,
         p_ref, dt_ref, ext_ref) = refs
    else:
        (x_ref, xp_ref, xn_ref, mod_ref, nw_ref, w_ref, wdt_ref, cw_ref, cb_ref,
         p_ref, dt_ref, ext_ref) = refs
    tm = x_ref.shape[0]
    t = pl.program_id(0) % seq_tiles
    m = mod_ref[0]

    def norm_mod(x):
        y = x * lax.rsqrt(jnp.mean(x * x, axis=-1, keepdims=True) + EPS) * nw_ref[...]
        return (y * (1.0 + m[:, D_MODEL:2 * D_MODEL]) + m[:, 0:D_MODEL]).astype(BF16)

    hb = norm_mod(x_ref[...])
    dt_ref[...] = _dot(hb, wdt_ref[...])
    halo = _dot(norm_mod(jnp.concatenate([xp_ref[...], xn_ref[...]], axis=0)), w_ref[:, 0:XBC_W])
    ext_ref[0:HALO, :] = jnp.where(t == 0, 0.0, halo[0:HALO])
    ext_ref[HALO + tm:2 * HALO + tm, :] = jnp.where(t == seq_tiles - 1, 0.0, halo[HALO:2 * HALO])
    ws = PROJ_STEP_B * CB
    for j in range(PROJ_STEPS):
        acc = _dot(hb, w_ref[:, j * ws:(j + 1) * ws])
        if j < PROJ_STEPS - 1:
            for b in range(PROJ_STEP_B):
                blk, cols = j * PROJ_STEP_B + b, slice(b * CB, (b + 1) * CB)
                if blk * CB < XBC_W:
                    ext_ref[HALO:HALO + tm, blk * CB:(blk + 1) * CB] = acc[:, cols]
                    continue
                act = (_sigmoid if COL_GATES <= blk < COL_FU else
                       _silu if blk in (COL_FG, COL_AG, COL_Z) else (lambda t: t))
                p_ref[:, blk * CB:(blk + 1) * CB] = act(acc[:, cols]).astype(BF16)
        else:
            q, k, v = (acc[:, b * CB:(b + 1) * CB] for b in range(PROJ_STEP_B))
            if rope:
                cos, sin = cos_ref[...], sin_ref[...]
                q, k = _rope(q, cos, sin), _rope(k, cos, sin)
            p_ref[:, j * ws:j * ws + CB] = (q * Q_SCALE).astype(BF16)
            p_ref[:, j * ws + CB:j * ws + 2 * CB] = k.astype(BF16)
            p_ref[:, j * ws + 2 * CB:(j + 1) * ws] = v.astype(BF16)
    for cols in (slice(c0, c0 + CB) for c0 in range(0, XBC_W, CB)):
        conv = jnp.broadcast_to(cb_ref[:, cols], (tm, CB))
        for k in range(CONV_W):
            conv = conv + ext_ref[pl.ds(HALO - CONV_W // 2 + k, tm), cols] * cw_ref[k:k + 1, cols]
        p_ref[:, cols] = _silu(conv).astype(BF16)


def _inproj(x2, mod, norm_w, w_main, w_dt, conv_w8, conv_b, cos, sin, n_mod, n_seq, tm):
    rows = x2.shape[0]
    mod_tiles, seq_tiles = n_mod // tm, n_seq // tm
    hb = tm // HALO
    last_hb = rows // HALO - 1
    rope = cos is not None
    resident = lambda shape: pl.BlockSpec(shape, lambda i: (0, 0), pipeline_mode=pl.Buffered(1))
    in_specs = [
        pl.BlockSpec((tm, D_MODEL), lambda i: (i, 0)),
        pl.BlockSpec((HALO, D_MODEL), lambda i: (jnp.maximum(i * hb - 1, 0), 0)),
        pl.BlockSpec((HALO, D_MODEL), lambda i: (jnp.minimum((i + 1) * hb, last_hb), 0)),
        pl.BlockSpec((1, 1, 3 * D_MODEL), lambda i: (i // mod_tiles, 0, 0)),
        pl.BlockSpec((1, D_MODEL), lambda i: (0, 0)),
        resident((D_MODEL, N_MAIN)),
        resident((D_MODEL, LANE)),
        pl.BlockSpec((SUBLANE, XBC_W), lambda i: (0, 0)),
        pl.BlockSpec((1, XBC_W), lambda i: (0, 0)),
    ]
    args = [x2, x2, x2, mod, norm_w, w_main, w_dt, conv_w8, conv_b]
    if rope:
        in_specs += [pl.BlockSpec((tm, CB), lambda i: (i % seq_tiles, 0))] * 2
        args += [cos, sin]
    return pl.pallas_call(
        functools.partial(_inproj_kernel, rope=rope, seq_tiles=seq_tiles),
        grid=(rows // tm,),
        in_specs=in_specs,
        out_specs=[
            pl.BlockSpec((tm, N_MAIN), lambda i: (i, 0)),
            pl.BlockSpec((tm, LANE), lambda i: (i, 0)),
        ],
        out_shape=[
            jax.ShapeDtypeStruct((rows, N_MAIN), BF16),
            jax.ShapeDtypeStruct((rows, LANE), F32),
        ],
        scratch_shapes=[pltpu.VMEM((tm + 2 * HALO, XBC_W), F32)],
        compiler_params=_params("parallel"),
        name="inproj",
    )(*args)


def _attn_kernel(*refs, with_latent, one_minus_lam_init, n_ctx, chunks):
    if with_latent:
        (q_ref, kc_ref, vc_ref, kl_ref, vl_ref, lam_ref, sw_ref, o_ref,
         k_scr, v_scr, m_scr, acc_scr) = refs
    else:
        q_ref, kc_ref, vc_ref, lam_ref, sw_ref, o_ref, k_scr, v_scr, m_scr, acc_scr = refs
    tq = q_ref.shape[0]

    @pl.when(pl.program_id(2) == 0)
    def _():
        def put(k_ref, v_ref, off):
            rows = k_ref.shape[0]
            k = k_ref[...]
            lane = lax.broadcasted_iota(jnp.int32, k.shape, 1)
            zero = jnp.zeros_like(k)
            k_scr[0, off:off + rows, :] = jnp.where(lane < ATT_QK, k, zero)
            k_scr[1, off:off + rows, :] = jnp.where(lane >= ATT_QK, k, zero)
            v_scr[off:off + rows, 0:ATT_V] = v_ref[...]
            v_scr[off:off + rows, ATT_V:2 * ATT_V] = jnp.ones((rows, ATT_V), BF16)
        put(kc_ref, vc_ref, 0)
        if with_latent:
            put(kl_ref, vl_ref, n_ctx)

    m_scr[...] = jnp.full(m_scr.shape, -jnp.inf, F32)
    acc_scr[...] = jnp.zeros(acc_scr.shape, F32)
    q = q_ref[...]

    def chunk(off, size):
        v = v_scr[pl.ds(off, size), :]
        nt = size // LANE
        for mi in range(2):
            s = _dot_nt(q, k_scr[mi, pl.ds(off, size), :])
            tiles = [s[:, t * LANE:(t + 1) * LANE] for t in range(nt)]
            mx = functools.reduce(jnp.maximum, tiles)
            m_prev = m_scr[mi]
            m_new = jnp.maximum(m_prev, jnp.max(mx, axis=-1, keepdims=True))
            p = jnp.concatenate([jnp.exp2(t - m_new) for t in tiles], axis=1).astype(BF16)
            alpha = jnp.exp2(m_prev - m_new)
            acc_scr[mi] = jnp.concatenate([alpha, alpha], axis=1) * acc_scr[mi] + _dot(p, v)
            m_scr[mi] = m_new

    for off, size in chunks:
        chunk(off, size)

    lam = lam_ref[0:1, :]
    a0, a1 = acc_scr[0], acc_scr[1]
    o = a0[:, 0:ATT_V] / a0[:, ATT_V:] - lam * (a1[:, 0:ATT_V] / a1[:, ATT_V:])
    y = o * lax.rsqrt(jnp.mean(o * o, axis=-1, keepdims=True) + EPS) * sw_ref[...]
    o_ref[...] = (y * one_minus_lam_init).astype(o_ref.dtype)


def _attention(qkv_q, qkv_c, qkv_l, lam_l, subln_w, lam_init, batch, nq_len, nc_len, nl_len, tq, tk):
    with_latent = qkv_l is not None
    nq = nq_len // tq
    hq, hk, hv = (c * (CB // LANE) for c in (COL_Q, COL_K, COL_V))
    in_specs = [
        pl.BlockSpec((tq, LANE), lambda b, h, i: (b * nq + i, hq + h)),
        pl.BlockSpec((nc_len, LANE), lambda b, h, i: (b, hk + h)),
        pl.BlockSpec((nc_len, LANE), lambda b, h, i: (b, hv + h)),
    ]
    args = [qkv_q, qkv_c, qkv_c]
    if with_latent:
        in_specs += [
            pl.BlockSpec((nl_len, LANE), lambda b, h, i: (b, hk + h)),
            pl.BlockSpec((nl_len, LANE), lambda b, h, i: (b, hv + h)),
        ]
        args += [qkv_l, qkv_l]
    in_specs += [
        pl.BlockSpec((SUBLANE, LANE), lambda b, h, i: (0, 0)),
        pl.BlockSpec((1, ATT_V), lambda b, h, i: (0, 0)),
    ]
    args += [lam_l, subln_w]
    n_keys = nc_len + nl_len
    sizes = [nc_len + tk] + [tk] * (nl_len // tk - 1) if with_latent else [nc_len]
    chunks = tuple((sum(sizes[:t]), sizes[t]) for t in range(len(sizes)))
    assert sum(sizes) == n_keys
    return pl.pallas_call(
        functools.partial(_attn_kernel, with_latent=with_latent,
                          one_minus_lam_init=1.0 - lam_init, n_ctx=nc_len, chunks=chunks),
        grid=(batch, ATT_HEADS, nq),
        in_specs=in_specs,
        out_specs=pl.BlockSpec((tq, ATT_V), lambda b, h, i: (b * nq + i, h)),
        out_shape=jax.ShapeDtypeStruct((batch * nq_len, ATT_HEADS * ATT_V), BF16),
        scratch_shapes=[
            pltpu.VMEM((2, n_keys, LANE), BF16),
            pltpu.VMEM((n_keys, 2 * ATT_V), BF16),
            pltpu.VMEM((2, tq, LANE), F32),
            pltpu.VMEM((2, tq, 2 * ATT_V), F32),
        ],
        compiler_params=_params("parallel", "parallel", "arbitrary"),
        name="diff_attention",
    )(*args)


def _fourier_kernel(x_ref, cs_ref, cc_ref, sc_ref, o_ref, acc_ref, *, scale):
    k = pl.program_id(1)

    @pl.when(k == 0)
    def _():
        acc_ref[...] = jnp.zeros(acc_ref.shape, F32)

    xb = x_ref[...]
    xc = _dot(xb, cc_ref[...]).astype(BF16)
    xs = _dot(xb, sc_ref[...]).astype(BF16)
    acc_ref[...] += _dot(cs_ref[...], jnp.concatenate([xc, xs], axis=0))

    @pl.when(k == pl.num_programs(1) - 1)
    def _():
        o_ref[...] = (acc_ref[...] * scale).astype(o_ref.dtype)


DFT_SPLIT = 64


def _dft_tables(n, cols=None):
    cols = jnp.arange(n, dtype=jnp.int32) if cols is None else cols
    m = cols.shape[0]

    def cos_sin(rows):
        ang = ((rows[:, None] * cols[None, :]) % n).astype(F32) * (2.0 * math.pi / n)
        return jnp.cos(ang), jnp.sin(ang)

    if n <= DFT_SPLIT * DFT_SPLIT // 4 or n % DFT_SPLIT:
        return cos_sin(jnp.arange(n, dtype=jnp.int32))
    ca, sa = cos_sin(jnp.arange(n // DFT_SPLIT, dtype=jnp.int32) * DFT_SPLIT)
    cb, sb = cos_sin(jnp.arange(DFT_SPLIT, dtype=jnp.int32))
    ca, sa, cb, sb = ca[:, None, :], sa[:, None, :], cb[None, :, :], sb[None, :, :]
    return (ca * cb - sa * sb).reshape(n, m), (sa * cb + ca * sb).reshape(n, m)


def _seq_dft_table(n, tk):
    col = jnp.arange(2 * n, dtype=jnp.int32)
    k = (col // (2 * tk)) * tk + col % tk
    cn, sn = _dft_tables(n, k)
    return jnp.where(((col // tk) % 2 == 0)[None, :], cn, -sn).astype(BF16)


def _fourier(p, cs, ccd, scd, batch, n, tk):
    nk = n // tk
    scale = 1.0 / math.sqrt(n * F_GROUP_W)
    return pl.pallas_call(
        functools.partial(_fourier_kernel, scale=scale),
        grid=(batch, nk),
        in_specs=[
            pl.BlockSpec((tk, CB), lambda b, k: (b * nk + k, COL_FU)),
            pl.BlockSpec((n, 2 * tk), lambda b, k: (0, k)),
            pl.BlockSpec((F_W, F_W), lambda b, k: (0, 0)),
            pl.BlockSpec((F_W, F_W), lambda b, k: (0, 0)),
        ],
        out_specs=pl.BlockSpec((n, F_W), lambda b, k: (b, 0)),
        out_shape=jax.ShapeDtypeStruct((batch * n, F_W), BF16),
        scratch_shapes=[pltpu.VMEM((n, F_W), F32)],
        compiler_params=_params("parallel", "arbitrary"),
        name="fourier_mix",
    )(p, cs, ccd, scd)


def _softplus(x):
    return jnp.maximum(x, 0.0) + jnp.log1p(jnp.exp(-jnp.abs(x)))


def _ssd_kernel(xsf_ref, bcf_ref, dtf_ref, xsb_ref, bcb_ref, dtb_ref, alog_ref, dtbias_ref,
                e_ref, init_ref, yf_ref, yb_ref, fin_ref, state_ref):
    c = pl.program_id(1)
    q = CHUNK
    cps = xsf_ref.shape[0] // q
    gw = D_INNER // SSD_GROUPS
    hpg = SSD_HEADS // SSD_GROUPS

    @pl.when(c == 0)
    def _():
        state_ref[...] = init_ref[0]

    row = lax.broadcasted_iota(jnp.int32, (q, q), 0)
    col = lax.broadcasted_iota(jnp.int32, (q, q), 1)
    head_of_lane = lax.broadcasted_iota(jnp.int32, (q, gw), 1) // SSD_P
    neg_a = -jnp.exp(alog_ref[...])
    dtbias = dtbias_ref[...]

    def expand(x, e2):
        x1 = x.astype(BF16)
        x2 = (x - x1.astype(F32)).astype(BF16)
        return _dot(jnp.concatenate([x1, x2], axis=1), e2)

    def one_chunk(d, rows, xs_ref, bc_ref, dt_ref, y_ref):
        fwd = d == 0
        tri = (row >= col) if fwd else (row <= col)
        tri_b = tri.astype(F32).astype(BF16)
        dt = _softplus(dt_ref[rows, :] + dtbias)
        a1, a2, a3 = _split3(dt * neg_a)
        acs = (_dot(jnp.concatenate([tri_b, tri_b], axis=1), jnp.concatenate([a1, a2], axis=0))
               + _dot(tri_b, a3))
        acs_t = acs.T
        last = q - 1 if fwd else 0
        tot_row = acs[last:last + 1, :]
        e2 = e_ref[d]
        dt_w = expand(dt, e2)
        eacs_w = expand(jnp.exp(acs), e2)
        dec_w = expand(jnp.exp(tot_row - acs), e2)
        sdec = _dot_sel_r(jnp.broadcast_to(jnp.exp(tot_row), (SUBLANE, LANE)), e2[0:LANE])[0:1, :]
        xd = xs_ref[rows, :].astype(F32) * dt_w
        xdb = xd.astype(BF16)
        xdd = (xd * dec_w).astype(BF16)
        bc = bc_ref[rows, :]
        for g in range(SSD_GROUPS):
            bg = bc[:, g * D_STATE:(g + 1) * D_STATE]
            cg = bc[:, (SSD_GROUPS + g) * D_STATE:(SSD_GROUPS + g + 1) * D_STATE]
            cb = _dot_nt(cg, bg)
            lanes = slice(g * gw, (g + 1) * gw)
            s_g = state_ref[d, :, lanes]
            y = _dot(cg, s_g.astype(BF16)) * eacs_w[:, lanes]
            xg = xdb[:, lanes]
            zero = jnp.zeros_like(xg)
            for r in range(0, hpg, 2):
                mats, xms = [], []
                for rr in (r, r + 1):
                    jl = d * SSD_HEADS + g * hpg + rr
                    seg = jnp.where(tri, acs[:, jl:jl + 1] - acs_t[jl:jl + 1, :], -jnp.inf)
                    mats.append((cb * jnp.exp(seg)).astype(BF16))
                    xms.append(jnp.where(head_of_lane == rr, xg, zero))
                y = y + _dot(jnp.concatenate(mats, axis=1), jnp.concatenate(xms, axis=0))
            y_ref[rows, lanes] = y.astype(y_ref.dtype)
            bg_t = bg.astype(F32).T.astype(BF16)
            state_ref[d, :, lanes] = s_g * sdec[:, lanes] + _dot(bg_t, xdd[:, lanes])

    for i in range(cps):
        one_chunk(0, pl.ds(i * q, q), xsf_ref, bcf_ref, dtf_ref, yf_ref)
        one_chunk(1, pl.ds((cps - 1 - i) * q, q), xsb_ref, bcb_ref, dtb_ref, yb_ref)

    @pl.when(c == pl.num_programs(1) - 1)
    def _():
        fin_ref[0] = state_ref[...]


def _ssd(act, dt, alog, dtbias, emat2, init, batch, n, cps):
    nc = n // (CHUNK * cps)
    tr = CHUNK * cps
    fidx = lambda b, c: b * nc + c
    bidx = lambda b, c: b * nc + (nc - 1 - c)
    st_shape = (2, D_STATE, D_INNER)
    return pl.pallas_call(
        _ssd_kernel,
        grid=(batch, nc),
        in_specs=[
            pl.BlockSpec((tr, CB), lambda b, c: (fidx(b, c), 0)),
            pl.BlockSpec((tr, CB), lambda b, c: (fidx(b, c), 1)),
            pl.BlockSpec((tr, LANE), lambda b, c: (fidx(b, c), 0)),
            pl.BlockSpec((tr, CB), lambda b, c: (bidx(b, c), 0)),
            pl.BlockSpec((tr, CB), lambda b, c: (bidx(b, c), 1)),
            pl.BlockSpec((tr, LANE), lambda b, c: (bidx(b, c), 0)),
            pl.BlockSpec((1, LANE), lambda b, c: (0, 0)),
            pl.BlockSpec((1, LANE), lambda b, c: (0, 0)),
            pl.BlockSpec((2, 2 * LANE, D_INNER), lambda b, c: (0, 0, 0)),
            pl.BlockSpec((1,) + st_shape, lambda b, c: (b, 0, 0, 0)),
        ],
        out_specs=[
            pl.BlockSpec((tr, D_INNER), lambda b, c: (fidx(b, c), 0)),
            pl.BlockSpec((tr, D_INNER), lambda b, c: (bidx(b, c), 0)),
            pl.BlockSpec((1,) + st_shape, lambda b, c: (b, 0, 0, 0)),
        ],
        out_shape=[
            jax.ShapeDtypeStruct((batch * n, D_INNER), BF16),
            jax.ShapeDtypeStruct((batch * n, D_INNER), BF16),
            jax.ShapeDtypeStruct((batch,) + st_shape, F32),
        ],
        scratch_shapes=[pltpu.VMEM(st_shape, F32)],
        compiler_params=_params("parallel", "arbitrary"),
        name="ssd_scan",
    )(act, act, dt, act, act, dt, alog, dtbias, emat2, init)


def _merge_kernel(four_ref, att_ref, yf_ref, yb_ref, xs_ref, fg_ref, ag_ref, z_ref,
                  g0_ref, g1_ref, g2_ref, x_ref, mod_ref, dskip_ref, snw_ref,
                  wof_ref, woa_ref, wos_ref, wout_ref, nf_ref, o_ref, *, final_norm):
    f = lambda ref: ref[...].astype(F32)
    y_f = _dot((f(four_ref) * f(fg_ref)).astype(BF16), wof_ref[...])
    y_a = _dot((f(att_ref) * f(ag_ref)).astype(BF16), woa_ref[...])
    ys = f(yf_ref) + f(yb_ref) + dskip_ref[...] * f(xs_ref)
    t = ys * f(z_ref)
    t = t * lax.rsqrt(jnp.mean(t * t, axis=-1, keepdims=True) + EPS) * snw_ref[...]
    y_s = _dot(t.astype(BF16), wos_ref[...])
    y = f(g0_ref) * y_f + f(g1_ref) * y_a + f(g2_ref) * y_s
    out = _dot(y.astype(BF16), wout_ref[...])
    xn = x_ref[...] + mod_ref[0][:, 2 * D_MODEL:3 * D_MODEL] * out
    if final_norm:
        xn = xn * lax.rsqrt(jnp.mean(xn * xn, axis=-1, keepdims=True) + EPS) * nf_ref[...]
    o_ref[...] = xn


def _merge(four, att, yf, yb, p, x2, mod, dskip_w, snw, wof, woa, wos, wout, norm_f,
           n, tm, final_norm):
    rows = x2.shape[0]
    tiles_per_b = n // tm
    gcol = COL_GATES * CB // D_MODEL
    row_blk = lambda w, cidx: pl.BlockSpec((tm, w), lambda i: (i, cidx))
    const = lambda shape: pl.BlockSpec(shape, lambda i: (0,) * len(shape))
    return pl.pallas_call(
        functools.partial(_merge_kernel, final_norm=final_norm),
        grid=(rows // tm,),
        in_specs=[
            row_blk(F_W, 0), row_blk(CB, 0), row_blk(D_INNER, 0), row_blk(D_INNER, 0),
            row_blk(CB, 0),
            row_blk(CB, COL_FG), row_blk(CB, COL_AG), row_blk(CB, COL_Z),
            row_blk(D_MODEL, gcol), row_blk(D_MODEL, gcol + 1), row_blk(D_MODEL, gcol + 2),
            row_blk(D_MODEL, 0),
            pl.BlockSpec((1, 1, 3 * D_MODEL), lambda i: (i // tiles_per_b, 0, 0)),
            const((1, D_INNER)), const((1, D_INNER)),
            const((F_W, D_MODEL)), const((CB, D_MODEL)), const((D_INNER, D_MODEL)),
            const((D_MODEL, D_MODEL)), const((1, D_MODEL)),
        ],
        out_specs=pl.BlockSpec((tm, D_MODEL), lambda i: (i, 0)),
        out_shape=jax.ShapeDtypeStruct((rows, D_MODEL), F32),
        compiler_params=_params("parallel"),
        name="branch_merge",
    )(four, att, yf, yb, p, p, p, p, p, p, p, x2, mod, dskip_w, snw, wof, woa, wos, wout, norm_f)


def _rope_tables(n):
    rows = n // GRID_W
    row = jnp.repeat(jnp.arange(rows, dtype=F32), GRID_W)
    col = jnp.tile(jnp.arange(GRID_W, dtype=F32), rows)
    freqs = ROPE_BASE ** (-jnp.arange(0, ROPE_AXIS, 2, dtype=F32) / ROPE_AXIS)
    ang_r = row[:, None] * freqs
    ang_c = col[:, None] * freqs
    ang = jnp.concatenate([ang_r, ang_r, ang_c, ang_c], axis=-1)
    reps = CB // ATT_QK
    return jnp.tile(jnp.cos(ang), (1, reps)), jnp.tile(jnp.sin(ang), (1, reps))


def _head_expanders():
    e = np.zeros((2, LANE, D_INNER), np.float32)
    for d in range(2):
        for h in range(SSD_HEADS):
            e[d, d * SSD_HEADS + h, h * SSD_P:(h + 1) * SSD_P] = 1.0
    return jnp.asarray(np.concatenate([e, e], axis=1), BF16)


def _group_dft(w):
    cw, sw = _dft_tables(F_GROUP_W)
    eye = jnp.eye(w // F_GROUP_W, dtype=F32)
    return jnp.kron(eye, cw).astype(BF16), jnp.kron(eye, sw).astype(BF16)


def kernel(x, c, ctx, c_ctx, w_mod, b_mod, norm_w, w_in, conv_w, conv_b, a_log, dt_bias, d_skip,
           ssd_norm_w, lam, subln_w, w_of, w_oa, w_os, w_out, norm_f):
    batch, n, _ = x.shape
    n_ctx = ctx.shape[1]
    depth = w_mod.shape[0]
    assert n % GRID_W == 0 and n % CHUNK == 0 and n_ctx % CHUNK == 0

    o_q = 2 * F_W
    o_ag = o_q + 3 * CB
    o_xbc = o_ag + CB + D_INNER
    o_dt = o_xbc + XBC_W
    o_gt = o_dt + DT_W
    w_main = jnp.concatenate([w_in[:, :, o_xbc:o_dt], w_in[:, :, o_gt:], w_in[:, :, :o_q],
                              w_in[:, :, o_ag:o_xbc], w_in[:, :, o_q:o_ag]], axis=-1).astype(BF16)
    w_dt = jnp.pad(w_in[:, :, o_dt:o_gt], ((0, 0), (0, 0), (0, LANE - DT_W))).astype(BF16)
    conv_w8 = jnp.pad(conv_w, ((0, 0), (0, SUBLANE - CONV_W), (0, 0)))
    pad_lanes = lambda a: jnp.pad(a.reshape(depth, 1, DT_W), ((0, 0), (0, 0), (0, LANE - DT_W)))
    alog_p, dtbias_p = pad_lanes(a_log), pad_lanes(dt_bias)
    dskip_w = jnp.repeat(d_skip, SSD_P, axis=-1).reshape(depth, 1, D_INNER)
    wof_b, woa_b, wos_b, wout_b = (w.astype(BF16) for w in (w_of, w_oa, w_os, w_out))
    lam_inits = [0.8 - 0.6 * math.exp(-0.3 * l) for l in range(depth)]
    linit = jnp.asarray(np.broadcast_to(np.asarray(lam_inits, np.float32)[:, None, None],
                                        (depth, SUBLANE, LANE)))

    cos_t, sin_t = _rope_tables(n)
    tf_l = min(n, 512)
    cs_l = _seq_dft_table(n, tf_l)
    cs_c = _seq_dft_table(n_ctx, n_ctx)
    ccd, scd = _group_dft(F_W)
    emat = _head_expanders()
    zero_state = jnp.zeros((batch, 2, D_STATE, D_INNER), F32)

    mod_rows = -(-(batch + 1) // SUBLANE) * SUBLANE
    cc = jnp.concatenate([c, c_ctx[None, :], jnp.zeros((mod_rows - batch - 1, D_MODEL), F32)], axis=0)
    mod_all, lam_all = _modulation(cc, w_mod, b_mod.reshape(depth, 1, 3 * D_MODEL), lam, linit)

    tm_l = min(n, 512)
    tq_l = min(n, 1024)
    tk_l = min(n, 1024)
    tmerge_l = min(n, 512)
    rows_c = batch * n_ctx
    tm_c = math.gcd(n_ctx, 512)
    cps_l = math.gcd(n // CHUNK, 8)
    cps_c = math.gcd(n_ctx // CHUNK, 2)
    xl = x.reshape(batch * n, D_MODEL)
    xc = ctx.reshape(batch * n_ctx, D_MODEL)
    for l in range(depth):
        last = l == depth - 1
        mod_l = mod_all[l, :batch].reshape(batch, 1, 3 * D_MODEL)
        mod_c = mod_all[l, batch].reshape(1, 1, 3 * D_MODEL)
        nw = norm_w[l].reshape(1, D_MODEL)
        cb_ = conv_b[l].reshape(1, XBC_W)
        p_c, dt_c = _inproj(xc, mod_c, nw, w_main[l], w_dt[l], conv_w8[l], cb_, None, None,
                            rows_c, n_ctx, tm_c)
        p_l, dt_l = _inproj(xl, mod_l, nw, w_main[l], w_dt[l], conv_w8[l], cb_, cos_t, sin_t,
                            n, n, tm_l)
        yf_c, yb_c, st_c = _ssd(p_c, dt_c, alog_p[l], dtbias_p[l], emat, zero_state, batch, n_ctx,
                                cps_c)
        yf_l, yb_l, _ = _ssd(p_l, dt_l, alog_p[l], dtbias_p[l], emat, st_c, batch, n, cps_l)
        sw = subln_w[l].reshape(1, ATT_V)
        att_l = _attention(p_l, p_c, p_l, lam_all[l], sw, lam_inits[l], batch, n, n_ctx, n,
                           tq_l, tk_l)
        four_l = _fourier(p_l, cs_l, ccd, scd, batch, n, tf_l)
        merge_w = (dskip_w[l], ssd_norm_w[l].reshape(1, D_INNER), wof_b[l], woa_b[l], wos_b[l],
                   wout_b[l], norm_f.reshape(1, D_MODEL))
        if not last:
            att_c = _attention(p_c, p_c, None, lam_all[l], sw, lam_inits[l], batch, n_ctx, n_ctx, 0,
                               n_ctx, 0)
            four_c = _fourier(p_c, cs_c, ccd, scd, batch, n_ctx, n_ctx)
            xc = _merge(four_c, att_c, yf_c, yb_c, p_c, xc, mod_c, *merge_w, rows_c, n_ctx, False)
        xl = _merge(four_l, att_l, yf_l, yb_l, p_l, xl, mod_l, *merge_w, n, tmerge_l, last)
    return xl.reshape(batch, n, D_MODEL)
```

```python
import functools
import math

import numpy as np
import jax
import jax.numpy as jnp
from jax import lax
from jax.experimental import pallas as pl
from jax.experimental.pallas import tpu as pltpu

F32 = jnp.float32
BF16 = jnp.bfloat16
EPS = 1e-6

D_MODEL = 1024
GRID_W = 64
F_GROUP_W = 128
F_W = 512
ATT_HEADS = 4
ATT_QK = 64
ATT_V = 128
ROPE_AXIS = 32
ROPE_BASE = 10000.0
D_INNER = 512
SSD_P = 64
SSD_HEADS = 8
SSD_GROUPS = 2
D_STATE = 128
CONV_W = 5
CHUNK = 128
XBC_W = 1024
DT_W = 16
MERGE_W = 3072

LANE = 128
SUBLANE = 8
VMEM_LIMIT = 56 * 1024 * 1024

SUBLANE_BF16 = 16
CB = 512
COL_XBC, COL_GATES, COL_FU, COL_FG, COL_AG, COL_Z = 0, 2, 8, 9, 10, 11
COL_Q, COL_K, COL_V = 12, 13, 14
N_COLB = 15
N_MAIN = N_COLB * CB
PROJ_STEP_B = 3
PROJ_STEPS = N_COLB // PROJ_STEP_B
assert COL_Q == (PROJ_STEPS - 1) * PROJ_STEP_B and COL_V == N_COLB - 1
Q_SCALE = ATT_QK ** -0.5 * math.log2(math.e)


def _dot(a, b):
    return jnp.dot(a, b, preferred_element_type=F32)


def _dot_nt(a, b):
    return lax.dot_general(a, b, (((1,), (1,)), ((), ())), preferred_element_type=F32)


def _split3(x):
    x1 = x.astype(BF16)
    r1 = x - x1.astype(F32)
    x2 = r1.astype(BF16)
    x3 = (r1 - x2.astype(F32)).astype(BF16)
    return x1, x2, x3


def _dot_sel_r(x, sel):
    x1, x2, x3 = _split3(x)
    return _dot(x1, sel) + _dot(x2, sel) + _dot(x3, sel)


def _dot_sel_l(sel, x):
    x1, x2, x3 = _split3(x)
    return _dot(sel, x1) + _dot(sel, x2) + _dot(sel, x3)


def _dot_sel_r2(x, sel):
    x1 = x.astype(BF16)
    x2 = (x - x1.astype(F32)).astype(BF16)
    return _dot(x1, sel) + _dot(x2, sel)


def _sigmoid(x):
    return 0.5 * jnp.tanh(0.5 * x) + 0.5


def _silu(x):
    return x * _sigmoid(x)


def _params(*sem, flags=None):
    return pltpu.CompilerParams(dimension_semantics=sem, vmem_limit_bytes=VMEM_LIMIT, flags=flags)


def _mod_kernel(cc_ref, w_ref, b_ref, lam_ref, linit_ref, mod_ref, lam_out_ref):
    s = _silu(cc_ref[...])
    mod_ref[0] = jnp.dot(s, w_ref[0], precision=lax.Precision.HIGHEST,
                         preferred_element_type=F32) + b_ref[0]
    lp = lam_ref[0]
    s1 = jnp.sum(lp[0:1] * lp[1:2], axis=-1, keepdims=True)
    s2 = jnp.sum(lp[2:3] * lp[3:4], axis=-1, keepdims=True)
    lam_out_ref[0] = jnp.broadcast_to(jnp.exp(s1) - jnp.exp(s2), (SUBLANE, LANE)) + linit_ref[0]


def _modulation(cc, w_mod, b_mod, lam, linit):
    depth = w_mod.shape[0]
    rows = cc.shape[0]
    tn = D_MODEL
    return pl.pallas_call(
        _mod_kernel,
        grid=(depth, 3 * D_MODEL // tn),
        in_specs=[
            pl.BlockSpec((rows, D_MODEL), lambda l, j: (0, 0)),
            pl.BlockSpec((1, D_MODEL, tn), lambda l, j: (l, 0, j)),
            pl.BlockSpec((1, 1, tn), lambda l, j: (l, 0, j)),
            pl.BlockSpec((1, 4, ATT_QK), lambda l, j: (l, 0, 0)),
            pl.BlockSpec((1, SUBLANE, LANE), lambda l, j: (l, 0, 0)),
        ],
        out_specs=[
            pl.BlockSpec((1, rows, tn), lambda l, j: (l, 0, j)),
            pl.BlockSpec((1, SUBLANE, LANE), lambda l, j: (l, 0, 0)),
        ],
        out_shape=[
            jax.ShapeDtypeStruct((depth, rows, 3 * D_MODEL), F32),
            jax.ShapeDtypeStruct((depth, SUBLANE, LANE), F32),
        ],
        compiler_params=_params("arbitrary", "arbitrary"),
        name="adaln_mod",
    )(cc, w_mod, b_mod, lam, linit)


def _rope(t, cos, sin):
    w = t.shape[-1]
    lane = lax.broadcasted_iota(jnp.int32, t.shape, 1)
    first = (lane % ROPE_AXIS) < (ROPE_AXIS // 2)
    rot = jnp.where(first, -pltpu.roll(t, w - ROPE_AXIS // 2, 1), pltpu.roll(t, ROPE_AXIS // 2, 1))
    return t * cos + rot * sin


HALO = SUBLANE


def _inproj_kernel(*refs, rope, seq_tiles):
    if rope:
        (x_ref, xp_ref, xn_ref, mod_ref, nw_ref, w_ref, wdt_ref, cw_ref, cb_ref, cos_ref, sin_ref,
         p_ref, dt_ref, ext_ref) = refs
    else:
        (x_ref, xp_ref, xn_ref, mod_ref, nw_ref, w_ref, wdt_ref, cw_ref, cb_ref,
         p_ref, dt_ref, ext_ref) = refs
    tm = x_ref.shape[0]
    t = pl.program_id(0) % seq_tiles
    m = mod_ref[0]

    def norm_mod(x):
        y = x * lax.rsqrt(jnp.mean(x * x, axis=-1, keepdims=True) + EPS) * nw_ref[...]
        return (y * (1.0 + m[:, D_MODEL:2 * D_MODEL]) + m[:, 0:D_MODEL]).astype(BF16)

    hb = norm_mod(x_ref[...])
    dt_ref[...] = _dot(hb, wdt_ref[...])
    halo = _dot(norm_mod(jnp.concatenate([xp_ref[...], xn_ref[...]], axis=0)), w_ref[:, 0:XBC_W])
    ext_ref[0:HALO, :] = jnp.where(t == 0, 0.0, halo[0:HALO])
    ext_ref[HALO + tm:2 * HALO + tm, :] = jnp.where(t == seq_tiles - 1, 0.0, halo[HALO:2 * HALO])
    ws = PROJ_STEP_B * CB
    for j in range(PROJ_STEPS):
        acc = _dot(hb, w_ref[:, j * ws:(j + 1) * ws])
        if j < PROJ_STEPS - 1:
            for b in range(PROJ_STEP_B):
                blk, cols = j * PROJ_STEP_B + b, slice(b * CB, (b + 1) * CB)
                if blk * CB < XBC_W:
                    ext_ref[HALO:HALO + tm, blk * CB:(blk + 1) * CB] = acc[:, cols]
                    continue
                act = (_sigmoid if COL_GATES <= blk < COL_FU else
                       _silu if blk in (COL_FG, COL_AG, COL_Z) else (lambda t: t))
                p_ref[:, blk * CB:(blk + 1) * CB] = act(acc[:, cols]).astype(BF16)
        else:
            q, k, v = (acc[:, b * CB:(b + 1) * CB] for b in range(PROJ_STEP_B))
            if rope:
                cos, sin = cos_ref[...], sin_ref[...]
                q, k = _rope(q, cos, sin), _rope(k, cos, sin)
            p_ref[:, j * ws:j * ws + CB] = (q * Q_SCALE).astype(BF16)
            p_ref[:, j * ws + CB:j * ws + 2 * CB] = k.astype(BF16)
            p_ref[:, j * ws + 2 * CB:(j + 1) * ws] = v.astype(BF16)
    for cols in (slice(c0, c0 + CB) for c0 in range(0, XBC_W, CB)):
        conv = jnp.broadcast_to(cb_ref[:, cols], (tm, CB))
        for k in range(CONV_W):
            conv = conv + ext_ref[pl.ds(HALO - CONV_W // 2 + k, tm), cols] * cw_ref[k:k + 1, cols]
        p_ref[:, cols] = _silu(conv).astype(BF16)


def _inproj(x2, mod, norm_w, w_main, w_dt, conv_w8, conv_b, cos, sin, n_mod, n_seq, tm):
    rows = x2.shape[0]
    mod_tiles, seq_tiles = n_mod // tm, n_seq // tm
    hb = tm // HALO
    last_hb = rows // HALO - 1
    rope = cos is not None
    resident = lambda shape: pl.BlockSpec(shape, lambda i: (0, 0), pipeline_mode=pl.Buffered(1))
    in_specs = [
        pl.BlockSpec((tm, D_MODEL), lambda i: (i, 0)),
        pl.BlockSpec((HALO, D_MODEL), lambda i: (jnp.maximum(i * hb - 1, 0), 0)),
        pl.BlockSpec((HALO, D_MODEL), lambda i: (jnp.minimum((i + 1) * hb, last_hb), 0)),
        pl.BlockSpec((1, 1, 3 * D_MODEL), lambda i: (i // mod_tiles, 0, 0)),
        pl.BlockSpec((1, D_MODEL), lambda i: (0, 0)),
        resident((D_MODEL, N_MAIN)),
        resident((D_MODEL, LANE)),
        pl.BlockSpec((SUBLANE, XBC_W), lambda i: (0, 0)),
        pl.BlockSpec((1, XBC_W), lambda i: (0, 0)),
    ]
    args = [x2, x2, x2, mod, norm_w, w_main, w_dt, conv_w8, conv_b]
    if rope:
        in_specs += [pl.BlockSpec((tm, CB), lambda i: (i % seq_tiles, 0))] * 2
        args += [cos, sin]
    return pl.pallas_call(
        functools.partial(_inproj_kernel, rope=rope, seq_tiles=seq_tiles),
        grid=(rows // tm,),
        in_specs=in_specs,
        out_specs=[
            pl.BlockSpec((tm, N_MAIN), lambda i: (i, 0)),
            pl.BlockSpec((tm, LANE), lambda i: (i, 0)),
        ],
        out_shape=[
            jax.ShapeDtypeStruct((rows, N_MAIN), BF16),
            jax.ShapeDtypeStruct((rows, LANE), F32),
        ],
        scratch_shapes=[pltpu.VMEM((tm + 2 * HALO, XBC_W), F32)],
        compiler_params=_params("parallel"),
        name="inproj",
    )(*args)


def _attn_kernel(*refs, with_latent, one_minus_lam_init, n_ctx, chunks):
    if with_latent:
        (q_ref, kc_ref, vc_ref, kl_ref, vl_ref, lam_ref, sw_ref, o_ref,
         k_scr, v_scr, m_scr, acc_scr) = refs
    else:
        q_ref, kc_ref, vc_ref, lam_ref, sw_ref, o_ref, k_scr, v_scr, m_scr, acc_scr = refs
    tq = q_ref.shape[0]

    @pl.when(pl.program_id(2) == 0)
    def _():
        def put(k_ref, v_ref, off):
            rows = k_ref.shape[0]
            k = k_ref[...]
            lane = lax.broadcasted_iota(jnp.int32, k.shape, 1)
            zero = jnp.zeros_like(k)
            k_scr[0, off:off + rows, :] = jnp.where(lane < ATT_QK, k, zero)
            k_scr[1, off:off + rows, :] = jnp.where(lane >= ATT_QK, k, zero)
            v_scr[off:off + rows, 0:ATT_V] = v_ref[...]
            v_scr[off:off + rows, ATT_V:2 * ATT_V] = jnp.ones((rows, ATT_V), BF16)
        put(kc_ref, vc_ref, 0)
        if with_latent:
            put(kl_ref, vl_ref, n_ctx)

    m_scr[...] = jnp.full(m_scr.shape, -jnp.inf, F32)
    acc_scr[...] = jnp.zeros(acc_scr.shape, F32)
    q = q_ref[...]

    def chunk(off, size):
        v = v_scr[pl.ds(off, size), :]
        nt = size // LANE
        for mi in range(2):
            s = _dot_nt(q, k_scr[mi, pl.ds(off, size), :])
            tiles = [s[:, t * LANE:(t + 1) * LANE] for t in range(nt)]
            mx = functools.reduce(jnp.maximum, tiles)
            m_prev = m_scr[mi]
            m_new = jnp.maximum(m_prev, jnp.max(mx, axis=-1, keepdims=True))
            p = jnp.concatenate([jnp.exp2(t - m_new) for t in tiles], axis=1).astype(BF16)
            alpha = jnp.exp2(m_prev - m_new)
            acc_scr[mi] = jnp.concatenate([alpha, alpha], axis=1) * acc_scr[mi] + _dot(p, v)
            m_scr[mi] = m_new

    for off, size in chunks:
        chunk(off, size)

    lam = lam_ref[0:1, :]
    a0, a1 = acc_scr[0], acc_scr[1]
    o = a0[:, 0:ATT_V] / a0[:, ATT_V:] - lam * (a1[:, 0:ATT_V] / a1[:, ATT_V:])
    y = o * lax.rsqrt(jnp.mean(o * o, axis=-1, keepdims=True) + EPS) * sw_ref[...]
    o_ref[...] = (y * one_minus_lam_init).astype(o_ref.dtype)


def _attention(qkv_q, qkv_c, qkv_l, lam_l, subln_w, lam_init, batch, nq_len, nc_len, nl_len, tq, tk):
    with_latent = qkv_l is not None
    nq = nq_len // tq
    hq, hk, hv = (c * (CB // LANE) for c in (COL_Q, COL_K, COL_V))
    in_specs = [
        pl.BlockSpec((tq, LANE), lambda b, h, i: (b * nq + i, hq + h)),
        pl.BlockSpec((nc_len, LANE), lambda b, h, i: (b, hk + h)),
        pl.BlockSpec((nc_len, LANE), lambda b, h, i: (b, hv + h)),
    ]
    args = [qkv_q, qkv_c, qkv_c]
    if with_latent:
        in_specs += [
            pl.BlockSpec((nl_len, LANE), lambda b, h, i: (b, hk + h)),
            pl.BlockSpec((nl_len, LANE), lambda b, h, i: (b, hv + h)),
        ]
        args += [qkv_l, qkv_l]
    in_specs += [
        pl.BlockSpec((SUBLANE, LANE), lambda b, h, i: (0, 0)),
        pl.BlockSpec((1, ATT_V), lambda b, h, i: (0, 0)),
    ]
    args += [lam_l, subln_w]
    n_keys = nc_len + nl_len
    sizes = [nc_len + tk] + [tk] * (nl_len // tk - 1) if with_latent else [nc_len]
    chunks = tuple((sum(sizes[:t]), sizes[t]) for t in range(len(sizes)))
    assert sum(sizes) == n_keys
    return pl.pallas_call(
        functools.partial(_attn_kernel, with_latent=with_latent,
                          one_minus_lam_init=1.0 - lam_init, n_ctx=nc_len, chunks=chunks),
        grid=(batch, ATT_HEADS, nq),
        in_specs=in_specs,
        out_specs=pl.BlockSpec((tq, ATT_V), lambda b, h, i: (b * nq + i, h)),
        out_shape=jax.ShapeDtypeStruct((batch * nq_len, ATT_HEADS * ATT_V), BF16),
        scratch_shapes=[
            pltpu.VMEM((2, n_keys, LANE), BF16),
            pltpu.VMEM((n_keys, 2 * ATT_V), BF16),
            pltpu.VMEM((2, tq, LANE), F32),
            pltpu.VMEM((2, tq, 2 * ATT_V), F32),
        ],
        compiler_params=_params("parallel", "parallel", "arbitrary"),
        name="diff_attention",
    )(*args)


def _fourier_kernel(x_ref, xr_ref, xm_ref, cs_ref, cc_ref, sc_ref, o_ref, acc_ref, *, scale):
    k = pl.program_id(1)

    @pl.when(k == 0)
    def _():
        mid = _dot(xm_ref[...], cc_ref[0:F_W, :])[0:1, :]
        row = lax.broadcasted_iota(jnp.int32, acc_ref.shape, 0)
        acc_ref[...] = jnp.where(row % 2 == 0, mid, -mid)

    xx = jnp.concatenate([x_ref[...], xr_ref[...]], axis=1)
    ue = _dot(xx, cc_ref[...]).astype(BF16)
    wo = _dot(xx, sc_ref[...]).astype(BF16)
    acc_ref[...] += _dot(cs_ref[...], jnp.concatenate([ue, wo], axis=0))

    @pl.when(k == pl.num_programs(1) - 1)
    def _():
        o_ref[...] = (acc_ref[...] * scale).astype(o_ref.dtype)


DFT_SPLIT = 64


def _dft_tables(n, cols=None):
    cols = jnp.arange(n, dtype=jnp.int32) if cols is None else cols
    m = cols.shape[0]

    def cos_sin(rows):
        ang = ((rows[:, None] * cols[None, :]) % n).astype(F32) * (2.0 * math.pi / n)
        return jnp.cos(ang), jnp.sin(ang)

    if n <= DFT_SPLIT * DFT_SPLIT // 4 or n % DFT_SPLIT:
        return cos_sin(jnp.arange(n, dtype=jnp.int32))
    ca, sa = cos_sin(jnp.arange(n // DFT_SPLIT, dtype=jnp.int32) * DFT_SPLIT)
    cb, sb = cos_sin(jnp.arange(DFT_SPLIT, dtype=jnp.int32))
    ca, sa, cb, sb = ca[:, None, :], sa[:, None, :], cb[None, :, :], sb[None, :, :]
    return (ca * cb - sa * sb).reshape(n, m), (sa * cb + ca * sb).reshape(n, m)


def _seq_dft_table(n, tk):
    col = jnp.arange(n, dtype=jnp.int32)
    k = (col // (2 * tk)) * tk + col % tk
    cn, sn = _dft_tables(n, k)
    return jnp.where(((col // tk) % 2 == 0)[None, :], cn, -sn).astype(BF16)


def _fourier(p, cs, ccd2, scd2, batch, n, tk):
    half = n // 2
    nk, nkh = n // tk, half // tk
    scale = 1.0 / math.sqrt(n * F_GROUP_W)
    fu = p.reshape(batch, n, N_MAIN)[:, half + 1:, COL_FU * CB:(COL_FU + 1) * CB]
    xr = jnp.concatenate([jnp.zeros((batch, 1, CB), BF16), jnp.flip(fu, axis=1)], axis=1)
    xr = xr.reshape(batch * half, CB)
    mid_blk = half // SUBLANE_BF16
    return pl.pallas_call(
        functools.partial(_fourier_kernel, scale=scale),
        grid=(batch, nkh),
        in_specs=[
            pl.BlockSpec((tk, CB), lambda b, k: (b * nk + k, COL_FU)),
            pl.BlockSpec((tk, CB), lambda b, k: (b * nkh + k, 0)),
            pl.BlockSpec((SUBLANE_BF16, CB), lambda b, k: (b * 2 * mid_blk + mid_blk, COL_FU)),
            pl.BlockSpec((n, 2 * tk), lambda b, k: (0, k)),
            pl.BlockSpec((2 * F_W, F_W), lambda b, k: (0, 0)),
            pl.BlockSpec((2 * F_W, F_W), lambda b, k: (0, 0)),
        ],
        out_specs=pl.BlockSpec((n, F_W), lambda b, k: (b, 0)),
        out_shape=jax.ShapeDtypeStruct((batch * n, F_W), BF16),
        scratch_shapes=[pltpu.VMEM((n, F_W), F32)],
        compiler_params=_params("parallel", "arbitrary"),
        name="fourier_mix",
    )(p, xr, p, cs, ccd2, scd2)


def _softplus(x):
    return jnp.maximum(x, 0.0) + jnp.log1p(jnp.exp(-jnp.abs(x)))


def _ssd_kernel(xsf_ref, bcf_ref, dtf_ref, xsb_ref, bcb_ref, dtb_ref, alog_ref, dtbias_ref,
                e_ref, init_ref, yf_ref, yb_ref, fin_ref, state_ref):
    c = pl.program_id(1)
    q = CHUNK
    cps = xsf_ref.shape[0] // q
    gw = D_INNER // SSD_GROUPS
    hpg = SSD_HEADS // SSD_GROUPS

    @pl.when(c == 0)
    def _():
        state_ref[...] = init_ref[0]

    row = lax.broadcasted_iota(jnp.int32, (q, q), 0)
    col = lax.broadcasted_iota(jnp.int32, (q, q), 1)
    head_of_lane = lax.broadcasted_iota(jnp.int32, (q, gw), 1) // SSD_P
    neg_a = -jnp.exp(alog_ref[...])
    dtbias = dtbias_ref[...]

    def expand(x, e2):
        x1 = x.astype(BF16)
        x2 = (x - x1.astype(F32)).astype(BF16)
        return _dot(jnp.concatenate([x1, x2], axis=1), e2)

    def one_chunk(d, rows, xs_ref, bc_ref, dt_ref, y_ref):
        fwd = d == 0
        tri = (row >= col) if fwd else (row <= col)
        tri_b = tri.astype(F32).astype(BF16)
        dt = _softplus(dt_ref[rows, :] + dtbias)
        a1, a2, a3 = _split3(dt * neg_a)
        acs = (_dot(jnp.concatenate([tri_b, tri_b], axis=1), jnp.concatenate([a1, a2], axis=0))
               + _dot(tri_b, a3))
        acs_t = acs.T
        last = q - 1 if fwd else 0
        tot_row = acs[last:last + 1, :]
        e2 = e_ref[d]
        dt_w = expand(dt, e2)
        eacs_w = expand(jnp.exp(acs), e2)
        dec_w = expand(jnp.exp(tot_row - acs), e2)
        sdec = _dot_sel_r(jnp.broadcast_to(jnp.exp(tot_row), (SUBLANE, LANE)), e2[0:LANE])[0:1, :]
        xd = xs_ref[rows, :].astype(F32) * dt_w
        xdb = xd.astype(BF16)
        xdd = (xd * dec_w).astype(BF16)
        bc = bc_ref[rows, :]
        for g in range(SSD_GROUPS):
            bg = bc[:, g * D_STATE:(g + 1) * D_STATE]
            cg = bc[:, (SSD_GROUPS + g) * D_STATE:(SSD_GROUPS + g + 1) * D_STATE]
            cb = _dot_nt(cg, bg)
            lanes = slice(g * gw, (g + 1) * gw)
            s_g = state_ref[d, :, lanes]
            y = _dot(cg, s_g.astype(BF16)) * eacs_w[:, lanes]
            xg = xdb[:, lanes]
            zero = jnp.zeros_like(xg)
            for r in range(0, hpg, 2):
                mats, xms = [], []
                for rr in (r, r + 1):
                    jl = d * SSD_HEADS + g * hpg + rr
                    seg = jnp.where(tri, acs[:, jl:jl + 1] - acs_t[jl:jl + 1, :], -jnp.inf)
                    mats.append((cb * jnp.exp(seg)).astype(BF16))
                    xms.append(jnp.where(head_of_lane == rr, xg, zero))
                y = y + _dot(jnp.concatenate(mats, axis=1), jnp.concatenate(xms, axis=0))
            y_ref[rows, lanes] = y.astype(y_ref.dtype)
            bg_t = bg.astype(F32).T.astype(BF16)
            state_ref[d, :, lanes] = s_g * sdec[:, lanes] + _dot(bg_t, xdd[:, lanes])

    for i in range(cps):
        one_chunk(0, pl.ds(i * q, q), xsf_ref, bcf_ref, dtf_ref, yf_ref)
        one_chunk(1, pl.ds((cps - 1 - i) * q, q), xsb_ref, bcb_ref, dtb_ref, yb_ref)

    @pl.when(c == pl.num_programs(1) - 1)
    def _():
        fin_ref[0] = state_ref[...]


def _ssd(act, dt, alog, dtbias, emat2, init, batch, n, cps):
    nc = n // (CHUNK * cps)
    tr = CHUNK * cps
    fidx = lambda b, c: b * nc + c
    bidx = lambda b, c: b * nc + (nc - 1 - c)
    st_shape = (2, D_STATE, D_INNER)
    return pl.pallas_call(
        _ssd_kernel,
        grid=(batch, nc),
        in_specs=[
            pl.BlockSpec((tr, CB), lambda b, c: (fidx(b, c), 0)),
            pl.BlockSpec((tr, CB), lambda b, c: (fidx(b, c), 1)),
            pl.BlockSpec((tr, LANE), lambda b, c: (fidx(b, c), 0)),
            pl.BlockSpec((tr, CB), lambda b, c: (bidx(b, c), 0)),
            pl.BlockSpec((tr, CB), lambda b, c: (bidx(b, c), 1)),
            pl.BlockSpec((tr, LANE), lambda b, c: (bidx(b, c), 0)),
            pl.BlockSpec((1, LANE), lambda b, c: (0, 0)),
            pl.BlockSpec((1, LANE), lambda b, c: (0, 0)),
            pl.BlockSpec((2, 2 * LANE, D_INNER), lambda b, c: (0, 0, 0)),
            pl.BlockSpec((1,) + st_shape, lambda b, c: (b, 0, 0, 0)),
        ],
        out_specs=[
            pl.BlockSpec((tr, D_INNER), lambda b, c: (fidx(b, c), 0)),
            pl.BlockSpec((tr, D_INNER), lambda b, c: (bidx(b, c), 0)),
            pl.BlockSpec((1,) + st_shape, lambda b, c: (b, 0, 0, 0)),
        ],
        out_shape=[
            jax.ShapeDtypeStruct((batch * n, D_INNER), BF16),
            jax.ShapeDtypeStruct((batch * n, D_INNER), BF16),
            jax.ShapeDtypeStruct((batch,) + st_shape, F32),
        ],
        scratch_shapes=[pltpu.VMEM(st_shape, F32)],
        compiler_params=_params("parallel", "arbitrary"),
        name="ssd_scan",
    )(act, act, dt, act, act, dt, alog, dtbias, emat2, init)


def _merge_kernel(four_ref, att_ref, yf_ref, yb_ref, xs_ref, fg_ref, ag_ref, z_ref,
                  g0_ref, g1_ref, g2_ref, x_ref, mod_ref, dskip_ref, snw_ref,
                  wof_ref, woa_ref, wos_ref, wout_ref, nf_ref, o_ref, *, final_norm):
    f = lambda ref: ref[...].astype(F32)
    y_f = _dot((f(four_ref) * f(fg_ref)).astype(BF16), wof_ref[...])
    y_a = _dot((f(att_ref) * f(ag_ref)).astype(BF16), woa_ref[...])
    ys = f(yf_ref) + f(yb_ref) + dskip_ref[...] * f(xs_ref)
    t = ys * f(z_ref)
    t = t * lax.rsqrt(jnp.mean(t * t, axis=-1, keepdims=True) + EPS) * snw_ref[...]
    y_s = _dot(t.astype(BF16), wos_ref[...])
    y = f(g0_ref) * y_f + f(g1_ref) * y_a + f(g2_ref) * y_s
    out = _dot(y.astype(BF16), wout_ref[...])
    xn = x_ref[...] + mod_ref[0][:, 2 * D_MODEL:3 * D_MODEL] * out
    if final_norm:
        xn = xn * lax.rsqrt(jnp.mean(xn * xn, axis=-1, keepdims=True) + EPS) * nf_ref[...]
    o_ref[...] = xn


def _merge(four, att, yf, yb, p, x2, mod, dskip_w, snw, wof, woa, wos, wout, norm_f,
           n, tm, final_norm):
    rows = x2.shape[0]
    tiles_per_b = n // tm
    gcol = COL_GATES * CB // D_MODEL
    row_blk = lambda w, cidx: pl.BlockSpec((tm, w), lambda i: (i, cidx))
    const = lambda shape: pl.BlockSpec(shape, lambda i: (0,) * len(shape))
    return pl.pallas_call(
        functools.partial(_merge_kernel, final_norm=final_norm),
        grid=(rows // tm,),
        in_specs=[
            row_blk(F_W, 0), row_blk(CB, 0), row_blk(D_INNER, 0), row_blk(D_INNER, 0),
            row_blk(CB, 0),
            row_blk(CB, COL_FG), row_blk(CB, COL_AG), row_blk(CB, COL_Z),
            row_blk(D_MODEL, gcol), row_blk(D_MODEL, gcol + 1), row_blk(D_MODEL, gcol + 2),
            row_blk(D_MODEL, 0),
            pl.BlockSpec((1, 1, 3 * D_MODEL), lambda i: (i // tiles_per_b, 0, 0)),
            const((1, D_INNER)), const((1, D_INNER)),
            const((F_W, D_MODEL)), const((CB, D_MODEL)), const((D_INNER, D_MODEL)),
            const((D_MODEL, D_MODEL)), const((1, D_MODEL)),
        ],
        out_specs=pl.BlockSpec((tm, D_MODEL), lambda i: (i, 0)),
        out_shape=jax.ShapeDtypeStruct((rows, D_MODEL), F32),
        compiler_params=_params("parallel"),
        name="branch_merge",
    )(four, att, yf, yb, p, p, p, p, p, p, p, x2, mod, dskip_w, snw, wof, woa, wos, wout, norm_f)


def _rope_tables(n):
    rows = n // GRID_W
    row = jnp.repeat(jnp.arange(rows, dtype=F32), GRID_W)
    col = jnp.tile(jnp.arange(GRID_W, dtype=F32), rows)
    freqs = ROPE_BASE ** (-jnp.arange(0, ROPE_AXIS, 2, dtype=F32) / ROPE_AXIS)
    ang_r = row[:, None] * freqs
    ang_c = col[:, None] * freqs
    ang = jnp.concatenate([ang_r, ang_r, ang_c, ang_c], axis=-1)
    reps = CB // ATT_QK
    return jnp.tile(jnp.cos(ang), (1, reps)), jnp.tile(jnp.sin(ang), (1, reps))


def _head_expanders():
    e = np.zeros((2, LANE, D_INNER), np.float32)
    for d in range(2):
        for h in range(SSD_HEADS):
            e[d, d * SSD_HEADS + h, h * SSD_P:(h + 1) * SSD_P] = 1.0
    return jnp.asarray(np.concatenate([e, e], axis=1), BF16)


def _group_dft(w):
    cw, sw = _dft_tables(F_GROUP_W)
    eye = jnp.eye(w // F_GROUP_W, dtype=F32)
    cc, sc = jnp.kron(eye, cw), jnp.kron(eye, sw)
    return (jnp.concatenate([cc, cc], axis=0).astype(BF16),
            jnp.concatenate([sc, -sc], axis=0).astype(BF16))


def kernel(x, c, ctx, c_ctx, w_mod, b_mod, norm_w, w_in, conv_w, conv_b, a_log, dt_bias, d_skip,
           ssd_norm_w, lam, subln_w, w_of, w_oa, w_os, w_out, norm_f):
    batch, n, _ = x.shape
    n_ctx = ctx.shape[1]
    depth = w_mod.shape[0]
    assert n % GRID_W == 0 and n % CHUNK == 0 and n_ctx % CHUNK == 0

    o_q = 2 * F_W
    o_ag = o_q + 3 * CB
    o_xbc = o_ag + CB + D_INNER
    o_dt = o_xbc + XBC_W
    o_gt = o_dt + DT_W
    w_main = jnp.concatenate([w_in[:, :, o_xbc:o_dt], w_in[:, :, o_gt:], w_in[:, :, :o_q],
                              w_in[:, :, o_ag:o_xbc], w_in[:, :, o_q:o_ag]], axis=-1).astype(BF16)
    w_dt = jnp.pad(w_in[:, :, o_dt:o_gt], ((0, 0), (0, 0), (0, LANE - DT_W))).astype(BF16)
    conv_w8 = jnp.pad(conv_w, ((0, 0), (0, SUBLANE - CONV_W), (0, 0)))
    pad_lanes = lambda a: jnp.pad(a.reshape(depth, 1, DT_W), ((0, 0), (0, 0), (0, LANE - DT_W)))
    alog_p, dtbias_p = pad_lanes(a_log), pad_lanes(dt_bias)
    dskip_w = jnp.repeat(d_skip, SSD_P, axis=-1).reshape(depth, 1, D_INNER)
    wof_b, woa_b, wos_b, wout_b = (w.astype(BF16) for w in (w_of, w_oa, w_os, w_out))
    lam_inits = [0.8 - 0.6 * math.exp(-0.3 * l) for l in range(depth)]
    linit = jnp.asarray(np.broadcast_to(np.asarray(lam_inits, np.float32)[:, None, None],
                                        (depth, SUBLANE, LANE)))

    cos_t, sin_t = _rope_tables(n)
    tf_l, tf_c = min(n // 2, 512), min(n_ctx // 2, 512)
    cs_l = _seq_dft_table(n, tf_l)
    cs_c = _seq_dft_table(n_ctx, tf_c)
    ccd, scd = _group_dft(F_W)
    emat = _head_expanders()
    zero_state = jnp.zeros((batch, 2, D_STATE, D_INNER), F32)

    mod_rows = -(-(batch + 1) // SUBLANE) * SUBLANE
    cc = jnp.concatenate([c, c_ctx[None, :], jnp.zeros((mod_rows - batch - 1, D_MODEL), F32)], axis=0)
    mod_all, lam_all = _modulation(cc, w_mod, b_mod.reshape(depth, 1, 3 * D_MODEL), lam, linit)

    tm_l = min(n, 512)
    tq_l = min(n, 1024)
    tk_l = min(n, 1024)
    tmerge_l = min(n, 512)
    rows_c = batch * n_ctx
    tm_c = math.gcd(n_ctx, 512)
    cps_l = math.gcd(n // CHUNK, 8)
    cps_c = math.gcd(n_ctx // CHUNK, 2)
    xl = x.reshape(batch * n, D_MODEL)
    xc = ctx.reshape(batch * n_ctx, D_MODEL)
    for l in range(depth):
        last = l == depth - 1
        mod_l = mod_all[l, :batch].reshape(batch, 1, 3 * D_MODEL)
        mod_c = mod_all[l, batch].reshape(1, 1, 3 * D_MODEL)
        nw = norm_w[l].reshape(1, D_MODEL)
        cb_ = conv_b[l].reshape(1, XBC_W)
        p_c, dt_c = _inproj(xc, mod_c, nw, w_main[l], w_dt[l], conv_w8[l], cb_, None, None,
                            rows_c, n_ctx, tm_c)
        p_l, dt_l = _inproj(xl, mod_l, nw, w_main[l], w_dt[l], conv_w8[l], cb_, cos_t, sin_t,
                            n, n, tm_l)
        yf_c, yb_c, st_c = _ssd(p_c, dt_c, alog_p[l], dtbias_p[l], emat, zero_state, batch, n_ctx,
                                cps_c)
        yf_l, yb_l, _ = _ssd(p_l, dt_l, alog_p[l], dtbias_p[l], emat, st_c, batch, n, cps_l)
        sw = subln_w[l].reshape(1, ATT_V)
        att_l = _attention(p_l, p_c, p_l, lam_all[l], sw, lam_inits[l], batch, n, n_ctx, n,
                           tq_l, tk_l)
        four_l = _fourier(p_l, cs_l, ccd, scd, batch, n, tf_l)
        merge_w = (dskip_w[l], ssd_norm_w[l].reshape(1, D_INNER), wof_b[l], woa_b[l], wos_b[l],
                   wout_b[l], norm_f.reshape(1, D_MODEL))
        if not last:
            att_c = _attention(p_c, p_c, None, lam_all[l], sw, lam_inits[l], batch, n_ctx, n_ctx, 0,
                               n_ctx, 0)
            four_c = _fourier(p_c, cs_c, ccd, scd, batch, n_ctx, tf_c)
            xc = _merge(four_c, att_c, yf_c, yb_c, p_c, xc, mod_c, *merge_w, rows_c, n_ctx, False)
        xl = _merge(four_l, att_l, yf_l, yb_l, p_l, xl, mod_l, *merge_w, n, tmerge_l, last)
    return xl.reshape(batch, n, D_MODEL)
```

```python
import functools
import math

import numpy as np
import jax
import jax.numpy as jnp
from jax import lax
from jax.experimental import pallas as pl
from jax.experimental.pallas import tpu as pltpu

F32 = jnp.float32
BF16 = jnp.bfloat16
EPS = 1e-6

D_MODEL = 1024
GRID_W = 64
F_GROUP_W = 128
F_W = 512
ATT_HEADS = 4
ATT_QK = 64
ATT_V = 128
ROPE_AXIS = 32
ROPE_BASE = 10000.0
D_INNER = 512
SSD_P = 64
SSD_HEADS = 8
SSD_GROUPS = 2
D_STATE = 128
CONV_W = 5
CHUNK = 128
XBC_W = 1024
DT_W = 16
MERGE_W = 3072

LANE = 128
SUBLANE = 8
VMEM_LIMIT = 56 * 1024 * 1024

SUBLANE_BF16 = 16
CB = 512
COL_XBC, COL_GATES, COL_FU, COL_FG, COL_AG, COL_Z = 0, 2, 8, 9, 10, 11
COL_Q, COL_K, COL_V = 12, 13, 14
N_COLB = 15
N_MAIN = N_COLB * CB
PROJ_STEP_B = 3
PROJ_STEPS = N_COLB // PROJ_STEP_B
assert COL_Q == (PROJ_STEPS - 1) * PROJ_STEP_B and COL_V == N_COLB - 1
Q_SCALE = ATT_QK ** -0.5 * math.log2(math.e)


def _dot(a, b):
    return jnp.dot(a, b, preferred_element_type=F32)


def _dot_nt(a, b):
    return lax.dot_general(a, b, (((1,), (1,)), ((), ())), preferred_element_type=F32)


def _split3(x):
    x1 = x.astype(BF16)
    r1 = x - x1.astype(F32)
    x2 = r1.astype(BF16)
    x3 = (r1 - x2.astype(F32)).astype(BF16)
    return x1, x2, x3


def _dot_sel_r(x, sel):
    x1, x2, x3 = _split3(x)
    return _dot(x1, sel) + _dot(x2, sel) + _dot(x3, sel)


def _dot_sel_l(sel, x):
    x1, x2, x3 = _split3(x)
    return _dot(sel, x1) + _dot(sel, x2) + _dot(sel, x3)


def _dot_sel_r2(x, sel):
    x1 = x.astype(BF16)
    x2 = (x - x1.astype(F32)).astype(BF16)
    return _dot(x1, sel) + _dot(x2, sel)


def _sigmoid(x):
    return 0.5 * jnp.tanh(0.5 * x) + 0.5


def _silu(x):
    return x * _sigmoid(x)


def _params(*sem, flags=None):
    return pltpu.CompilerParams(dimension_semantics=sem, vmem_limit_bytes=VMEM_LIMIT, flags=flags)


def _mod_kernel(cc_ref, w_ref, b_ref, lam_ref, linit_ref, mod_ref, lam_out_ref):
    s = _silu(cc_ref[...])
    mod_ref[0] = jnp.dot(s, w_ref[0], precision=lax.Precision.HIGHEST,
                         preferred_element_type=F32) + b_ref[0]
    lp = lam_ref[0]
    s1 = jnp.sum(lp[0:1] * lp[1:2], axis=-1, keepdims=True)
    s2 = jnp.sum(lp[2:3] * lp[3:4], axis=-1, keepdims=True)
    lam_out_ref[0] = jnp.broadcast_to(jnp.exp(s1) - jnp.exp(s2), (SUBLANE, LANE)) + linit_ref[0]


def _modulation(cc, w_mod, b_mod, lam, linit):
    depth = w_mod.shape[0]
    rows = cc.shape[0]
    tn = D_MODEL
    return pl.pallas_call(
        _mod_kernel,
        grid=(depth, 3 * D_MODEL // tn),
        in_specs=[
            pl.BlockSpec((rows, D_MODEL), lambda l, j: (0, 0)),
            pl.BlockSpec((1, D_MODEL, tn), lambda l, j: (l, 0, j)),
            pl.BlockSpec((1, 1, tn), lambda l, j: (l, 0, j)),
            pl.BlockSpec((1, 4, ATT_QK), lambda l, j: (l, 0, 0)),
            pl.BlockSpec((1, SUBLANE, LANE), lambda l, j: (l, 0, 0)),
        ],
        out_specs=[
            pl.BlockSpec((1, rows, tn), lambda l, j: (l, 0, j)),
            pl.BlockSpec((1, SUBLANE, LANE), lambda l, j: (l, 0, 0)),
        ],
        out_shape=[
            jax.ShapeDtypeStruct((depth, rows, 3 * D_MODEL), F32),
            jax.ShapeDtypeStruct((depth, SUBLANE, LANE), F32),
        ],
        compiler_params=_params("arbitrary", "arbitrary"),
        name="adaln_mod",
    )(cc, w_mod, b_mod, lam, linit)


def _rope(t, cos, sin):
    w = t.shape[-1]
    lane = lax.broadcasted_iota(jnp.int32, t.shape, 1)
    first = (lane % ROPE_AXIS) < (ROPE_AXIS // 2)
    rot = jnp.where(first, -pltpu.roll(t, w - ROPE_AXIS // 2, 1), pltpu.roll(t, ROPE_AXIS // 2, 1))
    return t * cos + rot * sin


HALO = SUBLANE


def _inproj_kernel(*refs, rope, seq_tiles):
    if rope:
        (x_ref, xp_ref, xn_ref, mod_ref, nw_ref, w_ref, wdt_ref, cw_ref, cb_ref, cos_ref, sin_ref,
         p_ref, dt_ref, ext_ref) = refs
    else:
        (x_ref, xp_ref, xn_ref, mod_ref, nw_ref, w_ref, wdt_ref, cw_ref, cb_ref,
         p_ref, dt_ref, ext_ref) = refs
    tm = x_ref.shape[0]
    t = pl.program_id(0) % seq_tiles
    m = mod_ref[0]

    def norm_mod(x):
        y = x * lax.rsqrt(jnp.mean(x * x, axis=-1, keepdims=True) + EPS) * nw_ref[...]
        return (y * (1.0 + m[:, D_MODEL:2 * D_MODEL]) + m[:, 0:D_MODEL]).astype(BF16)

    hb = norm_mod(x_ref[...])
    dt_ref[...] = _dot(hb, wdt_ref[...])
    halo = _dot(norm_mod(jnp.concatenate([xp_ref[...], xn_ref[...]], axis=0)), w_ref[:, 0:XBC_W])
    ext_ref[0:HALO, :] = jnp.where(t == 0, 0.0, halo[0:HALO])
    ext_ref[HALO + tm:2 * HALO + tm, :] = jnp.where(t == seq_tiles - 1, 0.0, halo[HALO:2 * HALO])
    ws = PROJ_STEP_B * CB
    for j in range(PROJ_STEPS):
        acc = _dot(hb, w_ref[:, j * ws:(j + 1) * ws])
        if j < PROJ_STEPS - 1:
            for b in range(PROJ_STEP_B):
                blk, cols = j * PROJ_STEP_B + b, slice(b * CB, (b + 1) * CB)
                if blk * CB < XBC_W:
                    ext_ref[HALO:HALO + tm, blk * CB:(blk + 1) * CB] = acc[:, cols]
                    continue
                act = (_sigmoid if COL_GATES <= blk < COL_FU else
                       _silu if blk in (COL_FG, COL_AG, COL_Z) else (lambda t: t))
                p_ref[:, blk * CB:(blk + 1) * CB] = act(acc[:, cols]).astype(BF16)
        else:
            q, k, v = (acc[:, b * CB:(b + 1) * CB] for b in range(PROJ_STEP_B))
            if rope:
                cos, sin = cos_ref[...], sin_ref[...]
                q, k = _rope(q, cos, sin), _rope(k, cos, sin)
            p_ref[:, j * ws:j * ws + CB] = (q * Q_SCALE).astype(BF16)
            p_ref[:, j * ws + CB:j * ws + 2 * CB] = k.astype(BF16)
            p_ref[:, j * ws + 2 * CB:(j + 1) * ws] = v.astype(BF16)
    for cols in (slice(c0, c0 + CB) for c0 in range(0, XBC_W, CB)):
        conv = jnp.broadcast_to(cb_ref[:, cols], (tm, CB))
        for k in range(CONV_W):
            conv = conv + ext_ref[pl.ds(HALO - CONV_W // 2 + k, tm), cols] * cw_ref[k:k + 1, cols]
        p_ref[:, cols] = _silu(conv).astype(BF16)


def _inproj(x2, mod, norm_w, w_main, w_dt, conv_w8, conv_b, cos, sin, n_mod, n_seq, tm):
    rows = x2.shape[0]
    mod_tiles, seq_tiles = n_mod // tm, n_seq // tm
    hb = tm // HALO
    last_hb = rows // HALO - 1
    rope = cos is not None
    resident = lambda shape: pl.BlockSpec(shape, lambda i: (0, 0), pipeline_mode=pl.Buffered(1))
    in_specs = [
        pl.BlockSpec((tm, D_MODEL), lambda i: (i, 0)),
        pl.BlockSpec((HALO, D_MODEL), lambda i: (jnp.maximum(i * hb - 1, 0), 0)),
        pl.BlockSpec((HALO, D_MODEL), lambda i: (jnp.minimum((i + 1) * hb, last_hb), 0)),
        pl.BlockSpec((1, 1, 3 * D_MODEL), lambda i: (i // mod_tiles, 0, 0)),
        pl.BlockSpec((1, D_MODEL), lambda i: (0, 0)),
        resident((D_MODEL, N_MAIN)),
        resident((D_MODEL, LANE)),
        pl.BlockSpec((SUBLANE, XBC_W), lambda i: (0, 0)),
        pl.BlockSpec((1, XBC_W), lambda i: (0, 0)),
    ]
    args = [x2, x2, x2, mod, norm_w, w_main, w_dt, conv_w8, conv_b]
    if rope:
        in_specs += [pl.BlockSpec((tm, CB), lambda i: (i % seq_tiles, 0))] * 2
        args += [cos, sin]
    return pl.pallas_call(
        functools.partial(_inproj_kernel, rope=rope, seq_tiles=seq_tiles),
        grid=(rows // tm,),
        in_specs=in_specs,
        out_specs=[
            pl.BlockSpec((tm, N_MAIN), lambda i: (i, 0)),
            pl.BlockSpec((tm, LANE), lambda i: (i, 0)),
        ],
        out_shape=[
            jax.ShapeDtypeStruct((rows, N_MAIN), BF16),
            jax.ShapeDtypeStruct((rows, LANE), F32),
        ],
        scratch_shapes=[pltpu.VMEM((tm + 2 * HALO, XBC_W), F32)],
        compiler_params=_params("parallel"),
        name="inproj",
    )(*args)


def _attn_kernel(*refs, with_latent, one_minus_lam_init, n_ctx, chunks):
    if with_latent:
        (q_ref, kc_ref, vc_ref, kl_ref, vl_ref, lam_ref, sw_ref, o_ref,
         k_scr, v_scr, m_scr, acc_scr) = refs
    else:
        q_ref, kc_ref, vc_ref, lam_ref, sw_ref, o_ref, k_scr, v_scr, m_scr, acc_scr = refs
    tq = q_ref.shape[0]

    @pl.when(pl.program_id(2) == 0)
    def _():
        def put(k_ref, v_ref, off):
            rows = k_ref.shape[0]
            k = k_ref[...]
            lane = lax.broadcasted_iota(jnp.int32, k.shape, 1)
            zero = jnp.zeros_like(k)
            k_scr[0, off:off + rows, :] = jnp.where(lane < ATT_QK, k, zero)
            k_scr[1, off:off + rows, :] = jnp.where(lane >= ATT_QK, k, zero)
            v_scr[off:off + rows, 0:ATT_V] = v_ref[...]
            v_scr[off:off + rows, ATT_V:2 * ATT_V] = jnp.ones((rows, ATT_V), BF16)
        put(kc_ref, vc_ref, 0)
        if with_latent:
            put(kl_ref, vl_ref, n_ctx)

    m_scr[...] = jnp.full(m_scr.shape, -jnp.inf, F32)
    acc_scr[...] = jnp.zeros(acc_scr.shape, F32)
    q = q_ref[...]

    def chunk(off, size):
        v = v_scr[pl.ds(off, size), :]
        nt = size // LANE
        for mi in range(2):
            s = _dot_nt(q, k_scr[mi, pl.ds(off, size), :])
            tiles = [s[:, t * LANE:(t + 1) * LANE] for t in range(nt)]
            mx = functools.reduce(jnp.maximum, tiles)
            m_prev = m_scr[mi]
            m_new = jnp.maximum(m_prev, jnp.max(mx, axis=-1, keepdims=True))
            p = jnp.concatenate([jnp.exp2(t - m_new) for t in tiles], axis=1).astype(BF16)
            alpha = jnp.exp2(m_prev - m_new)
            acc_scr[mi] = jnp.concatenate([alpha, alpha], axis=1) * acc_scr[mi] + _dot(p, v)
            m_scr[mi] = m_new

    for off, size in chunks:
        chunk(off, size)

    lam = lam_ref[0:1, :]
    a0, a1 = acc_scr[0], acc_scr[1]
    o = a0[:, 0:ATT_V] / a0[:, ATT_V:] - lam * (a1[:, 0:ATT_V] / a1[:, ATT_V:])
    y = o * lax.rsqrt(jnp.mean(o * o, axis=-1, keepdims=True) + EPS) * sw_ref[...]
    o_ref[...] = (y * one_minus_lam_init).astype(o_ref.dtype)


def _attention(qkv_q, qkv_c, qkv_l, lam_l, subln_w, lam_init, batch, nq_len, nc_len, nl_len, tq, tk):
    with_latent = qkv_l is not None
    nq = nq_len // tq
    hq, hk, hv = (c * (CB // LANE) for c in (COL_Q, COL_K, COL_V))
    in_specs = [
        pl.BlockSpec((tq, LANE), lambda b, h, i: (b * nq + i, hq + h)),
        pl.BlockSpec((nc_len, LANE), lambda b, h, i: (b, hk + h)),
        pl.BlockSpec((nc_len, LANE), lambda b, h, i: (b, hv + h)),
    ]
    args = [qkv_q, qkv_c, qkv_c]
    if with_latent:
        in_specs += [
            pl.BlockSpec((nl_len, LANE), lambda b, h, i: (b, hk + h)),
            pl.BlockSpec((nl_len, LANE), lambda b, h, i: (b, hv + h)),
        ]
        args += [qkv_l, qkv_l]
    in_specs += [
        pl.BlockSpec((SUBLANE, LANE), lambda b, h, i: (0, 0)),
        pl.BlockSpec((1, ATT_V), lambda b, h, i: (0, 0)),
    ]
    args += [lam_l, subln_w]
    n_keys = nc_len + nl_len
    sizes = [nc_len + tk] + [tk] * (nl_len // tk - 1) if with_latent else [nc_len]
    chunks = tuple((sum(sizes[:t]), sizes[t]) for t in range(len(sizes)))
    assert sum(sizes) == n_keys
    return pl.pallas_call(
        functools.partial(_attn_kernel, with_latent=with_latent,
                          one_minus_lam_init=1.0 - lam_init, n_ctx=nc_len, chunks=chunks),
        grid=(batch, ATT_HEADS, nq),
        in_specs=in_specs,
        out_specs=pl.BlockSpec((tq, ATT_V), lambda b, h, i: (b * nq + i, h)),
        out_shape=jax.ShapeDtypeStruct((batch * nq_len, ATT_HEADS * ATT_V), BF16),
        scratch_shapes=[
            pltpu.VMEM((2, n_keys, LANE), BF16),
            pltpu.VMEM((n_keys, 2 * ATT_V), BF16),
            pltpu.VMEM((2, tq, LANE), F32),
            pltpu.VMEM((2, tq, 2 * ATT_V), F32),
        ],
        compiler_params=_params("parallel", "parallel", "arbitrary"),
        name="diff_attention",
    )(*args)


def _fourier_kernel(x_ref, xb_ref, xn_ref, xm_ref, rev_ref, cs_ref, cc_ref, sc_ref, o_ref, acc_ref,
                    *, scale):
    k = pl.program_id(1)
    xn = jnp.where(k == 0, jnp.zeros_like(xn_ref[...]), xn_ref[...])
    pad = jnp.zeros((LANE - xn.shape[0], xn.shape[1]), BF16)
    xr = _dot(rev_ref[...], jnp.concatenate([xb_ref[...], xn, pad], axis=0)).astype(BF16)

    @pl.when(k == 0)
    def _():
        mid = _dot(xm_ref[...], cc_ref[0:F_W, :])[0:1, :]
        row = lax.broadcasted_iota(jnp.int32, acc_ref.shape, 0)
        acc_ref[...] = jnp.where(row % 2 == 0, mid, -mid)

    xx = jnp.concatenate([x_ref[...], xr], axis=1)
    ue = _dot(xx, cc_ref[...]).astype(BF16)
    wo = _dot(xx, sc_ref[...]).astype(BF16)
    acc_ref[...] += _dot(cs_ref[...], jnp.concatenate([ue, wo], axis=0))

    @pl.when(k == pl.num_programs(1) - 1)
    def _():
        o_ref[...] = (acc_ref[...] * scale).astype(o_ref.dtype)


DFT_SPLIT = 64


def _dft_tables(n, cols=None):
    cols = jnp.arange(n, dtype=jnp.int32) if cols is None else cols
    m = cols.shape[0]

    def cos_sin(rows):
        ang = ((rows[:, None] * cols[None, :]) % n).astype(F32) * (2.0 * math.pi / n)
        return jnp.cos(ang), jnp.sin(ang)

    if n <= DFT_SPLIT * DFT_SPLIT // 4 or n % DFT_SPLIT:
        return cos_sin(jnp.arange(n, dtype=jnp.int32))
    ca, sa = cos_sin(jnp.arange(n // DFT_SPLIT, dtype=jnp.int32) * DFT_SPLIT)
    cb, sb = cos_sin(jnp.arange(DFT_SPLIT, dtype=jnp.int32))
    ca, sa, cb, sb = ca[:, None, :], sa[:, None, :], cb[None, :, :], sb[None, :, :]
    return (ca * cb - sa * sb).reshape(n, m), (sa * cb + ca * sb).reshape(n, m)


def _seq_dft_table(n, tk):
    col = jnp.arange(n, dtype=jnp.int32)
    k = (col // (2 * tk)) * tk + col % tk
    cn, sn = _dft_tables(n, k)
    return jnp.where(((col // tk) % 2 == 0)[None, :], cn, -sn).astype(BF16)


def _fourier(p, cs, ccd2, scd2, batch, n, tk):
    half = n // 2
    nk, nkh = n // tk, half // tk
    scale = 1.0 / math.sqrt(n * F_GROUP_W)
    sb = SUBLANE_BF16
    rev = np.zeros((tk, tk + LANE), np.float32)
    rev[np.arange(1, tk), tk - np.arange(1, tk)] = 1.0
    rev[0, tk] = 1.0
    seq_blks, tk_blks = n // sb, tk // sb
    mid_blk = half // sb
    return pl.pallas_call(
        functools.partial(_fourier_kernel, scale=scale),
        grid=(batch, nkh),
        in_specs=[
            pl.BlockSpec((tk, CB), lambda b, k: (b * nk + k, COL_FU)),
            pl.BlockSpec((tk, CB), lambda b, k: (b * nk + nk - 1 - k, COL_FU)),
            pl.BlockSpec((sb, CB), lambda b, k: (b * seq_blks + jnp.minimum((nk - k) * tk_blks,
                                                                            seq_blks - 1), COL_FU)),
            pl.BlockSpec((sb, CB), lambda b, k: (b * seq_blks + mid_blk, COL_FU)),
            pl.BlockSpec((tk, tk + LANE), lambda b, k: (0, 0)),
            pl.BlockSpec((n, 2 * tk), lambda b, k: (0, k)),
            pl.BlockSpec((2 * F_W, F_W), lambda b, k: (0, 0)),
            pl.BlockSpec((2 * F_W, F_W), lambda b, k: (0, 0)),
        ],
        out_specs=pl.BlockSpec((n, F_W), lambda b, k: (b, 0)),
        out_shape=jax.ShapeDtypeStruct((batch * n, F_W), BF16),
        scratch_shapes=[pltpu.VMEM((n, F_W), F32)],
        compiler_params=_params("parallel", "arbitrary"),
        name="fourier_mix",
    )(p, p, p, p, jnp.asarray(rev, BF16), cs, ccd2, scd2)


def _softplus(x):
    return jnp.maximum(x, 0.0) + jnp.log1p(jnp.exp(-jnp.abs(x)))


def _ssd_kernel(xsf_ref, bcf_ref, dtf_ref, xsb_ref, bcb_ref, dtb_ref, alog_ref, dtbias_ref,
                e_ref, init_ref, yf_ref, yb_ref, fin_ref, state_ref):
    c = pl.program_id(1)
    q = CHUNK
    cps = xsf_ref.shape[0] // q
    gw = D_INNER // SSD_GROUPS
    hpg = SSD_HEADS // SSD_GROUPS

    @pl.when(c == 0)
    def _():
        state_ref[...] = init_ref[0]

    row = lax.broadcasted_iota(jnp.int32, (q, q), 0)
    col = lax.broadcasted_iota(jnp.int32, (q, q), 1)
    head_of_lane = lax.broadcasted_iota(jnp.int32, (q, gw), 1) // SSD_P
    neg_a = -jnp.exp(alog_ref[...])
    dtbias = dtbias_ref[...]

    def expand(x, e2):
        x1 = x.astype(BF16)
        x2 = (x - x1.astype(F32)).astype(BF16)
        return _dot(jnp.concatenate([x1, x2], axis=1), e2)

    def one_chunk(d, rows, xs_ref, bc_ref, dt_ref, y_ref):
        fwd = d == 0
        tri = (row >= col) if fwd else (row <= col)
        tri_b = tri.astype(F32).astype(BF16)
        dt = _softplus(dt_ref[rows, :] + dtbias)
        a1, a2, a3 = _split3(dt * neg_a)
        acs = (_dot(jnp.concatenate([tri_b, tri_b], axis=1), jnp.concatenate([a1, a2], axis=0))
               + _dot(tri_b, a3))
        acs_t = acs.T
        last = q - 1 if fwd else 0
        tot_row = acs[last:last + 1, :]
        e2 = e_ref[d]
        dt_w = expand(dt, e2)
        eacs_w = expand(jnp.exp(acs), e2)
        dec_w = expand(jnp.exp(tot_row - acs), e2)
        sdec = _dot_sel_r(jnp.broadcast_to(jnp.exp(tot_row), (SUBLANE, LANE)), e2[0:LANE])[0:1, :]
        xd = xs_ref[rows, :].astype(F32) * dt_w
        xdb = xd.astype(BF16)
        xdd = (xd * dec_w).astype(BF16)
        bc = bc_ref[rows, :]
        for g in range(SSD_GROUPS):
            bg = bc[:, g * D_STATE:(g + 1) * D_STATE]
            cg = bc[:, (SSD_GROUPS + g) * D_STATE:(SSD_GROUPS + g + 1) * D_STATE]
            cb = _dot_nt(cg, bg)
            lanes = slice(g * gw, (g + 1) * gw)
            s_g = state_ref[d, :, lanes]
            y = _dot(cg, s_g.astype(BF16)) * eacs_w[:, lanes]
            xg = xdb[:, lanes]
            zero = jnp.zeros_like(xg)
            for r in range(0, hpg, 2):
                mats, xms = [], []
                for rr in (r, r + 1):
                    jl = d * SSD_HEADS + g * hpg + rr
                    seg = jnp.where(tri, acs[:, jl:jl + 1] - acs_t[jl:jl + 1, :], -jnp.inf)
                    mats.append((cb * jnp.exp(seg)).astype(BF16))
                    xms.append(jnp.where(head_of_lane == rr, xg, zero))
                y = y + _dot(jnp.concatenate(mats, axis=1), jnp.concatenate(xms, axis=0))
            y_ref[rows, lanes] = y.astype(y_ref.dtype)
            bg_t = bg.astype(F32).T.astype(BF16)
            state_ref[d, :, lanes] = s_g * sdec[:, lanes] + _dot(bg_t, xdd[:, lanes])

    for i in range(cps):
        one_chunk(0, pl.ds(i * q, q), xsf_ref, bcf_ref, dtf_ref, yf_ref)
        one_chunk(1, pl.ds((cps - 1 - i) * q, q), xsb_ref, bcb_ref, dtb_ref, yb_ref)

    @pl.when(c == pl.num_programs(1) - 1)
    def _():
        fin_ref[0] = state_ref[...]


def _ssd(act, dt, alog, dtbias, emat2, init, batch, n, cps):
    nc = n // (CHUNK * cps)
    tr = CHUNK * cps
    fidx = lambda b, c: b * nc + c
    bidx = lambda b, c: b * nc + (nc - 1 - c)
    st_shape = (2, D_STATE, D_INNER)
    return pl.pallas_call(
        _ssd_kernel,
        grid=(batch, nc),
        in_specs=[
            pl.BlockSpec((tr, CB), lambda b, c: (fidx(b, c), 0)),
            pl.BlockSpec((tr, CB), lambda b, c: (fidx(b, c), 1)),
            pl.BlockSpec((tr, LANE), lambda b, c: (fidx(b, c), 0)),
            pl.BlockSpec((tr, CB), lambda b, c: (bidx(b, c), 0)),
            pl.BlockSpec((tr, CB), lambda b, c: (bidx(b, c), 1)),
            pl.BlockSpec((tr, LANE), lambda b, c: (bidx(b, c), 0)),
            pl.BlockSpec((1, LANE), lambda b, c: (0, 0)),
            pl.BlockSpec((1, LANE), lambda b, c: (0, 0)),
            pl.BlockSpec((2, 2 * LANE, D_INNER), lambda b, c: (0, 0, 0)),
            pl.BlockSpec((1,) + st_shape, lambda b, c: (b, 0, 0, 0)),
        ],
        out_specs=[
            pl.BlockSpec((tr, D_INNER), lambda b, c: (fidx(b, c), 0)),
            pl.BlockSpec((tr, D_INNER), lambda b, c: (bidx(b, c), 0)),
            pl.BlockSpec((1,) + st_shape, lambda b, c: (b, 0, 0, 0)),
        ],
        out_shape=[
            jax.ShapeDtypeStruct((batch * n, D_INNER), BF16),
            jax.ShapeDtypeStruct((batch * n, D_INNER), BF16),
            jax.ShapeDtypeStruct((batch,) + st_shape, F32),
        ],
        scratch_shapes=[pltpu.VMEM(st_shape, F32)],
        compiler_params=_params("parallel", "arbitrary"),
        name="ssd_scan",
    )(act, act, dt, act, act, dt, alog, dtbias, emat2, init)


def _merge_kernel(four_ref, att_ref, yf_ref, yb_ref, xs_ref, fg_ref, ag_ref, z_ref,
                  g0_ref, g1_ref, g2_ref, x_ref, mod_ref, dskip_ref, snw_ref,
                  wof_ref, woa_ref, wos_ref, wout_ref, nf_ref, o_ref, *, final_norm):
    f = lambda ref: ref[...].astype(F32)
    y_f = _dot((f(four_ref) * f(fg_ref)).astype(BF16), wof_ref[...])
    y_a = _dot((f(att_ref) * f(ag_ref)).astype(BF16), woa_ref[...])
    ys = f(yf_ref) + f(yb_ref) + dskip_ref[...] * f(xs_ref)
    t = ys * f(z_ref)
    t = t * lax.rsqrt(jnp.mean(t * t, axis=-1, keepdims=True) + EPS) * snw_ref[...]
    y_s = _dot(t.astype(BF16), wos_ref[...])
    y = f(g0_ref) * y_f + f(g1_ref) * y_a + f(g2_ref) * y_s
    out = _dot(y.astype(BF16), wout_ref[...])
    xn = x_ref[...] + mod_ref[0][:, 2 * D_MODEL:3 * D_MODEL] * out
    if final_norm:
        xn = xn * lax.rsqrt(jnp.mean(xn * xn, axis=-1, keepdims=True) + EPS) * nf_ref[...]
    o_ref[...] = xn


def _merge(four, att, yf, yb, p, x2, mod, dskip_w, snw, wof, woa, wos, wout, norm_f,
           n, tm, final_norm):
    rows = x2.shape[0]
    tiles_per_b = n // tm
    gcol = COL_GATES * CB // D_MODEL
    row_blk = lambda w, cidx: pl.BlockSpec((tm, w), lambda i: (i, cidx))
    const = lambda shape: pl.BlockSpec(shape, lambda i: (0,) * len(shape))
    return pl.pallas_call(
        functools.partial(_merge_kernel, final_norm=final_norm),
        grid=(rows // tm,),
        in_specs=[
            row_blk(F_W, 0), row_blk(CB, 0), row_blk(D_INNER, 0), row_blk(D_INNER, 0),
            row_blk(CB, 0),
            row_blk(CB, COL_FG), row_blk(CB, COL_AG), row_blk(CB, COL_Z),
            row_blk(D_MODEL, gcol), row_blk(D_MODEL, gcol + 1), row_blk(D_MODEL, gcol + 2),
            row_blk(D_MODEL, 0),
            pl.BlockSpec((1, 1, 3 * D_MODEL), lambda i: (i // tiles_per_b, 0, 0)),
            const((1, D_INNER)), const((1, D_INNER)),
            const((F_W, D_MODEL)), const((CB, D_MODEL)), const((D_INNER, D_MODEL)),
            const((D_MODEL, D_MODEL)), const((1, D_MODEL)),
        ],
        out_specs=pl.BlockSpec((tm, D_MODEL), lambda i: (i, 0)),
        out_shape=jax.ShapeDtypeStruct((rows, D_MODEL), F32),
        compiler_params=_params("parallel"),
        name="branch_merge",
    )(four, att, yf, yb, p, p, p, p, p, p, p, x2, mod, dskip_w, snw, wof, woa, wos, wout, norm_f)


def _rope_tables(n):
    rows = n // GRID_W
    row = jnp.repeat(jnp.arange(rows, dtype=F32), GRID_W)
    col = jnp.tile(jnp.arange(GRID_W, dtype=F32), rows)
    freqs = ROPE_BASE ** (-jnp.arange(0, ROPE_AXIS, 2, dtype=F32) / ROPE_AXIS)
    ang_r = row[:, None] * freqs
    ang_c = col[:, None] * freqs
    ang = jnp.concatenate([ang_r, ang_r, ang_c, ang_c], axis=-1)
    reps = CB // ATT_QK
    return jnp.tile(jnp.cos(ang), (1, reps)), jnp.tile(jnp.sin(ang), (1, reps))


def _head_expanders():
    e = np.zeros((2, LANE, D_INNER), np.float32)
    for d in range(2):
        for h in range(SSD_HEADS):
            e[d, d * SSD_HEADS + h, h * SSD_P:(h + 1) * SSD_P] = 1.0
    return jnp.asarray(np.concatenate([e, e], axis=1), BF16)


def _group_dft(w):
    cw, sw = _dft_tables(F_GROUP_W)
    eye = jnp.eye(w // F_GROUP_W, dtype=F32)
    cc, sc = jnp.kron(eye, cw), jnp.kron(eye, sw)
    return (jnp.concatenate([cc, cc], axis=0).astype(BF16),
            jnp.concatenate([sc, -sc], axis=0).astype(BF16))


def kernel(x, c, ctx, c_ctx, w_mod, b_mod, norm_w, w_in, conv_w, conv_b, a_log, dt_bias, d_skip,
           ssd_norm_w, lam, subln_w, w_of, w_oa, w_os, w_out, norm_f):
    batch, n, _ = x.shape
    n_ctx = ctx.shape[1]
    depth = w_mod.shape[0]
    assert n % GRID_W == 0 and n % CHUNK == 0 and n_ctx % CHUNK == 0

    o_q = 2 * F_W
    o_ag = o_q + 3 * CB
    o_xbc = o_ag + CB + D_INNER
    o_dt = o_xbc + XBC_W
    o_gt = o_dt + DT_W
    w_main = jnp.concatenate([w_in[:, :, o_xbc:o_dt], w_in[:, :, o_gt:], w_in[:, :, :o_q],
                              w_in[:, :, o_ag:o_xbc], w_in[:, :, o_q:o_ag]], axis=-1).astype(BF16)
    w_dt = jnp.pad(w_in[:, :, o_dt:o_gt], ((0, 0), (0, 0), (0, LANE - DT_W))).astype(BF16)
    conv_w8 = jnp.pad(conv_w, ((0, 0), (0, SUBLANE - CONV_W), (0, 0)))
    pad_lanes = lambda a: jnp.pad(a.reshape(depth, 1, DT_W), ((0, 0), (0, 0), (0, LANE - DT_W)))
    alog_p, dtbias_p = pad_lanes(a_log), pad_lanes(dt_bias)
    dskip_w = jnp.repeat(d_skip, SSD_P, axis=-1).reshape(depth, 1, D_INNER)
    wof_b, woa_b, wos_b, wout_b = (w.astype(BF16) for w in (w_of, w_oa, w_os, w_out))
    lam_inits = [0.8 - 0.6 * math.exp(-0.3 * l) for l in range(depth)]
    linit = jnp.asarray(np.broadcast_to(np.asarray(lam_inits, np.float32)[:, None, None],
                                        (depth, SUBLANE, LANE)))

    cos_t, sin_t = _rope_tables(n)
    tf_l, tf_c = min(n // 2, 512), min(n_ctx // 2, 512)
    cs_l = _seq_dft_table(n, tf_l)
    cs_c = _seq_dft_table(n_ctx, tf_c)
    ccd, scd = _group_dft(F_W)
    emat = _head_expanders()
    zero_state = jnp.zeros((batch, 2, D_STATE, D_INNER), F32)

    mod_rows = -(-(batch + 1) // SUBLANE) * SUBLANE
    cc = jnp.concatenate([c, c_ctx[None, :], jnp.zeros((mod_rows - batch - 1, D_MODEL), F32)], axis=0)
    mod_all, lam_all = _modulation(cc, w_mod, b_mod.reshape(depth, 1, 3 * D_MODEL), lam, linit)

    tm_l = min(n, 512)
    tq_l = min(n, 1024)
    tk_l = min(n, 1024)
    tmerge_l = min(n, 512)
    rows_c = batch * n_ctx
    tm_c = math.gcd(n_ctx, 512)
    cps_l = math.gcd(n // CHUNK, 8)
    cps_c = math.gcd(n_ctx // CHUNK, 2)
    xl = x.reshape(batch * n, D_MODEL)
    xc = ctx.reshape(batch * n_ctx, D_MODEL)
    for l in range(depth):
        last = l == depth - 1
        mod_l = mod_all[l, :batch].reshape(batch, 1, 3 * D_MODEL)
        mod_c = mod_all[l, batch].reshape(1, 1, 3 * D_MODEL)
        nw = norm_w[l].reshape(1, D_MODEL)
        cb_ = conv_b[l].reshape(1, XBC_W)
        p_c, dt_c = _inproj(xc, mod_c, nw, w_main[l], w_dt[l], conv_w8[l], cb_, None, None,
                            rows_c, n_ctx, tm_c)
        p_l, dt_l = _inproj(xl, mod_l, nw, w_main[l], w_dt[l], conv_w8[l], cb_, cos_t, sin_t,
                            n, n, tm_l)
        yf_c, yb_c, st_c = _ssd(p_c, dt_c, alog_p[l], dtbias_p[l], emat, zero_state, batch, n_ctx,
                                cps_c)
        yf_l, yb_l, _ = _ssd(p_l, dt_l, alog_p[l], dtbias_p[l], emat, st_c, batch, n, cps_l)
        sw = subln_w[l].reshape(1, ATT_V)
        att_l = _attention(p_l, p_c, p_l, lam_all[l], sw, lam_inits[l], batch, n, n_ctx, n,
                           tq_l, tk_l)
        four_l = _fourier(p_l, cs_l, ccd, scd, batch, n, tf_l)
        merge_w = (dskip_w[l], ssd_norm_w[l].reshape(1, D_INNER), wof_b[l], woa_b[l], wos_b[l],
                   wout_b[l], norm_f.reshape(1, D_MODEL))
        if not last:
            att_c = _attention(p_c, p_c, None, lam_all[l], sw, lam_inits[l], batch, n_ctx, n_ctx, 0,
                               n_ctx, 0)
            four_c = _fourier(p_c, cs_c, ccd, scd, batch, n_ctx, tf_c)
            xc = _merge(four_c, att_c, yf_c, yb_c, p_c, xc, mod_c, *merge_w, rows_c, n_ctx, False)
        xl = _merge(four_l, att_l, yf_l, yb_l, p_l, xl, mod_l, *merge_w, n, tmerge_l, last)
    return xl.reshape(batch, n, D_MODEL)
```

```python
import functools
import math

import numpy as np
import jax
import jax.numpy as jnp
from jax import lax
from jax.experimental import pallas as pl
from jax.experimental.pallas import tpu as pltpu

F32 = jnp.float32
BF16 = jnp.bfloat16
EPS = 1e-6

D_MODEL = 1024
GRID_W = 64
F_GROUP_W = 128
F_W = 512
ATT_HEADS = 4
ATT_QK = 64
ATT_V = 128
ROPE_AXIS = 32
ROPE_BASE = 10000.0
D_INNER = 512
SSD_P = 64
SSD_HEADS = 8
SSD_GROUPS = 2
D_STATE = 128
CONV_W = 5
CHUNK = 128
XBC_W = 1024
DT_W = 16

LANE = 128
SUBLANE = 8
SUBLANE_BF16 = 16
VMEM_LIMIT = 56 * 1024 * 1024

TM_PROJ = 512
TQ_ATT = 1024
TK_ATT = 1024
TK_FOURIER = 512
TM_MERGE = 512
SSD_CHUNKS_PER_STEP = 8

CB = 512
COL_XBC, COL_GATES, COL_FU, COL_FG, COL_AG, COL_Z = 0, 2, 8, 9, 10, 11
COL_Q, COL_K, COL_V = 12, 13, 14
N_COLB = 15
N_MAIN = N_COLB * CB
PROJ_STEP_B = 3
PROJ_STEPS = N_COLB // PROJ_STEP_B
assert COL_Q == (PROJ_STEPS - 1) * PROJ_STEP_B and COL_V == N_COLB - 1
Q_SCALE = ATT_QK ** -0.5 * math.log2(math.e)


def _dot(a, b):
    return jnp.dot(a, b, preferred_element_type=F32)


def _dot_nt(a, b):
    return lax.dot_general(a, b, (((1,), (1,)), ((), ())), preferred_element_type=F32)


def _split3(x):
    x1 = x.astype(BF16)
    r1 = x - x1.astype(F32)
    x2 = r1.astype(BF16)
    x3 = (r1 - x2.astype(F32)).astype(BF16)
    return x1, x2, x3


def _dot_sel_r(x, sel):
    x1, x2, x3 = _split3(x)
    return _dot(x1, sel) + _dot(x2, sel) + _dot(x3, sel)


def _sigmoid(x):
    return 0.5 * jnp.tanh(0.5 * x) + 0.5


def _silu(x):
    return x * _sigmoid(x)


def _params(*sem):
    return pltpu.CompilerParams(dimension_semantics=sem, vmem_limit_bytes=VMEM_LIMIT)


def _mod_kernel(cc_ref, w_ref, b_ref, lam_ref, linit_ref, mod_ref, lam_out_ref):
    s = _silu(cc_ref[...])
    mod_ref[0] = jnp.dot(s, w_ref[0], precision=lax.Precision.HIGHEST,
                         preferred_element_type=F32) + b_ref[0]
    lp = lam_ref[0]
    s1 = jnp.sum(lp[0:1] * lp[1:2], axis=-1, keepdims=True)
    s2 = jnp.sum(lp[2:3] * lp[3:4], axis=-1, keepdims=True)
    lam_out_ref[0] = jnp.broadcast_to(jnp.exp(s1) - jnp.exp(s2), (SUBLANE, LANE)) + linit_ref[0]


def _modulation(cc, w_mod, b_mod, lam, linit):
    depth = w_mod.shape[0]
    rows = cc.shape[0]
    tn = D_MODEL
    return pl.pallas_call(
        _mod_kernel,
        grid=(depth, 3 * D_MODEL // tn),
        in_specs=[
            pl.BlockSpec((rows, D_MODEL), lambda l, j: (0, 0)),
            pl.BlockSpec((1, D_MODEL, tn), lambda l, j: (l, 0, j)),
            pl.BlockSpec((1, 1, tn), lambda l, j: (l, 0, j)),
            pl.BlockSpec((1, 4, ATT_QK), lambda l, j: (l, 0, 0)),
            pl.BlockSpec((1, SUBLANE, LANE), lambda l, j: (l, 0, 0)),
        ],
        out_specs=[
            pl.BlockSpec((1, rows, tn), lambda l, j: (l, 0, j)),
            pl.BlockSpec((1, SUBLANE, LANE), lambda l, j: (l, 0, 0)),
        ],
        out_shape=[
            jax.ShapeDtypeStruct((depth, rows, 3 * D_MODEL), F32),
            jax.ShapeDtypeStruct((depth, SUBLANE, LANE), F32),
        ],
        compiler_params=_params("arbitrary", "arbitrary"),
        name="adaln_mod",
    )(cc, w_mod, b_mod, lam, linit)


def _rope(t, cos, sin):
    w = t.shape[-1]
    lane = lax.broadcasted_iota(jnp.int32, t.shape, 1)
    first = (lane % ROPE_AXIS) < (ROPE_AXIS // 2)
    rot = jnp.where(first, -pltpu.roll(t, w - ROPE_AXIS // 2, 1), pltpu.roll(t, ROPE_AXIS // 2, 1))
    return t * cos + rot * sin


HALO = SUBLANE


def _inproj_kernel(*refs, rope, seq_tiles):
    if rope:
        (x_ref, xp_ref, xn_ref, mod_ref, nw_ref, w_ref, wdt_ref, cw_ref, cb_ref, cos_ref, sin_ref,
         p_ref, dt_ref, ext_ref) = refs
    else:
        (x_ref, xp_ref, xn_ref, mod_ref, nw_ref, w_ref, wdt_ref, cw_ref, cb_ref,
         p_ref, dt_ref, ext_ref) = refs
    tm = x_ref.shape[0]
    t = pl.program_id(0) % seq_tiles
    m = mod_ref[0]

    def norm_mod(x):
        y = x * lax.rsqrt(jnp.mean(x * x, axis=-1, keepdims=True) + EPS) * nw_ref[...]
        return (y * (1.0 + m[:, D_MODEL:2 * D_MODEL]) + m[:, 0:D_MODEL]).astype(BF16)

    hb = norm_mod(x_ref[...])
    dt_ref[...] = _dot(hb, wdt_ref[...])
    halo = _dot(norm_mod(jnp.concatenate([xp_ref[...], xn_ref[...]], axis=0)), w_ref[:, 0:XBC_W])
    ext_ref[0:HALO, :] = jnp.where(t == 0, 0.0, halo[0:HALO])
    ext_ref[HALO + tm:2 * HALO + tm, :] = jnp.where(t == seq_tiles - 1, 0.0, halo[HALO:2 * HALO])
    ws = PROJ_STEP_B * CB
    for j in range(PROJ_STEPS):
        acc = _dot(hb, w_ref[:, j * ws:(j + 1) * ws])
        if j < PROJ_STEPS - 1:
            for b in range(PROJ_STEP_B):
                blk, cols = j * PROJ_STEP_B + b, slice(b * CB, (b + 1) * CB)
                if blk * CB < XBC_W:
                    ext_ref[HALO:HALO + tm, blk * CB:(blk + 1) * CB] = acc[:, cols]
                    continue
                act = (_sigmoid if COL_GATES <= blk < COL_FU else
                       _silu if blk in (COL_FG, COL_AG, COL_Z) else (lambda t: t))
                p_ref[:, blk * CB:(blk + 1) * CB] = act(acc[:, cols]).astype(BF16)
        else:
            q, k, v = (acc[:, b * CB:(b + 1) * CB] for b in range(PROJ_STEP_B))
            if rope:
                cos, sin = cos_ref[...], sin_ref[...]
                q, k = _rope(q, cos, sin), _rope(k, cos, sin)
            p_ref[:, j * ws:j * ws + CB] = (q * Q_SCALE).astype(BF16)
            p_ref[:, j * ws + CB:j * ws + 2 * CB] = k.astype(BF16)
            p_ref[:, j * ws + 2 * CB:(j + 1) * ws] = v.astype(BF16)
    for cols in (slice(c0, c0 + CB) for c0 in range(0, XBC_W, CB)):
        conv = jnp.broadcast_to(cb_ref[:, cols], (tm, CB))
        for k in range(CONV_W):
            conv = conv + ext_ref[pl.ds(HALO - CONV_W // 2 + k, tm), cols] * cw_ref[k:k + 1, cols]
        p_ref[:, cols] = _silu(conv).astype(BF16)


def _inproj(x2, mod, norm_w, w_main, w_dt, conv_w8, conv_b, cos, sin, n_mod, n_seq, tm):
    rows = x2.shape[0]
    mod_tiles, seq_tiles = n_mod // tm, n_seq // tm
    hb = tm // HALO
    last_hb = rows // HALO - 1
    rope = cos is not None
    resident = lambda shape: pl.BlockSpec(shape, lambda i: (0, 0), pipeline_mode=pl.Buffered(1))
    in_specs = [
        pl.BlockSpec((tm, D_MODEL), lambda i: (i, 0)),
        pl.BlockSpec((HALO, D_MODEL), lambda i: (jnp.maximum(i * hb - 1, 0), 0)),
        pl.BlockSpec((HALO, D_MODEL), lambda i: (jnp.minimum((i + 1) * hb, last_hb), 0)),
        pl.BlockSpec((1, 1, 3 * D_MODEL), lambda i: (i // mod_tiles, 0, 0)),
        pl.BlockSpec((1, D_MODEL), lambda i: (0, 0)),
        resident((D_MODEL, N_MAIN)),
        resident((D_MODEL, LANE)),
        pl.BlockSpec((SUBLANE, XBC_W), lambda i: (0, 0)),
        pl.BlockSpec((1, XBC_W), lambda i: (0, 0)),
    ]
    args = [x2, x2, x2, mod, norm_w, w_main, w_dt, conv_w8, conv_b]
    if rope:
        in_specs += [pl.BlockSpec((tm, CB), lambda i: (i % seq_tiles, 0))] * 2
        args += [cos, sin]
    return pl.pallas_call(
        functools.partial(_inproj_kernel, rope=rope, seq_tiles=seq_tiles),
        grid=(rows // tm,),
        in_specs=in_specs,
        out_specs=[
            pl.BlockSpec((tm, N_MAIN), lambda i: (i, 0)),
            pl.BlockSpec((tm, LANE), lambda i: (i, 0)),
        ],
        out_shape=[
            jax.ShapeDtypeStruct((rows, N_MAIN), BF16),
            jax.ShapeDtypeStruct((rows, LANE), F32),
        ],
        scratch_shapes=[pltpu.VMEM((tm + 2 * HALO, XBC_W), F32)],
        compiler_params=_params("parallel"),
        name="inproj",
    )(*args)


def _attn_kernel(*refs, with_latent, one_minus_lam_init, n_ctx, chunks):
    if with_latent:
        (q_ref, kc_ref, vc_ref, kl_ref, vl_ref, lam_ref, sw_ref, o_ref,
         k_scr, v_scr, m_scr, acc_scr) = refs
    else:
        q_ref, kc_ref, vc_ref, lam_ref, sw_ref, o_ref, k_scr, v_scr, m_scr, acc_scr = refs
    tq = q_ref.shape[0]

    @pl.when(pl.program_id(2) == 0)
    def _():
        def put(k_ref, v_ref, off):
            rows = k_ref.shape[0]
            k = k_ref[...]
            lane = lax.broadcasted_iota(jnp.int32, k.shape, 1)
            zero = jnp.zeros_like(k)
            k_scr[0, off:off + rows, :] = jnp.where(lane < ATT_QK, k, zero)
            k_scr[1, off:off + rows, :] = jnp.where(lane >= ATT_QK, k, zero)
            v_scr[off:off + rows, 0:ATT_V] = v_ref[...]
            v_scr[off:off + rows, ATT_V:2 * ATT_V] = jnp.ones((rows, ATT_V), BF16)
        put(kc_ref, vc_ref, 0)
        if with_latent:
            put(kl_ref, vl_ref, n_ctx)

    m_scr[...] = jnp.full(m_scr.shape, -jnp.inf, F32)
    acc_scr[...] = jnp.zeros(acc_scr.shape, F32)
    q = q_ref[...]

    def chunk(off, size):
        v = v_scr[pl.ds(off, size), :]
        nt = size // LANE
        for mi in range(2):
            s = _dot_nt(q, k_scr[mi, pl.ds(off, size), :])
            tiles = [s[:, t * LANE:(t + 1) * LANE] for t in range(nt)]
            mx = functools.reduce(jnp.maximum, tiles)
            m_prev = m_scr[mi]
            m_new = jnp.maximum(m_prev, jnp.max(mx, axis=-1, keepdims=True))
            p = jnp.concatenate([jnp.exp2(t - m_new) for t in tiles], axis=1).astype(BF16)
            alpha = jnp.exp2(m_prev - m_new)
            acc_scr[mi] = jnp.concatenate([alpha, alpha], axis=1) * acc_scr[mi] + _dot(p, v)
            m_scr[mi] = m_new

    for off, size in chunks:
        chunk(off, size)

    lam = lam_ref[0:1, :]
    a0, a1 = acc_scr[0], acc_scr[1]
    o = a0[:, 0:ATT_V] / a0[:, ATT_V:] - lam * (a1[:, 0:ATT_V] / a1[:, ATT_V:])
    y = o * lax.rsqrt(jnp.mean(o * o, axis=-1, keepdims=True) + EPS) * sw_ref[...]
    o_ref[...] = (y * one_minus_lam_init).astype(o_ref.dtype)


def _attention(qkv_q, qkv_c, qkv_l, lam_l, subln_w, lam_init, batch, nq_len, nc_len, nl_len, tq, tk):
    with_latent = qkv_l is not None
    nq = nq_len // tq
    hq, hk, hv = (c * (CB // LANE) for c in (COL_Q, COL_K, COL_V))
    in_specs = [
        pl.BlockSpec((tq, LANE), lambda b, h, i: (b * nq + i, hq + h)),
        pl.BlockSpec((nc_len, LANE), lambda b, h, i: (b, hk + h)),
        pl.BlockSpec((nc_len, LANE), lambda b, h, i: (b, hv + h)),
    ]
    args = [qkv_q, qkv_c, qkv_c]
    if with_latent:
        in_specs += [
            pl.BlockSpec((nl_len, LANE), lambda b, h, i: (b, hk + h)),
            pl.BlockSpec((nl_len, LANE), lambda b, h, i: (b, hv + h)),
        ]
        args += [qkv_l, qkv_l]
    in_specs += [
        pl.BlockSpec((SUBLANE, LANE), lambda b, h, i: (0, 0)),
        pl.BlockSpec((1, ATT_V), lambda b, h, i: (0, 0)),
    ]
    args += [lam_l, subln_w]
    n_keys = nc_len + nl_len
    sizes = [nc_len + tk] + [tk] * (nl_len // tk - 1) if with_latent else [nc_len]
    chunks = tuple((sum(sizes[:t]), sizes[t]) for t in range(len(sizes)))
    assert sum(sizes) == n_keys
    return pl.pallas_call(
        functools.partial(_attn_kernel, with_latent=with_latent,
                          one_minus_lam_init=1.0 - lam_init, n_ctx=nc_len, chunks=chunks),
        grid=(batch, ATT_HEADS, nq),
        in_specs=in_specs,
        out_specs=pl.BlockSpec((tq, ATT_V), lambda b, h, i: (b * nq + i, h)),
        out_shape=jax.ShapeDtypeStruct((batch * nq_len, ATT_HEADS * ATT_V), BF16),
        scratch_shapes=[
            pltpu.VMEM((2, n_keys, LANE), BF16),
            pltpu.VMEM((n_keys, 2 * ATT_V), BF16),
            pltpu.VMEM((2, tq, LANE), F32),
            pltpu.VMEM((2, tq, 2 * ATT_V), F32),
        ],
        compiler_params=_params("parallel", "parallel", "arbitrary"),
        name="diff_attention",
    )(*args)


def _fourier_kernel(x_ref, xb_ref, xn_ref, xm_ref, rev_ref, cs_ref, cc_ref, sc_ref, o_ref, acc_ref,
                    *, scale):
    k = pl.program_id(1)
    xn = jnp.where(k == 0, jnp.zeros_like(xn_ref[...]), xn_ref[...])
    pad = jnp.zeros((LANE - xn.shape[0], xn.shape[1]), BF16)
    xr = _dot(rev_ref[...], jnp.concatenate([xb_ref[...], xn, pad], axis=0)).astype(BF16)

    @pl.when(k == 0)
    def _():
        mid = _dot(xm_ref[...], cc_ref[0:F_W, :])[0:1, :]
        row = lax.broadcasted_iota(jnp.int32, acc_ref.shape, 0)
        acc_ref[...] = jnp.where(row % 2 == 0, mid, -mid)

    xx = jnp.concatenate([x_ref[...], xr], axis=1)
    ue = _dot(xx, cc_ref[...]).astype(BF16)
    wo = _dot(xx, sc_ref[...]).astype(BF16)
    acc_ref[...] += _dot(cs_ref[...], jnp.concatenate([ue, wo], axis=0))

    @pl.when(k == pl.num_programs(1) - 1)
    def _():
        o_ref[...] = (acc_ref[...] * scale).astype(o_ref.dtype)


DFT_SPLIT = 64


def _dft_tables(n, cols=None):
    cols = jnp.arange(n, dtype=jnp.int32) if cols is None else cols
    m = cols.shape[0]

    def cos_sin(rows):
        ang = ((rows[:, None] * cols[None, :]) % n).astype(F32) * (2.0 * math.pi / n)
        return jnp.cos(ang), jnp.sin(ang)

    if n <= DFT_SPLIT * DFT_SPLIT // 4 or n % DFT_SPLIT:
        return cos_sin(jnp.arange(n, dtype=jnp.int32))
    ca, sa = cos_sin(jnp.arange(n // DFT_SPLIT, dtype=jnp.int32) * DFT_SPLIT)
    cb, sb = cos_sin(jnp.arange(DFT_SPLIT, dtype=jnp.int32))
    ca, sa, cb, sb = ca[:, None, :], sa[:, None, :], cb[None, :, :], sb[None, :, :]
    return (ca * cb - sa * sb).reshape(n, m), (sa * cb + ca * sb).reshape(n, m)


def _seq_dft_table(n, tk):
    col = jnp.arange(n, dtype=jnp.int32)
    k = (col // (2 * tk)) * tk + col % tk
    cn, sn = _dft_tables(n, k)
    return jnp.where(((col // tk) % 2 == 0)[None, :], cn, -sn).astype(BF16)


def _fourier(p, cs, ccd2, scd2, batch, n, tk):
    half = n // 2
    nk, nkh = n // tk, half // tk
    scale = 1.0 / math.sqrt(n * F_GROUP_W)
    sb = SUBLANE_BF16
    rev = np.zeros((tk, tk + LANE), np.float32)
    rev[np.arange(1, tk), tk - np.arange(1, tk)] = 1.0
    rev[0, tk] = 1.0
    seq_blks, tk_blks = n // sb, tk // sb
    mid_blk = half // sb
    return pl.pallas_call(
        functools.partial(_fourier_kernel, scale=scale),
        grid=(batch, nkh),
        in_specs=[
            pl.BlockSpec((tk, CB), lambda b, k: (b * nk + k, COL_FU)),
            pl.BlockSpec((tk, CB), lambda b, k: (b * nk + nk - 1 - k, COL_FU)),
            pl.BlockSpec((sb, CB), lambda b, k: (b * seq_blks + jnp.minimum((nk - k) * tk_blks,
                                                                            seq_blks - 1), COL_FU)),
            pl.BlockSpec((sb, CB), lambda b, k: (b * seq_blks + mid_blk, COL_FU)),
            pl.BlockSpec((tk, tk + LANE), lambda b, k: (0, 0)),
            pl.BlockSpec((n, 2 * tk), lambda b, k: (0, k)),
            pl.BlockSpec((2 * F_W, F_W), lambda b, k: (0, 0)),
            pl.BlockSpec((2 * F_W, F_W), lambda b, k: (0, 0)),
        ],
        out_specs=pl.BlockSpec((n, F_W), lambda b, k: (b, 0)),
        out_shape=jax.ShapeDtypeStruct((batch * n, F_W), BF16),
        scratch_shapes=[pltpu.VMEM((n, F_W), F32)],
        compiler_params=_params("parallel", "arbitrary"),
        name="fourier_mix",
    )(p, p, p, p, jnp.asarray(rev, BF16), cs, ccd2, scd2)


def _softplus(x):
    return jnp.maximum(x, 0.0) + jnp.log1p(jnp.exp(-jnp.abs(x)))


def _ssd_kernel(xsf_ref, bcf_ref, dtf_ref, xsb_ref, bcb_ref, dtb_ref, alog_ref, dtbias_ref,
                e_ref, init_ref, yf_ref, yb_ref, fin_ref, state_ref):
    c = pl.program_id(1)
    q = CHUNK
    cps = xsf_ref.shape[0] // q
    gw = D_INNER // SSD_GROUPS
    hpg = SSD_HEADS // SSD_GROUPS

    @pl.when(c == 0)
    def _():
        state_ref[...] = init_ref[0]

    row = lax.broadcasted_iota(jnp.int32, (q, q), 0)
    col = lax.broadcasted_iota(jnp.int32, (q, q), 1)
    head_of_lane = lax.broadcasted_iota(jnp.int32, (q, gw), 1) // SSD_P
    neg_a = -jnp.exp(alog_ref[...])
    dtbias = dtbias_ref[...]

    def expand(x, e2):
        x1 = x.astype(BF16)
        x2 = (x - x1.astype(F32)).astype(BF16)
        return _dot(jnp.concatenate([x1, x2], axis=1), e2)

    tris = (row >= col, row <= col)
    tri_bs = tuple(t.astype(F32).astype(BF16) for t in tris)
    tri_b2s = tuple(jnp.concatenate([t, t], axis=1) for t in tri_bs)

    def one_chunk(d, rows, xs_ref, bc_ref, dt_ref, y_ref):
        fwd = d == 0
        tri, tri_b = tris[d], tri_bs[d]
        dt = _softplus(dt_ref[rows, :] + dtbias)
        a1, a2, a3 = _split3(dt * neg_a)
        acs = (_dot(tri_b2s[d], jnp.concatenate([a1, a2], axis=0))
               + _dot(tri_b, a3))
        acs_t = acs.T
        last = q - 1 if fwd else 0
        tot_row = acs[last:last + 1, :]
        e2 = e_ref[d]
        dt_w = expand(dt, e2)
        eacs_w = expand(jnp.exp(acs), e2)
        dec_w = expand(jnp.exp(tot_row - acs), e2)
        sdec = _dot_sel_r(jnp.broadcast_to(jnp.exp(tot_row), (SUBLANE, LANE)), e2[0:LANE])[0:1, :]
        xd = xs_ref[rows, :].astype(F32) * dt_w
        xdb = xd.astype(BF16)
        xdd = (xd * dec_w).astype(BF16)
        bc = bc_ref[rows, :]
        for g in range(SSD_GROUPS):
            bg = bc[:, g * D_STATE:(g + 1) * D_STATE]
            cg = bc[:, (SSD_GROUPS + g) * D_STATE:(SSD_GROUPS + g + 1) * D_STATE]
            cb = _dot_nt(cg, bg)
            lanes = slice(g * gw, (g + 1) * gw)
            s_g = state_ref[d, :, lanes]
            y = _dot(cg, s_g.astype(BF16)) * eacs_w[:, lanes]
            xg = xdb[:, lanes]
            zero = jnp.zeros_like(xg)
            for r in range(0, hpg, 2):
                mats, xms = [], []
                for rr in (r, r + 1):
                    jl = d * SSD_HEADS + g * hpg + rr
                    seg = jnp.where(tri, acs[:, jl:jl + 1] - acs_t[jl:jl + 1, :], -jnp.inf)
                    mats.append((cb * jnp.exp(seg)).astype(BF16))
                    xms.append(jnp.where(head_of_lane == rr, xg, zero))
                y = y + _dot(jnp.concatenate(mats, axis=1), jnp.concatenate(xms, axis=0))
            y_ref[rows, lanes] = y.astype(y_ref.dtype)
            bg_t = bg.astype(F32).T.astype(BF16)
            state_ref[d, :, lanes] = s_g * sdec[:, lanes] + _dot(bg_t, xdd[:, lanes])

    for i in range(cps):
        one_chunk(0, pl.ds(i * q, q), xsf_ref, bcf_ref, dtf_ref, yf_ref)
        one_chunk(1, pl.ds((cps - 1 - i) * q, q), xsb_ref, bcb_ref, dtb_ref, yb_ref)

    @pl.when(c == pl.num_programs(1) - 1)
    def _():
        fin_ref[0] = state_ref[...]


def _ssd(act, dt, alog, dtbias, emat2, init, batch, n, cps):
    nc = n // (CHUNK * cps)
    tr = CHUNK * cps
    fidx = lambda b, c: b * nc + c
    bidx = lambda b, c: b * nc + (nc - 1 - c)
    st_shape = (2, D_STATE, D_INNER)
    return pl.pallas_call(
        _ssd_kernel,
        grid=(batch, nc),
        in_specs=[
            pl.BlockSpec((tr, CB), lambda b, c: (fidx(b, c), 0)),
            pl.BlockSpec((tr, CB), lambda b, c: (fidx(b, c), 1)),
            pl.BlockSpec((tr, LANE), lambda b, c: (fidx(b, c), 0)),
            pl.BlockSpec((tr, CB), lambda b, c: (bidx(b, c), 0)),
            pl.BlockSpec((tr, CB), lambda b, c: (bidx(b, c), 1)),
            pl.BlockSpec((tr, LANE), lambda b, c: (bidx(b, c), 0)),
            pl.BlockSpec((1, LANE), lambda b, c: (0, 0)),
            pl.BlockSpec((1, LANE), lambda b, c: (0, 0)),
            pl.BlockSpec((2, 2 * LANE, D_INNER), lambda b, c: (0, 0, 0)),
            pl.BlockSpec((1,) + st_shape, lambda b, c: (b, 0, 0, 0)),
        ],
        out_specs=[
            pl.BlockSpec((tr, D_INNER), lambda b, c: (fidx(b, c), 0)),
            pl.BlockSpec((tr, D_INNER), lambda b, c: (bidx(b, c), 0)),
            pl.BlockSpec((1,) + st_shape, lambda b, c: (b, 0, 0, 0)),
        ],
        out_shape=[
            jax.ShapeDtypeStruct((batch * n, D_INNER), BF16),
            jax.ShapeDtypeStruct((batch * n, D_INNER), BF16),
            jax.ShapeDtypeStruct((batch,) + st_shape, F32),
        ],
        scratch_shapes=[pltpu.VMEM(st_shape, F32)],
        compiler_params=_params("parallel", "arbitrary"),
        name="ssd_scan",
    )(act, act, dt, act, act, dt, alog, dtbias, emat2, init)


def _merge_kernel(four_ref, att_ref, yf_ref, yb_ref, xs_ref, fg_ref, ag_ref, z_ref,
                  g0_ref, g1_ref, g2_ref, x_ref, mod_ref, dskip_ref, snw_ref,
                  wof_ref, woa_ref, wos_ref, wout_ref, nf_ref, o_ref, *, final_norm):
    f = lambda ref: ref[...].astype(F32)
    y_f = _dot((f(four_ref) * f(fg_ref)).astype(BF16), wof_ref[...])
    y_a = _dot((f(att_ref) * f(ag_ref)).astype(BF16), woa_ref[...])
    ys = f(yf_ref) + f(yb_ref) + dskip_ref[...] * f(xs_ref)
    t = ys * f(z_ref)
    t = t * lax.rsqrt(jnp.mean(t * t, axis=-1, keepdims=True) + EPS) * snw_ref[...]
    y_s = _dot(t.astype(BF16), wos_ref[...])
    y = f(g0_ref) * y_f + f(g1_ref) * y_a + f(g2_ref) * y_s
    out = _dot(y.astype(BF16), wout_ref[...])
    xn = x_ref[...] + mod_ref[0][:, 2 * D_MODEL:3 * D_MODEL] * out
    if final_norm:
        xn = xn * lax.rsqrt(jnp.mean(xn * xn, axis=-1, keepdims=True) + EPS) * nf_ref[...]
    o_ref[...] = xn


def _merge(four, att, yf, yb, p, x2, mod, dskip_w, snw, wof, woa, wos, wout, norm_f,
           n, tm, final_norm):
    rows = x2.shape[0]
    tiles_per_b = n // tm
    gcol = COL_GATES * CB // D_MODEL
    row_blk = lambda w, cidx: pl.BlockSpec((tm, w), lambda i: (i, cidx))
    const = lambda shape: pl.BlockSpec(shape, lambda i: (0,) * len(shape))
    return pl.pallas_call(
        functools.partial(_merge_kernel, final_norm=final_norm),
        grid=(rows // tm,),
        in_specs=[
            row_blk(F_W, 0), row_blk(CB, 0), row_blk(D_INNER, 0), row_blk(D_INNER, 0),
            row_blk(CB, 0),
            row_blk(CB, COL_FG), row_blk(CB, COL_AG), row_blk(CB, COL_Z),
            row_blk(D_MODEL, gcol), row_blk(D_MODEL, gcol + 1), row_blk(D_MODEL, gcol + 2),
            row_blk(D_MODEL, 0),
            pl.BlockSpec((1, 1, 3 * D_MODEL), lambda i: (i // tiles_per_b, 0, 0)),
            const((1, D_INNER)), const((1, D_INNER)),
            const((F_W, D_MODEL)), const((CB, D_MODEL)), const((D_INNER, D_MODEL)),
            const((D_MODEL, D_MODEL)), const((1, D_MODEL)),
        ],
        out_specs=pl.BlockSpec((tm, D_MODEL), lambda i: (i, 0)),
        out_shape=jax.ShapeDtypeStruct((rows, D_MODEL), F32),
        compiler_params=_params("parallel"),
        name="branch_merge",
    )(four, att, yf, yb, p, p, p, p, p, p, p, x2, mod, dskip_w, snw, wof, woa, wos, wout, norm_f)


def _rope_tables(n):
    rows = n // GRID_W
    row = jnp.repeat(jnp.arange(rows, dtype=F32), GRID_W)
    col = jnp.tile(jnp.arange(GRID_W, dtype=F32), rows)
    freqs = ROPE_BASE ** (-jnp.arange(0, ROPE_AXIS, 2, dtype=F32) / ROPE_AXIS)
    ang_r = row[:, None] * freqs
    ang_c = col[:, None] * freqs
    ang = jnp.concatenate([ang_r, ang_r, ang_c, ang_c], axis=-1)
    reps = CB // ATT_QK
    return jnp.tile(jnp.cos(ang), (1, reps)), jnp.tile(jnp.sin(ang), (1, reps))


def _head_expanders():
    e = np.zeros((2, LANE, D_INNER), np.float32)
    for d in range(2):
        for h in range(SSD_HEADS):
            e[d, d * SSD_HEADS + h, h * SSD_P:(h + 1) * SSD_P] = 1.0
    return jnp.asarray(np.concatenate([e, e], axis=1), BF16)


def _group_dft(w):
    cw, sw = _dft_tables(F_GROUP_W)
    eye = jnp.eye(w // F_GROUP_W, dtype=F32)
    cc, sc = jnp.kron(eye, cw), jnp.kron(eye, sw)
    return (jnp.concatenate([cc, cc], axis=0).astype(BF16),
            jnp.concatenate([sc, -sc], axis=0).astype(BF16))


def kernel(x, c, ctx, c_ctx, w_mod, b_mod, norm_w, w_in, conv_w, conv_b, a_log, dt_bias, d_skip,
           ssd_norm_w, lam, subln_w, w_of, w_oa, w_os, w_out, norm_f):
    batch, n, _ = x.shape
    n_ctx = ctx.shape[1]
    depth = w_mod.shape[0]
    assert n % GRID_W == 0 and n % CHUNK == 0 and n_ctx % CHUNK == 0

    o_q = 2 * F_W
    o_ag = o_q + 3 * CB
    o_xbc = o_ag + CB + D_INNER
    o_dt = o_xbc + XBC_W
    o_gt = o_dt + DT_W
    w_main = jnp.concatenate([w_in[:, :, o_xbc:o_dt], w_in[:, :, o_gt:], w_in[:, :, :o_q],
                              w_in[:, :, o_ag:o_xbc], w_in[:, :, o_q:o_ag]], axis=-1).astype(BF16)
    w_dt = jnp.pad(w_in[:, :, o_dt:o_gt], ((0, 0), (0, 0), (0, LANE - DT_W))).astype(BF16)
    conv_w8 = jnp.pad(conv_w, ((0, 0), (0, SUBLANE - CONV_W), (0, 0)))
    pad_lanes = lambda a: jnp.pad(a.reshape(depth, 1, DT_W), ((0, 0), (0, 0), (0, LANE - DT_W)))
    alog_p, dtbias_p = pad_lanes(a_log), pad_lanes(dt_bias)
    dskip_w = jnp.repeat(d_skip, SSD_P, axis=-1).reshape(depth, 1, D_INNER)
    wof_b, woa_b, wos_b, wout_b = (w.astype(BF16) for w in (w_of, w_oa, w_os, w_out))
    lam_inits = [0.8 - 0.6 * math.exp(-0.3 * l) for l in range(depth)]
    linit = jnp.asarray(np.broadcast_to(np.asarray(lam_inits, np.float32)[:, None, None],
                                        (depth, SUBLANE, LANE)))

    cos_t, sin_t = _rope_tables(n)
    tf_l, tf_c = math.gcd(n // 2, TK_FOURIER), math.gcd(n_ctx // 2, TK_FOURIER)
    cs_l = _seq_dft_table(n, tf_l)
    cs_c = _seq_dft_table(n_ctx, tf_c)
    ccd, scd = _group_dft(F_W)
    emat = _head_expanders()
    zero_state = jnp.zeros((batch, 2, D_STATE, D_INNER), F32)

    mod_rows = -(-(batch + 1) // SUBLANE) * SUBLANE
    cc = jnp.concatenate([c, c_ctx[None, :], jnp.zeros((mod_rows - batch - 1, D_MODEL), F32)], axis=0)
    mod_all, lam_all = _modulation(cc, w_mod, b_mod.reshape(depth, 1, 3 * D_MODEL), lam, linit)

    tm_l, tm_c = math.gcd(n, TM_PROJ), math.gcd(n_ctx, TM_PROJ)
    tq_l, tk_l = math.gcd(n, TQ_ATT), math.gcd(n, TK_ATT)
    tmerge_l = math.gcd(n, TM_MERGE)
    rows_c = batch * n_ctx
    cps_l = math.gcd(n // CHUNK, SSD_CHUNKS_PER_STEP)
    cps_c = math.gcd(n_ctx // CHUNK, SSD_CHUNKS_PER_STEP)
    xl = x.reshape(batch * n, D_MODEL)
    xc = ctx.reshape(batch * n_ctx, D_MODEL)
    for l in range(depth):
        last = l == depth - 1
        mod_l = mod_all[l, :batch].reshape(batch, 1, 3 * D_MODEL)
        mod_c = mod_all[l, batch].reshape(1, 1, 3 * D_MODEL)
        nw = norm_w[l].reshape(1, D_MODEL)
        cb_ = conv_b[l].reshape(1, XBC_W)
        p_c, dt_c = _inproj(xc, mod_c, nw, w_main[l], w_dt[l], conv_w8[l], cb_, None, None,
                            rows_c, n_ctx, tm_c)
        p_l, dt_l = _inproj(xl, mod_l, nw, w_main[l], w_dt[l], conv_w8[l], cb_, cos_t, sin_t,
                            n, n, tm_l)
        yf_c, yb_c, st_c = _ssd(p_c, dt_c, alog_p[l], dtbias_p[l], emat, zero_state, batch, n_ctx,
                                cps_c)
        yf_l, yb_l, _ = _ssd(p_l, dt_l, alog_p[l], dtbias_p[l], emat, st_c, batch, n, cps_l)
        sw = subln_w[l].reshape(1, ATT_V)
        att_l = _attention(p_l, p_c, p_l, lam_all[l], sw, lam_inits[l], batch, n, n_ctx, n,
                           tq_l, tk_l)
        four_l = _fourier(p_l, cs_l, ccd, scd, batch, n, tf_l)
        merge_w = (dskip_w[l], ssd_norm_w[l].reshape(1, D_INNER), wof_b[l], woa_b[l], wos_b[l],
                   wout_b[l], norm_f.reshape(1, D_MODEL))
        if not last:
            att_c = _attention(p_c, p_c, None, lam_all[l], sw, lam_inits[l], batch, n_ctx, n_ctx, 0,
                               n_ctx, 0)
            four_c = _fourier(p_c, cs_c, ccd, scd, batch, n_ctx, tf_c)
            xc = _merge(four_c, att_c, yf_c, yb_c, p_c, xc, mod_c, *merge_w, rows_c, n_ctx, False)
        xl = _merge(four_l, att_l, yf_l, yb_l, p_l, xl, mod_l, *merge_w, n, tmerge_l, last)
    return xl.reshape(batch, n, D_MODEL)
```

```python
import functools
import math

import numpy as np
import jax
import jax.numpy as jnp
from jax import lax
from jax.experimental import pallas as pl
from jax.experimental.pallas import tpu as pltpu

F32 = jnp.float32
BF16 = jnp.bfloat16
EPS = 1e-6

D_MODEL = 1024
GRID_W = 64
F_GROUP_W = 128
F_W = 512
ATT_HEADS = 4
ATT_QK = 64
ATT_V = 128
ROPE_AXIS = 32
ROPE_BASE = 10000.0
D_INNER = 512
SSD_P = 64
SSD_HEADS = 8
SSD_GROUPS = 2
D_STATE = 128
CONV_W = 5
CHUNK = 128
XBC_W = 1024
DT_W = 16

LANE = 128
SUBLANE = 8
SUBLANE_BF16 = 16
VMEM_LIMIT = 56 * 1024 * 1024

TM_PROJ = 512
TQ_ATT = 1024
TK_ATT = 1024
TK_FOURIER = 512
TM_MERGE = 512
SSD_CHUNKS_PER_STEP = 8

CB = 512
COL_XBC, COL_GATES, COL_FU, COL_FG, COL_AG, COL_Z = 0, 2, 8, 9, 10, 11
COL_Q, COL_K, COL_V = 12, 13, 14
N_COLB = 15
N_MAIN = N_COLB * CB
PROJ_STEP_B = 3
PROJ_STEPS = N_COLB // PROJ_STEP_B
assert PROJ_STEPS * PROJ_STEP_B == N_COLB
Q_SCALE = ATT_QK ** -0.5 * math.log2(math.e)


def _dot(a, b):
    return jnp.dot(a, b, preferred_element_type=F32)


def _dot_nt(a, b):
    return lax.dot_general(a, b, (((1,), (1,)), ((), ())), preferred_element_type=F32)


def _split3(x):
    x1 = x.astype(BF16)
    r1 = x - x1.astype(F32)
    x2 = r1.astype(BF16)
    x3 = (r1 - x2.astype(F32)).astype(BF16)
    return x1, x2, x3


def _dot_sel_r(x, sel):
    x1, x2, x3 = _split3(x)
    return _dot(x1, sel) + _dot(x2, sel) + _dot(x3, sel)


def _sigmoid(x):
    return 0.5 * jnp.tanh(0.5 * x) + 0.5


def _silu(x):
    return x * _sigmoid(x)


def _params(*sem):
    return pltpu.CompilerParams(dimension_semantics=sem, vmem_limit_bytes=VMEM_LIMIT)


def _mod_kernel(cc_ref, w_ref, b_ref, lam_ref, linit_ref, mod_ref, lam_out_ref):
    s = _silu(cc_ref[...])
    mod_ref[0] = jnp.dot(s, w_ref[0], precision=lax.Precision.HIGHEST,
                         preferred_element_type=F32) + b_ref[0]
    lp = lam_ref[0]
    s1 = jnp.sum(lp[0:1] * lp[1:2], axis=-1, keepdims=True)
    s2 = jnp.sum(lp[2:3] * lp[3:4], axis=-1, keepdims=True)
    lam_out_ref[0] = jnp.broadcast_to(jnp.exp(s1) - jnp.exp(s2), (SUBLANE, LANE)) + linit_ref[0]


def _modulation(cc, w_mod, b_mod, lam, linit):
    depth = w_mod.shape[0]
    rows = cc.shape[0]
    tn = D_MODEL
    return pl.pallas_call(
        _mod_kernel,
        grid=(depth, 3 * D_MODEL // tn),
        in_specs=[
            pl.BlockSpec((rows, D_MODEL), lambda l, j: (0, 0)),
            pl.BlockSpec((1, D_MODEL, tn), lambda l, j: (l, 0, j)),
            pl.BlockSpec((1, 1, tn), lambda l, j: (l, 0, j)),
            pl.BlockSpec((1, 4, ATT_QK), lambda l, j: (l, 0, 0)),
            pl.BlockSpec((1, SUBLANE, LANE), lambda l, j: (l, 0, 0)),
        ],
        out_specs=[
            pl.BlockSpec((1, rows, tn), lambda l, j: (l, 0, j)),
            pl.BlockSpec((1, SUBLANE, LANE), lambda l, j: (l, 0, 0)),
        ],
        out_shape=[
            jax.ShapeDtypeStruct((depth, rows, 3 * D_MODEL), F32),
            jax.ShapeDtypeStruct((depth, SUBLANE, LANE), F32),
        ],
        compiler_params=_params("arbitrary", "arbitrary"),
        name="adaln_mod",
    )(cc, w_mod, b_mod, lam, linit)


def _rope(t, cos, sin):
    w = t.shape[-1]
    lane = lax.broadcasted_iota(jnp.int32, t.shape, 1)
    first = (lane % ROPE_AXIS) < (ROPE_AXIS // 2)
    rot = jnp.where(first, -pltpu.roll(t, w - ROPE_AXIS // 2, 1), pltpu.roll(t, ROPE_AXIS // 2, 1))
    return t * cos + rot * sin


HALO = SUBLANE


def _inproj_kernel(*refs, rope, seq_tiles):
    if rope:
        (x_ref, xp_ref, xn_ref, mod_ref, nw_ref, w_ref, wdt_ref, cw_ref, cb_ref, cos_ref, sin_ref,
         p_ref, dt_ref, ext_ref) = refs
    else:
        (x_ref, xp_ref, xn_ref, mod_ref, nw_ref, w_ref, wdt_ref, cw_ref, cb_ref,
         p_ref, dt_ref, ext_ref) = refs
    tm = x_ref.shape[0]
    t = pl.program_id(0) % seq_tiles
    m = mod_ref[0]

    def norm_mod(x):
        y = x * lax.rsqrt(jnp.mean(x * x, axis=-1, keepdims=True) + EPS) * nw_ref[...]
        return (y * (1.0 + m[:, D_MODEL:2 * D_MODEL]) + m[:, 0:D_MODEL]).astype(BF16)

    hb = norm_mod(x_ref[...])
    dt_ref[...] = _dot(hb, wdt_ref[...])
    halo = _dot(norm_mod(jnp.concatenate([xp_ref[...], xn_ref[...]], axis=0)), w_ref[:, 0:XBC_W])
    ext_ref[0:HALO, :] = jnp.where(t == 0, 0.0, halo[0:HALO])
    ext_ref[HALO + tm:2 * HALO + tm, :] = jnp.where(t == seq_tiles - 1, 0.0, halo[HALO:2 * HALO])
    ws = PROJ_STEP_B * CB
    for j in range(PROJ_STEPS):
        acc = _dot(hb, w_ref[:, j * ws:(j + 1) * ws])
        for b in range(PROJ_STEP_B):
            blk = j * PROJ_STEP_B + b
            t = acc[:, b * CB:(b + 1) * CB]
            if blk * CB < XBC_W:
                ext_ref[HALO:HALO + tm, blk * CB:(blk + 1) * CB] = t
                continue
            if COL_GATES <= blk < COL_FU:
                t = _sigmoid(t)
            elif blk in (COL_FG, COL_AG, COL_Z):
                t = _silu(t)
            elif blk in (COL_Q, COL_K) and rope:
                t = _rope(t, cos_ref[...], sin_ref[...])
            if blk == COL_Q:
                t = t * Q_SCALE
            p_ref[:, blk * CB:(blk + 1) * CB] = t.astype(BF16)
    ext_rows = tm + 2 * HALO
    for cols in (slice(c0, c0 + CB) for c0 in range(0, XBC_W, CB)):
        e = ext_ref[:, cols]
        w = [cw_ref[k:k + 1, cols] for k in range(CONV_W)]
        before = pltpu.roll(pltpu.roll(w[0] * e, 1, 0) + w[1] * e, 1, 0)
        after = pltpu.roll(pltpu.roll(w[4] * e, ext_rows - 1, 0) + w[3] * e, ext_rows - 1, 0)
        conv = (before + w[2] * e + after)[HALO:HALO + tm] + cb_ref[:, cols]
        p_ref[:, cols] = _silu(conv).astype(BF16)


def _inproj(x2, mod, norm_w, w_main, w_dt, conv_w8, conv_b, cos, sin, n_mod, n_seq, tm):
    rows = x2.shape[0]
    mod_tiles, seq_tiles = n_mod // tm, n_seq // tm
    hb = tm // HALO
    last_hb = rows // HALO - 1
    rope = cos is not None
    resident = lambda shape: pl.BlockSpec(shape, lambda i: (0, 0), pipeline_mode=pl.Buffered(1))
    in_specs = [
        pl.BlockSpec((tm, D_MODEL), lambda i: (i, 0)),
        pl.BlockSpec((HALO, D_MODEL), lambda i: (jnp.maximum(i * hb - 1, 0), 0)),
        pl.BlockSpec((HALO, D_MODEL), lambda i: (jnp.minimum((i + 1) * hb, last_hb), 0)),
        pl.BlockSpec((1, 1, 3 * D_MODEL), lambda i: (i // mod_tiles, 0, 0)),
        pl.BlockSpec((1, D_MODEL), lambda i: (0, 0)),
        resident((D_MODEL, N_MAIN)),
        resident((D_MODEL, LANE)),
        pl.BlockSpec((SUBLANE, XBC_W), lambda i: (0, 0)),
        pl.BlockSpec((1, XBC_W), lambda i: (0, 0)),
    ]
    args = [x2, x2, x2, mod, norm_w, w_main, w_dt, conv_w8, conv_b]
    if rope:
        in_specs += [pl.BlockSpec((tm, CB), lambda i: (i % seq_tiles, 0))] * 2
        args += [cos, sin]
    return pl.pallas_call(
        functools.partial(_inproj_kernel, rope=rope, seq_tiles=seq_tiles),
        grid=(rows // tm,),
        in_specs=in_specs,
        out_specs=[
            pl.BlockSpec((tm, N_MAIN), lambda i: (i, 0)),
            pl.BlockSpec((tm, LANE), lambda i: (i, 0)),
        ],
        out_shape=[
            jax.ShapeDtypeStruct((rows, N_MAIN), BF16),
            jax.ShapeDtypeStruct((rows, LANE), F32),
        ],
        scratch_shapes=[pltpu.VMEM((tm + 2 * HALO, XBC_W), F32)],
        compiler_params=_params("parallel"),
        name="inproj",
    )(*args)


def _attn_kernel(*refs, with_latent, one_minus_lam_init, n_ctx, chunks):
    if with_latent:
        (q_ref, kc_ref, vc_ref, kl_ref, vl_ref, lam_ref, sw_ref, o_ref,
         k_scr, v_scr, m_scr, acc_scr) = refs
    else:
        q_ref, kc_ref, vc_ref, lam_ref, sw_ref, o_ref, k_scr, v_scr, m_scr, acc_scr = refs
    tq = q_ref.shape[0]

    @pl.when(pl.program_id(2) == 0)
    def _():
        def put(k_ref, v_ref, off):
            rows = k_ref.shape[0]
            k = k_ref[...]
            lane = lax.broadcasted_iota(jnp.int32, k.shape, 1)
            zero = jnp.zeros_like(k)
            k_scr[0, off:off + rows, :] = jnp.where(lane < ATT_QK, k, zero)
            k_scr[1, off:off + rows, :] = jnp.where(lane >= ATT_QK, k, zero)
            v_scr[off:off + rows, 0:ATT_V] = v_ref[...]
            v_scr[off:off + rows, ATT_V:2 * ATT_V] = jnp.ones((rows, ATT_V), BF16)
        put(kc_ref, vc_ref, 0)
        if with_latent:
            put(kl_ref, vl_ref, n_ctx)

    m_scr[...] = jnp.full(m_scr.shape, -jnp.inf, F32)
    acc_scr[...] = jnp.zeros(acc_scr.shape, F32)
    q = q_ref[...]

    def chunk(off, size):
        v = v_scr[pl.ds(off, size), :]
        nt = size // LANE
        for mi in range(2):
            s = _dot_nt(q, k_scr[mi, pl.ds(off, size), :])
            tiles = [s[:, t * LANE:(t + 1) * LANE] for t in range(nt)]
            mx = functools.reduce(jnp.maximum, tiles)
            m_prev = m_scr[mi]
            m_new = jnp.maximum(m_prev, jnp.max(mx, axis=-1, keepdims=True))
            p = jnp.concatenate([jnp.exp2(t - m_new) for t in tiles], axis=1).astype(BF16)
            alpha = jnp.exp2(m_prev - m_new)
            acc_scr[mi] = jnp.concatenate([alpha, alpha], axis=1) * acc_scr[mi] + _dot(p, v)
            m_scr[mi] = m_new

    for off, size in chunks:
        chunk(off, size)

    lam = lam_ref[0:1, :]
    a0, a1 = acc_scr[0], acc_scr[1]
    o = a0[:, 0:ATT_V] / a0[:, ATT_V:] - lam * (a1[:, 0:ATT_V] / a1[:, ATT_V:])
    y = o * lax.rsqrt(jnp.mean(o * o, axis=-1, keepdims=True) + EPS) * sw_ref[...]
    o_ref[...] = (y * one_minus_lam_init).astype(o_ref.dtype)


def _attention(qkv_q, qkv_c, qkv_l, lam_l, subln_w, lam_init, batch, nq_len, nc_len, nl_len, tq, tk):
    with_latent = qkv_l is not None
    nq = nq_len // tq
    hq, hk, hv = (c * (CB // LANE) for c in (COL_Q, COL_K, COL_V))
    in_specs = [
        pl.BlockSpec((tq, LANE), lambda b, h, i: (b * nq + i, hq + h)),
        pl.BlockSpec((nc_len, LANE), lambda b, h, i: (b, hk + h)),
        pl.BlockSpec((nc_len, LANE), lambda b, h, i: (b, hv + h)),
    ]
    args = [qkv_q, qkv_c, qkv_c]
    if with_latent:
        in_specs += [
            pl.BlockSpec((nl_len, LANE), lambda b, h, i: (b, hk + h)),
            pl.BlockSpec((nl_len, LANE), lambda b, h, i: (b, hv + h)),
        ]
        args += [qkv_l, qkv_l]
    in_specs += [
        pl.BlockSpec((SUBLANE, LANE), lambda b, h, i: (0, 0)),
        pl.BlockSpec((1, ATT_V), lambda b, h, i: (0, 0)),
    ]
    args += [lam_l, subln_w]
    n_keys = nc_len + nl_len
    sizes = [nc_len + tk] + [tk] * (nl_len // tk - 1) if with_latent else [nc_len]
    chunks = tuple((sum(sizes[:t]), sizes[t]) for t in range(len(sizes)))
    assert sum(sizes) == n_keys
    return pl.pallas_call(
        functools.partial(_attn_kernel, with_latent=with_latent,
                          one_minus_lam_init=1.0 - lam_init, n_ctx=nc_len, chunks=chunks),
        grid=(batch, ATT_HEADS, nq),
        in_specs=in_specs,
        out_specs=pl.BlockSpec((tq, ATT_V), lambda b, h, i: (b * nq + i, h)),
        out_shape=jax.ShapeDtypeStruct((batch * nq_len, ATT_HEADS * ATT_V), BF16),
        scratch_shapes=[
            pltpu.VMEM((2, n_keys, LANE), BF16),
            pltpu.VMEM((n_keys, 2 * ATT_V), BF16),
            pltpu.VMEM((2, tq, LANE), F32),
            pltpu.VMEM((2, tq, 2 * ATT_V), F32),
        ],
        compiler_params=_params("parallel", "parallel", "arbitrary"),
        name="diff_attention",
    )(*args)


def _fourier_kernel(x_ref, xb_ref, xn_ref, xm_ref, rev_ref, cs_ref, cc_ref, sc_ref, o_ref, acc_ref,
                    *, scale):
    k = pl.program_id(1)
    xn = jnp.where(k == 0, jnp.zeros_like(xn_ref[...]), xn_ref[...])
    pad = jnp.zeros((LANE - xn.shape[0], xn.shape[1]), BF16)
    xr = _dot(rev_ref[...], jnp.concatenate([xb_ref[...], xn, pad], axis=0)).astype(BF16)

    @pl.when(k == 0)
    def _():
        mid = _dot(xm_ref[...], cc_ref[0:F_W, :])[0:1, :]
        row = lax.broadcasted_iota(jnp.int32, acc_ref.shape, 0)
        acc_ref[...] = jnp.where(row % 2 == 0, mid, -mid)

    xx = jnp.concatenate([x_ref[...], xr], axis=1)
    ue = _dot(xx, cc_ref[...]).astype(BF16)
    wo = _dot(xx, sc_ref[...]).astype(BF16)
    acc_ref[...] += _dot(cs_ref[...], jnp.concatenate([ue, wo], axis=0))

    @pl.when(k == pl.num_programs(1) - 1)
    def _():
        o_ref[...] = (acc_ref[...] * scale).astype(o_ref.dtype)


DFT_SPLIT = 64


def _dft_tables(n, cols=None):
    cols = jnp.arange(n, dtype=jnp.int32) if cols is None else cols
    m = cols.shape[0]

    def cos_sin(rows):
        ang = ((rows[:, None] * cols[None, :]) % n).astype(F32) * (2.0 * math.pi / n)
        return jnp.cos(ang), jnp.sin(ang)

    if n <= DFT_SPLIT * DFT_SPLIT // 4 or n % DFT_SPLIT:
        return cos_sin(jnp.arange(n, dtype=jnp.int32))
    ca, sa = cos_sin(jnp.arange(n // DFT_SPLIT, dtype=jnp.int32) * DFT_SPLIT)
    cb, sb = cos_sin(jnp.arange(DFT_SPLIT, dtype=jnp.int32))
    ca, sa, cb, sb = ca[:, None, :], sa[:, None, :], cb[None, :, :], sb[None, :, :]
    return (ca * cb - sa * sb).reshape(n, m), (sa * cb + ca * sb).reshape(n, m)


def _seq_dft_table(n, tk):
    col = jnp.arange(n, dtype=jnp.int32)
    k = (col // (2 * tk)) * tk + col % tk
    cn, sn = _dft_tables(n, k)
    return jnp.where(((col // tk) % 2 == 0)[None, :], cn, -sn).astype(BF16)


def _fourier(p, cs, ccd2, scd2, batch, n, tk):
    half = n // 2
    nk, nkh = n // tk, half // tk
    scale = 1.0 / math.sqrt(n * F_GROUP_W)
    sb = SUBLANE_BF16
    rev = np.zeros((tk, tk + LANE), np.float32)
    rev[np.arange(1, tk), tk - np.arange(1, tk)] = 1.0
    rev[0, tk] = 1.0
    seq_blks, tk_blks = n // sb, tk // sb
    mid_blk = half // sb
    return pl.pallas_call(
        functools.partial(_fourier_kernel, scale=scale),
        grid=(batch, nkh),
        in_specs=[
            pl.BlockSpec((tk, CB), lambda b, k: (b * nk + k, COL_FU)),
            pl.BlockSpec((tk, CB), lambda b, k: (b * nk + nk - 1 - k, COL_FU)),
            pl.BlockSpec((sb, CB), lambda b, k: (b * seq_blks + jnp.minimum((nk - k) * tk_blks,
                                                                            seq_blks - 1), COL_FU)),
            pl.BlockSpec((sb, CB), lambda b, k: (b * seq_blks + mid_blk, COL_FU)),
            pl.BlockSpec((tk, tk + LANE), lambda b, k: (0, 0)),
            pl.BlockSpec((n, 2 * tk), lambda b, k: (0, k)),
            pl.BlockSpec((2 * F_W, F_W), lambda b, k: (0, 0)),
            pl.BlockSpec((2 * F_W, F_W), lambda b, k: (0, 0)),
        ],
        out_specs=pl.BlockSpec((n, F_W), lambda b, k: (b, 0)),
        out_shape=jax.ShapeDtypeStruct((batch * n, F_W), BF16),
        scratch_shapes=[pltpu.VMEM((n, F_W), F32)],
        compiler_params=_params("parallel", "arbitrary"),
        name="fourier_mix",
    )(p, p, p, p, jnp.asarray(rev, BF16), cs, ccd2, scd2)


def _softplus(x):
    return jnp.maximum(x, 0.0) + jnp.log1p(jnp.exp(-jnp.abs(x)))


def _ssd_kernel(xsf_ref, bcf_ref, dtf_ref, xsb_ref, bcb_ref, dtb_ref, alog_ref, dtbias_ref,
                e_ref, init_ref, yf_ref, yb_ref, fin_ref, state_ref):
    c = pl.program_id(1)
    q = CHUNK
    cps = xsf_ref.shape[0] // q
    gw = D_INNER // SSD_GROUPS
    hpg = SSD_HEADS // SSD_GROUPS

    @pl.when(c == 0)
    def _():
        state_ref[...] = init_ref[0]

    row = lax.broadcasted_iota(jnp.int32, (q, q), 0)
    col = lax.broadcasted_iota(jnp.int32, (q, q), 1)
    head_of_lane = lax.broadcasted_iota(jnp.int32, (q, gw), 1) // SSD_P
    neg_a = -jnp.exp(alog_ref[...])
    dtbias = dtbias_ref[...]

    def expand(x, e2):
        x1 = x.astype(BF16)
        x2 = (x - x1.astype(F32)).astype(BF16)
        return _dot(jnp.concatenate([x1, x2], axis=1), e2)

    tris = (row >= col, row <= col)
    tri_bs = tuple(t.astype(F32).astype(BF16) for t in tris)
    tri_b2s = tuple(jnp.concatenate([t, t], axis=1) for t in tri_bs)

    def one_chunk(d, rows, xs_ref, bc_ref, dt_ref, y_ref):
        fwd = d == 0
        tri, tri_b = tris[d], tri_bs[d]
        dt = _softplus(dt_ref[rows, :] + dtbias)
        a1, a2, a3 = _split3(dt * neg_a)
        acs = (_dot(tri_b2s[d], jnp.concatenate([a1, a2], axis=0))
               + _dot(tri_b, a3))
        acs_t = acs.T
        last = q - 1 if fwd else 0
        tot_row = acs[last:last + 1, :]
        e2 = e_ref[d]
        dt_w = expand(dt, e2)
        eacs_w = expand(jnp.exp(acs), e2)
        dec_w = expand(jnp.exp(tot_row - acs), e2)
        sdec = _dot_sel_r(jnp.broadcast_to(jnp.exp(tot_row), (SUBLANE, LANE)), e2[0:LANE])[0:1, :]
        xd = xs_ref[rows, :].astype(F32) * dt_w
        xdb = xd.astype(BF16)
        xdd = (xd * dec_w).astype(BF16)
        bc = bc_ref[rows, :]
        for g in range(SSD_GROUPS):
            bg = bc[:, g * D_STATE:(g + 1) * D_STATE]
            cg = bc[:, (SSD_GROUPS + g) * D_STATE:(SSD_GROUPS + g + 1) * D_STATE]
            cb = _dot_nt(cg, bg)
            lanes = slice(g * gw, (g + 1) * gw)
            s_g = state_ref[d, :, lanes]
            y = _dot(cg, s_g.astype(BF16)) * eacs_w[:, lanes]
            xg = xdb[:, lanes]
            zero = jnp.zeros_like(xg)
            for r in range(0, hpg, 2):
                mats, xms = [], []
                for rr in (r, r + 1):
                    jl = d * SSD_HEADS + g * hpg + rr
                    seg = jnp.where(tri, acs[:, jl:jl + 1] - acs_t[jl:jl + 1, :], -jnp.inf)
                    mats.append((cb * jnp.exp(seg)).astype(BF16))
                    xms.append(jnp.where(head_of_lane == rr, xg, zero))
                y = y + _dot(jnp.concatenate(mats, axis=1), jnp.concatenate(xms, axis=0))
            y_ref[rows, lanes] = y.astype(y_ref.dtype)
            bg_t = bg.astype(F32).T.astype(BF16)
            state_ref[d, :, lanes] = s_g * sdec[:, lanes] + _dot(bg_t, xdd[:, lanes])

    for i in range(cps):
        one_chunk(0, pl.ds(i * q, q), xsf_ref, bcf_ref, dtf_ref, yf_ref)
        one_chunk(1, pl.ds((cps - 1 - i) * q, q), xsb_ref, bcb_ref, dtb_ref, yb_ref)

    @pl.when(c == pl.num_programs(1) - 1)
    def _():
        fin_ref[0] = state_ref[...]


def _ssd(act, dt, alog, dtbias, emat2, init, batch, n, cps):
    nc = n // (CHUNK * cps)
    tr = CHUNK * cps
    fidx = lambda b, c: b * nc + c
    bidx = lambda b, c: b * nc + (nc - 1 - c)
    st_shape = (2, D_STATE, D_INNER)
    return pl.pallas_call(
        _ssd_kernel,
        grid=(batch, nc),
        in_specs=[
            pl.BlockSpec((tr, CB), lambda b, c: (fidx(b, c), 0)),
            pl.BlockSpec((tr, CB), lambda b, c: (fidx(b, c), 1)),
            pl.BlockSpec((tr, LANE), lambda b, c: (fidx(b, c), 0)),
            pl.BlockSpec((tr, CB), lambda b, c: (bidx(b, c), 0)),
            pl.BlockSpec((tr, CB), lambda b, c: (bidx(b, c), 1)),
            pl.BlockSpec((tr, LANE), lambda b, c: (bidx(b, c), 0)),
            pl.BlockSpec((1, LANE), lambda b, c: (0, 0)),
            pl.BlockSpec((1, LANE), lambda b, c: (0, 0)),
            pl.BlockSpec((2, 2 * LANE, D_INNER), lambda b, c: (0, 0, 0)),
            pl.BlockSpec((1,) + st_shape, lambda b, c: (b, 0, 0, 0)),
        ],
        out_specs=[
            pl.BlockSpec((tr, D_INNER), lambda b, c: (fidx(b, c), 0)),
            pl.BlockSpec((tr, D_INNER), lambda b, c: (bidx(b, c), 0)),
            pl.BlockSpec((1,) + st_shape, lambda b, c: (b, 0, 0, 0)),
        ],
        out_shape=[
            jax.ShapeDtypeStruct((batch * n, D_INNER), BF16),
            jax.ShapeDtypeStruct((batch * n, D_INNER), BF16),
            jax.ShapeDtypeStruct((batch,) + st_shape, F32),
        ],
        scratch_shapes=[pltpu.VMEM(st_shape, F32)],
        compiler_params=_params("parallel", "arbitrary"),
        name="ssd_scan",
    )(act, act, dt, act, act, dt, alog, dtbias, emat2, init)


def _merge_kernel(four_ref, att_ref, yf_ref, yb_ref, xs_ref, fg_ref, ag_ref, z_ref,
                  g0_ref, g1_ref, g2_ref, x_ref, mod_ref, dskip_ref, snw_ref,
                  wof_ref, woa_ref, wos_ref, wout_ref, nf_ref, o_ref, *, final_norm):
    f = lambda ref: ref[...].astype(F32)
    y_f = _dot((f(four_ref) * f(fg_ref)).astype(BF16), wof_ref[...])
    y_a = _dot((f(att_ref) * f(ag_ref)).astype(BF16), woa_ref[...])
    ys = f(yf_ref) + f(yb_ref) + dskip_ref[...] * f(xs_ref)
    t = ys * f(z_ref)
    t = t * lax.rsqrt(jnp.mean(t * t, axis=-1, keepdims=True) + EPS) * snw_ref[...]
    y_s = _dot(t.astype(BF16), wos_ref[...])
    y = f(g0_ref) * y_f + f(g1_ref) * y_a + f(g2_ref) * y_s
    out = _dot(y.astype(BF16), wout_ref[...])
    xn = x_ref[...] + mod_ref[0][:, 2 * D_MODEL:3 * D_MODEL] * out
    if final_norm:
        xn = xn * lax.rsqrt(jnp.mean(xn * xn, axis=-1, keepdims=True) + EPS) * nf_ref[...]
    o_ref[...] = xn


def _merge(four, att, yf, yb, p, x2, mod, dskip_w, snw, wof, woa, wos, wout, norm_f,
           n, tm, final_norm):
    rows = x2.shape[0]
    tiles_per_b = n // tm
    gcol = COL_GATES * CB // D_MODEL
    row_blk = lambda w, cidx: pl.BlockSpec((tm, w), lambda i: (i, cidx))
    const = lambda shape: pl.BlockSpec(shape, lambda i: (0,) * len(shape))
    return pl.pallas_call(
        functools.partial(_merge_kernel, final_norm=final_norm),
        grid=(rows // tm,),
        in_specs=[
            row_blk(F_W, 0), row_blk(CB, 0), row_blk(D_INNER, 0), row_blk(D_INNER, 0),
            row_blk(CB, 0),
            row_blk(CB, COL_FG), row_blk(CB, COL_AG), row_blk(CB, COL_Z),
            row_blk(D_MODEL, gcol), row_blk(D_MODEL, gcol + 1), row_blk(D_MODEL, gcol + 2),
            row_blk(D_MODEL, 0),
            pl.BlockSpec((1, 1, 3 * D_MODEL), lambda i: (i // tiles_per_b, 0, 0)),
            const((1, D_INNER)), const((1, D_INNER)),
            const((F_W, D_MODEL)), const((CB, D_MODEL)), const((D_INNER, D_MODEL)),
            const((D_MODEL, D_MODEL)), const((1, D_MODEL)),
        ],
        out_specs=pl.BlockSpec((tm, D_MODEL), lambda i: (i, 0)),
        out_shape=jax.ShapeDtypeStruct((rows, D_MODEL), F32),
        compiler_params=_params("parallel"),
        name="branch_merge",
    )(four, att, yf, yb, p, p, p, p, p, p, p, x2, mod, dskip_w, snw, wof, woa, wos, wout, norm_f)


def _rope_tables(n):
    rows = n // GRID_W
    row = jnp.repeat(jnp.arange(rows, dtype=F32), GRID_W)
    col = jnp.tile(jnp.arange(GRID_W, dtype=F32), rows)
    freqs = ROPE_BASE ** (-jnp.arange(0, ROPE_AXIS, 2, dtype=F32) / ROPE_AXIS)
    ang_r = row[:, None] * freqs
    ang_c = col[:, None] * freqs
    ang = jnp.concatenate([ang_r, ang_r, ang_c, ang_c], axis=-1)
    reps = CB // ATT_QK
    return jnp.tile(jnp.cos(ang), (1, reps)), jnp.tile(jnp.sin(ang), (1, reps))


def _head_expanders():
    e = np.zeros((2, LANE, D_INNER), np.float32)
    for d in range(2):
        for h in range(SSD_HEADS):
            e[d, d * SSD_HEADS + h, h * SSD_P:(h + 1) * SSD_P] = 1.0
    return jnp.asarray(np.concatenate([e, e], axis=1), BF16)


def _group_dft(w):
    cw, sw = _dft_tables(F_GROUP_W)
    eye = jnp.eye(w // F_GROUP_W, dtype=F32)
    cc, sc = jnp.kron(eye, cw), jnp.kron(eye, sw)
    return (jnp.concatenate([cc, cc], axis=0).astype(BF16),
            jnp.concatenate([sc, -sc], axis=0).astype(BF16))


def kernel(x, c, ctx, c_ctx, w_mod, b_mod, norm_w, w_in, conv_w, conv_b, a_log, dt_bias, d_skip,
           ssd_norm_w, lam, subln_w, w_of, w_oa, w_os, w_out, norm_f):
    batch, n, _ = x.shape
    n_ctx = ctx.shape[1]
    depth = w_mod.shape[0]
    assert n % GRID_W == 0 and n % CHUNK == 0 and n_ctx % CHUNK == 0

    o_q = 2 * F_W
    o_ag = o_q + 3 * CB
    o_xbc = o_ag + CB + D_INNER
    o_dt = o_xbc + XBC_W
    o_gt = o_dt + DT_W
    w_main = jnp.concatenate([w_in[:, :, o_xbc:o_dt], w_in[:, :, o_gt:], w_in[:, :, :o_q],
                              w_in[:, :, o_ag:o_xbc], w_in[:, :, o_q:o_ag]], axis=-1).astype(BF16)
    w_dt = jnp.pad(w_in[:, :, o_dt:o_gt], ((0, 0), (0, 0), (0, LANE - DT_W))).astype(BF16)
    conv_w8 = jnp.pad(conv_w, ((0, 0), (0, SUBLANE - CONV_W), (0, 0)))
    pad_lanes = lambda a: jnp.pad(a.reshape(depth, 1, DT_W), ((0, 0), (0, 0), (0, LANE - DT_W)))
    alog_p, dtbias_p = pad_lanes(a_log), pad_lanes(dt_bias)
    dskip_w = jnp.repeat(d_skip, SSD_P, axis=-1).reshape(depth, 1, D_INNER)
    wof_b, woa_b, wos_b, wout_b = (w.astype(BF16) for w in (w_of, w_oa, w_os, w_out))
    lam_inits = [0.8 - 0.6 * math.exp(-0.3 * l) for l in range(depth)]
    linit = jnp.asarray(np.broadcast_to(np.asarray(lam_inits, np.float32)[:, None, None],
                                        (depth, SUBLANE, LANE)))

    cos_t, sin_t = _rope_tables(n)
    tf_l, tf_c = math.gcd(n // 2, TK_FOURIER), math.gcd(n_ctx // 2, TK_FOURIER)
    cs_l = _seq_dft_table(n, tf_l)
    cs_c = _seq_dft_table(n_ctx, tf_c)
    ccd, scd = _group_dft(F_W)
    emat = _head_expanders()
    zero_state = jnp.zeros((batch, 2, D_STATE, D_INNER), F32)

    mod_rows = -(-(batch + 1) // SUBLANE) * SUBLANE
    cc = jnp.concatenate([c, c_ctx[None, :], jnp.zeros((mod_rows - batch - 1, D_MODEL), F32)], axis=0)
    mod_all, lam_all = _modulation(cc, w_mod, b_mod.reshape(depth, 1, 3 * D_MODEL), lam, linit)

    tm_l, tm_c = math.gcd(n, TM_PROJ), math.gcd(n_ctx, TM_PROJ)
    tq_l, tk_l = math.gcd(n, TQ_ATT), math.gcd(n, TK_ATT)
    tmerge_l = math.gcd(n, TM_MERGE)
    rows_c = batch * n_ctx
    cps_l = math.gcd(n // CHUNK, SSD_CHUNKS_PER_STEP)
    cps_c = math.gcd(n_ctx // CHUNK, SSD_CHUNKS_PER_STEP)
    xl = x.reshape(batch * n, D_MODEL)
    xc = ctx.reshape(batch * n_ctx, D_MODEL)
    for l in range(depth):
        last = l == depth - 1
        mod_l = mod_all[l, :batch].reshape(batch, 1, 3 * D_MODEL)
        mod_c = mod_all[l, batch].reshape(1, 1, 3 * D_MODEL)
        nw = norm_w[l].reshape(1, D_MODEL)
        cb_ = conv_b[l].reshape(1, XBC_W)
        p_c, dt_c = _inproj(xc, mod_c, nw, w_main[l], w_dt[l], conv_w8[l], cb_, None, None,
                            rows_c, n_ctx, tm_c)
        p_l, dt_l = _inproj(xl, mod_l, nw, w_main[l], w_dt[l], conv_w8[l], cb_, cos_t, sin_t,
                            n, n, tm_l)
        yf_c, yb_c, st_c = _ssd(p_c, dt_c, alog_p[l], dtbias_p[l], emat, zero_state, batch, n_ctx,
                                cps_c)
        yf_l, yb_l, _ = _ssd(p_l, dt_l, alog_p[l], dtbias_p[l], emat, st_c, batch, n, cps_l)
        sw = subln_w[l].reshape(1, ATT_V)
        att_l = _attention(p_l, p_c, p_l, lam_all[l], sw, lam_inits[l], batch, n, n_ctx, n,
                           tq_l, tk_l)
        four_l = _fourier(p_l, cs_l, ccd, scd, batch, n, tf_l)
        merge_w = (dskip_w[l], ssd_norm_w[l].reshape(1, D_INNER), wof_b[l], woa_b[l], wos_b[l],
                   wout_b[l], norm_f.reshape(1, D_MODEL))
        if not last:
            att_c = _attention(p_c, p_c, None, lam_all[l], sw, lam_inits[l], batch, n_ctx, n_ctx, 0,
                               n_ctx, 0)
            four_c = _fourier(p_c, cs_c, ccd, scd, batch, n_ctx, tf_c)
            xc = _merge(four_c, att_c, yf_c, yb_c, p_c, xc, mod_c, *merge_w, rows_c, n_ctx, False)
        xl = _merge(four_l, att_l, yf_l, yb_l, p_l, xl, mod_l, *merge_w, n, tmerge_l, last)
    return xl.reshape(batch, n, D_MODEL)
```

```python
import functools
import math

import numpy as np
import jax
import jax.numpy as jnp
from jax import lax
from jax.experimental import pallas as pl
from jax.experimental.pallas import tpu as pltpu

F32 = jnp.float32
BF16 = jnp.bfloat16
EPS = 1e-6

D_MODEL = 1024
GRID_W = 64
F_GROUP_W = 128
F_W = 512
ATT_HEADS = 4
ATT_QK = 64
ATT_V = 128
ROPE_AXIS = 32
ROPE_BASE = 10000.0
D_INNER = 512
SSD_P = 64
SSD_HEADS = 8
SSD_GROUPS = 2
D_STATE = 128
CONV_W = 5
CHUNK = 128
XBC_W = 1024
DT_W = 16

LANE = 128
SUBLANE = 8
SUBLANE_BF16 = 16
VMEM_LIMIT = 56 * 1024 * 1024

TM_PROJ = 512
TQ_ATT = 1024
TK_ATT = 1024
TK_FOURIER = 512
TM_MERGE = 512
SSD_CHUNKS_PER_STEP = 8

CB = 512
COL_XBC, COL_GATES, COL_FU, COL_FG, COL_AG, COL_Z = 0, 2, 8, 9, 10, 11
COL_Q, COL_K, COL_V = 12, 13, 14
N_COLB = 15
N_MAIN = N_COLB * CB
PROJ_STEP_B = 3
PROJ_STEPS = N_COLB // PROJ_STEP_B
assert PROJ_STEPS * PROJ_STEP_B == N_COLB
Q_SCALE = ATT_QK ** -0.5 * math.log2(math.e)


def _dot(a, b):
    return jnp.dot(a, b, preferred_element_type=F32)


def _dot_nt(a, b):
    return lax.dot_general(a, b, (((1,), (1,)), ((), ())), preferred_element_type=F32)


def _split3(x):
    x1 = x.astype(BF16)
    r1 = x - x1.astype(F32)
    x2 = r1.astype(BF16)
    x3 = (r1 - x2.astype(F32)).astype(BF16)
    return x1, x2, x3


def _dot_sel_r(x, sel):
    x1, x2, x3 = _split3(x)
    return _dot(x1, sel) + _dot(x2, sel) + _dot(x3, sel)


def _sigmoid(x):
    return 0.5 * jnp.tanh(0.5 * x) + 0.5


def _silu(x):
    return x * _sigmoid(x)


def _params(*sem):
    return pltpu.CompilerParams(dimension_semantics=sem, vmem_limit_bytes=VMEM_LIMIT)


def _mod_kernel(cc_ref, w_ref, b_ref, lam_ref, linit_ref, mod_ref, lam_out_ref):
    s = _silu(cc_ref[...])
    mod_ref[0] = jnp.dot(s, w_ref[0], precision=lax.Precision.HIGHEST,
                         preferred_element_type=F32) + b_ref[0]
    lp = lam_ref[0]
    s1 = jnp.sum(lp[0:1] * lp[1:2], axis=-1, keepdims=True)
    s2 = jnp.sum(lp[2:3] * lp[3:4], axis=-1, keepdims=True)
    lam_out_ref[0] = jnp.broadcast_to(jnp.exp(s1) - jnp.exp(s2), (SUBLANE, LANE)) + linit_ref[0]


def _modulation(cc, w_mod, b_mod, lam, linit):
    depth = w_mod.shape[0]
    rows = cc.shape[0]
    tn = D_MODEL
    return pl.pallas_call(
        _mod_kernel,
        grid=(depth, 3 * D_MODEL // tn),
        in_specs=[
            pl.BlockSpec((rows, D_MODEL), lambda l, j: (0, 0)),
            pl.BlockSpec((1, D_MODEL, tn), lambda l, j: (l, 0, j)),
            pl.BlockSpec((1, 1, tn), lambda l, j: (l, 0, j)),
            pl.BlockSpec((1, 4, ATT_QK), lambda l, j: (l, 0, 0)),
            pl.BlockSpec((1, SUBLANE, LANE), lambda l, j: (l, 0, 0)),
        ],
        out_specs=[
            pl.BlockSpec((1, rows, tn), lambda l, j: (l, 0, j)),
            pl.BlockSpec((1, SUBLANE, LANE), lambda l, j: (l, 0, 0)),
        ],
        out_shape=[
            jax.ShapeDtypeStruct((depth, rows, 3 * D_MODEL), F32),
            jax.ShapeDtypeStruct((depth, SUBLANE, LANE), F32),
        ],
        compiler_params=_params("arbitrary", "arbitrary"),
        name="adaln_mod",
    )(cc, w_mod, b_mod, lam, linit)


def _rope(t, cos, sin):
    w = t.shape[-1]
    lane = lax.broadcasted_iota(jnp.int32, t.shape, 1)
    first = (lane % ROPE_AXIS) < (ROPE_AXIS // 2)
    rot = jnp.where(first, -pltpu.roll(t, w - ROPE_AXIS // 2, 1), pltpu.roll(t, ROPE_AXIS // 2, 1))
    return t * cos + rot * sin


HALO = SUBLANE


def _inproj_kernel(*refs, rope, seq_tiles):
    if rope:
        (x_ref, xp_ref, xn_ref, mod_ref, nw_ref, w_ref, wdt_ref, cw_ref, cb_ref, cos_ref, sin_ref,
         p_ref, dt_ref, ext_ref) = refs
    else:
        (x_ref, xp_ref, xn_ref, mod_ref, nw_ref, w_ref, wdt_ref, cw_ref, cb_ref,
         p_ref, dt_ref, ext_ref) = refs
    tm = x_ref.shape[0]
    t = pl.program_id(0) % seq_tiles
    m = mod_ref[0]

    def norm_mod(x):
        y = x * lax.rsqrt(jnp.mean(x * x, axis=-1, keepdims=True) + EPS) * nw_ref[...]
        return (y * (1.0 + m[:, D_MODEL:2 * D_MODEL]) + m[:, 0:D_MODEL]).astype(BF16)

    hb = norm_mod(x_ref[...])
    dt_ref[...] = _dot(hb, wdt_ref[...])
    halo = _dot(norm_mod(jnp.concatenate([xp_ref[...], xn_ref[...]], axis=0)), w_ref[:, 0:XBC_W])
    ext_ref[0:HALO, :] = jnp.where(t == 0, 0.0, halo[0:HALO])
    ext_ref[HALO + tm:2 * HALO + tm, :] = jnp.where(t == seq_tiles - 1, 0.0, halo[HALO:2 * HALO])
    ws = PROJ_STEP_B * CB
    for j in range(PROJ_STEPS):
        acc = _dot(hb, w_ref[:, j * ws:(j + 1) * ws])
        for b in range(PROJ_STEP_B):
            blk = j * PROJ_STEP_B + b
            t = acc[:, b * CB:(b + 1) * CB]
            if blk * CB < XBC_W:
                ext_ref[HALO:HALO + tm, blk * CB:(blk + 1) * CB] = t
                continue
            if COL_GATES <= blk < COL_FU:
                t = _sigmoid(t)
            elif blk in (COL_FG, COL_AG, COL_Z):
                t = _silu(t)
            elif blk in (COL_Q, COL_K) and rope:
                t = _rope(t, cos_ref[...], sin_ref[...])
            if blk == COL_Q:
                t = t * Q_SCALE
            p_ref[:, blk * CB:(blk + 1) * CB] = t.astype(BF16)
    ext_rows = tm + 2 * HALO
    for cols in (slice(c0, c0 + CB) for c0 in range(0, XBC_W, CB)):
        e = ext_ref[:, cols]
        w = [cw_ref[k:k + 1, cols] for k in range(CONV_W)]
        before = pltpu.roll(pltpu.roll(w[0] * e, 1, 0) + w[1] * e, 1, 0)
        after = pltpu.roll(pltpu.roll(w[4] * e, ext_rows - 1, 0) + w[3] * e, ext_rows - 1, 0)
        conv = (before + w[2] * e + after)[HALO:HALO + tm] + cb_ref[:, cols]
        p_ref[:, cols] = _silu(conv).astype(BF16)


def _inproj(x2, mod, norm_w, w_main, w_dt, conv_w8, conv_b, cos, sin, n_mod, n_seq, tm):
    rows = x2.shape[0]
    mod_tiles, seq_tiles = n_mod // tm, n_seq // tm
    hb = tm // HALO
    last_hb = rows // HALO - 1
    rope = cos is not None
    resident = lambda shape: pl.BlockSpec(shape, lambda i: (0, 0), pipeline_mode=pl.Buffered(1))
    in_specs = [
        pl.BlockSpec((tm, D_MODEL), lambda i: (i, 0)),
        pl.BlockSpec((HALO, D_MODEL), lambda i: (jnp.maximum(i * hb - 1, 0), 0)),
        pl.BlockSpec((HALO, D_MODEL), lambda i: (jnp.minimum((i + 1) * hb, last_hb), 0)),
        pl.BlockSpec((1, 1, 3 * D_MODEL), lambda i: (i // mod_tiles, 0, 0)),
        pl.BlockSpec((1, D_MODEL), lambda i: (0, 0)),
        resident((D_MODEL, N_MAIN)),
        resident((D_MODEL, LANE)),
        pl.BlockSpec((SUBLANE, XBC_W), lambda i: (0, 0)),
        pl.BlockSpec((1, XBC_W), lambda i: (0, 0)),
    ]
    args = [x2, x2, x2, mod, norm_w, w_main, w_dt, conv_w8, conv_b]
    if rope:
        in_specs += [pl.BlockSpec((tm, CB), lambda i: (i % seq_tiles, 0))] * 2
        args += [cos, sin]
    return pl.pallas_call(
        functools.partial(_inproj_kernel, rope=rope, seq_tiles=seq_tiles),
        grid=(rows // tm,),
        in_specs=in_specs,
        out_specs=[
            pl.BlockSpec((tm, N_MAIN), lambda i: (i, 0)),
            pl.BlockSpec((tm, LANE), lambda i: (i, 0)),
        ],
        out_shape=[
            jax.ShapeDtypeStruct((rows, N_MAIN), BF16),
            jax.ShapeDtypeStruct((rows, LANE), F32),
        ],
        scratch_shapes=[pltpu.VMEM((tm + 2 * HALO, XBC_W), F32)],
        compiler_params=_params("parallel"),
        name="inproj",
    )(*args)


def _attn_kernel(*refs, with_latent, one_minus_lam_init, n_ctx, chunks):
    if with_latent:
        (q_ref, kc_ref, vc_ref, kl_ref, vl_ref, lam_ref, sw_ref, o_ref,
         k_scr, v_scr, m_scr, acc_scr) = refs
    else:
        q_ref, kc_ref, vc_ref, lam_ref, sw_ref, o_ref, k_scr, v_scr, m_scr, acc_scr = refs
    tq = q_ref.shape[0]

    @pl.when(pl.program_id(2) == 0)
    def _():
        def put(k_ref, v_ref, off):
            rows = k_ref.shape[0]
            k = k_ref[...]
            lane = lax.broadcasted_iota(jnp.int32, k.shape, 1)
            zero = jnp.zeros_like(k)
            k_scr[0, off:off + rows, :] = jnp.where(lane < ATT_QK, k, zero)
            k_scr[1, off:off + rows, :] = jnp.where(lane >= ATT_QK, k, zero)
            v_scr[off:off + rows, 0:ATT_V] = v_ref[...]
            v_scr[off:off + rows, ATT_V:2 * ATT_V] = jnp.ones((rows, ATT_V), BF16)
        put(kc_ref, vc_ref, 0)
        if with_latent:
            put(kl_ref, vl_ref, n_ctx)

    m_scr[...] = jnp.full(m_scr.shape, -jnp.inf, F32)
    acc_scr[...] = jnp.zeros(acc_scr.shape, F32)
    q = q_ref[...]

    def chunk(off, size):
        v = v_scr[pl.ds(off, size), :]
        nt = size // LANE
        kk = jnp.concatenate([k_scr[0, pl.ds(off, size), :], k_scr[1, pl.ds(off, size), :]], axis=0)
        s_both = _dot_nt(q, kk)
        for mi in range(2):
            tiles = [s_both[:, mi * size + t * LANE:mi * size + (t + 1) * LANE] for t in range(nt)]
            mx = functools.reduce(jnp.maximum, tiles)
            m_prev = m_scr[mi]
            m_new = jnp.maximum(m_prev, jnp.max(mx, axis=-1, keepdims=True))
            p = jnp.concatenate([jnp.exp2(t - m_new) for t in tiles], axis=1).astype(BF16)
            alpha = jnp.exp2(m_prev - m_new)
            acc_scr[mi] = jnp.concatenate([alpha, alpha], axis=1) * acc_scr[mi] + _dot(p, v)
            m_scr[mi] = m_new

    for off, size in chunks:
        chunk(off, size)

    lam = lam_ref[0:1, :]
    a0, a1 = acc_scr[0], acc_scr[1]
    o = a0[:, 0:ATT_V] / a0[:, ATT_V:] - lam * (a1[:, 0:ATT_V] / a1[:, ATT_V:])
    y = o * lax.rsqrt(jnp.mean(o * o, axis=-1, keepdims=True) + EPS) * sw_ref[...]
    o_ref[...] = (y * one_minus_lam_init).astype(o_ref.dtype)


def _attention(qkv_q, qkv_c, qkv_l, lam_l, subln_w, lam_init, batch, nq_len, nc_len, nl_len, tq, tk):
    with_latent = qkv_l is not None
    nq = nq_len // tq
    hq, hk, hv = (c * (CB // LANE) for c in (COL_Q, COL_K, COL_V))
    in_specs = [
        pl.BlockSpec((tq, LANE), lambda b, h, i: (b * nq + i, hq + h)),
        pl.BlockSpec((nc_len, LANE), lambda b, h, i: (b, hk + h)),
        pl.BlockSpec((nc_len, LANE), lambda b, h, i: (b, hv + h)),
    ]
    args = [qkv_q, qkv_c, qkv_c]
    if with_latent:
        in_specs += [
            pl.BlockSpec((nl_len, LANE), lambda b, h, i: (b, hk + h)),
            pl.BlockSpec((nl_len, LANE), lambda b, h, i: (b, hv + h)),
        ]
        args += [qkv_l, qkv_l]
    in_specs += [
        pl.BlockSpec((SUBLANE, LANE), lambda b, h, i: (0, 0)),
        pl.BlockSpec((1, ATT_V), lambda b, h, i: (0, 0)),
    ]
    args += [lam_l, subln_w]
    n_keys = nc_len + nl_len
    sizes = [nc_len + tk] + [tk] * (nl_len // tk - 1) if with_latent else [nc_len]
    chunks = tuple((sum(sizes[:t]), sizes[t]) for t in range(len(sizes)))
    assert sum(sizes) == n_keys
    return pl.pallas_call(
        functools.partial(_attn_kernel, with_latent=with_latent,
                          one_minus_lam_init=1.0 - lam_init, n_ctx=nc_len, chunks=chunks),
        grid=(batch, ATT_HEADS, nq),
        in_specs=in_specs,
        out_specs=pl.BlockSpec((tq, ATT_V), lambda b, h, i: (b * nq + i, h)),
        out_shape=jax.ShapeDtypeStruct((batch * nq_len, ATT_HEADS * ATT_V), BF16),
        scratch_shapes=[
            pltpu.VMEM((2, n_keys, LANE), BF16),
            pltpu.VMEM((n_keys, 2 * ATT_V), BF16),
            pltpu.VMEM((2, tq, LANE), F32),
            pltpu.VMEM((2, tq, 2 * ATT_V), F32),
        ],
        compiler_params=_params("parallel", "parallel", "arbitrary"),
        name="diff_attention",
    )(*args)


def _fourier_kernel(x_ref, xb_ref, xn_ref, xm_ref, rev_ref, cs_ref, cc_ref, sc_ref, o_ref, acc_ref,
                    *, scale):
    k = pl.program_id(1)
    xn = jnp.where(k == 0, jnp.zeros_like(xn_ref[...]), xn_ref[...])
    pad = jnp.zeros((LANE - xn.shape[0], xn.shape[1]), BF16)
    xr = _dot(rev_ref[...], jnp.concatenate([xb_ref[...], xn, pad], axis=0)).astype(BF16)

    @pl.when(k == 0)
    def _():
        mid = _dot(xm_ref[...], cc_ref[0:F_W, :])[0:1, :]
        row = lax.broadcasted_iota(jnp.int32, acc_ref.shape, 0)
        acc_ref[...] = jnp.where(row % 2 == 0, mid, -mid)

    xx = jnp.concatenate([x_ref[...], xr], axis=1)
    ue = _dot(xx, cc_ref[...]).astype(BF16)
    wo = _dot(xx, sc_ref[...]).astype(BF16)
    acc_ref[...] += _dot(cs_ref[...], jnp.concatenate([ue, wo], axis=0))

    @pl.when(k == pl.num_programs(1) - 1)
    def _():
        o_ref[...] = (acc_ref[...] * scale).astype(o_ref.dtype)


DFT_SPLIT = 64


def _dft_tables(n, cols=None):
    cols = jnp.arange(n, dtype=jnp.int32) if cols is None else cols
    m = cols.shape[0]

    def cos_sin(rows):
        ang = ((rows[:, None] * cols[None, :]) % n).astype(F32) * (2.0 * math.pi / n)
        return jnp.cos(ang), jnp.sin(ang)

    if n <= DFT_SPLIT * DFT_SPLIT // 4 or n % DFT_SPLIT:
        return cos_sin(jnp.arange(n, dtype=jnp.int32))
    ca, sa = cos_sin(jnp.arange(n // DFT_SPLIT, dtype=jnp.int32) * DFT_SPLIT)
    cb, sb = cos_sin(jnp.arange(DFT_SPLIT, dtype=jnp.int32))
    ca, sa, cb, sb = ca[:, None, :], sa[:, None, :], cb[None, :, :], sb[None, :, :]
    return (ca * cb - sa * sb).reshape(n, m), (sa * cb + ca * sb).reshape(n, m)


def _seq_dft_table(n, tk):
    col = jnp.arange(n, dtype=jnp.int32)
    k = (col // (2 * tk)) * tk + col % tk
    cn, sn = _dft_tables(n, k)
    return jnp.where(((col // tk) % 2 == 0)[None, :], cn, -sn).astype(BF16)


def _fourier(p, cs, ccd2, scd2, batch, n, tk):
    half = n // 2
    nk, nkh = n // tk, half // tk
    scale = 1.0 / math.sqrt(n * F_GROUP_W)
    sb = SUBLANE_BF16
    rev = np.zeros((tk, tk + LANE), np.float32)
    rev[np.arange(1, tk), tk - np.arange(1, tk)] = 1.0
    rev[0, tk] = 1.0
    seq_blks, tk_blks = n // sb, tk // sb
    mid_blk = half // sb
    return pl.pallas_call(
        functools.partial(_fourier_kernel, scale=scale),
        grid=(batch, nkh),
        in_specs=[
            pl.BlockSpec((tk, CB), lambda b, k: (b * nk + k, COL_FU)),
            pl.BlockSpec((tk, CB), lambda b, k: (b * nk + nk - 1 - k, COL_FU)),
            pl.BlockSpec((sb, CB), lambda b, k: (b * seq_blks + jnp.minimum((nk - k) * tk_blks,
                                                                            seq_blks - 1), COL_FU)),
            pl.BlockSpec((sb, CB), lambda b, k: (b * seq_blks + mid_blk, COL_FU)),
            pl.BlockSpec((tk, tk + LANE), lambda b, k: (0, 0)),
            pl.BlockSpec((n, 2 * tk), lambda b, k: (0, k)),
            pl.BlockSpec((2 * F_W, F_W), lambda b, k: (0, 0)),
            pl.BlockSpec((2 * F_W, F_W), lambda b, k: (0, 0)),
        ],
        out_specs=pl.BlockSpec((n, F_W), lambda b, k: (b, 0)),
        out_shape=jax.ShapeDtypeStruct((batch * n, F_W), BF16),
        scratch_shapes=[pltpu.VMEM((n, F_W), F32)],
        compiler_params=_params("parallel", "arbitrary"),
        name="fourier_mix",
    )(p, p, p, p, jnp.asarray(rev, BF16), cs, ccd2, scd2)


def _softplus(x):
    return jnp.maximum(x, 0.0) + jnp.log1p(jnp.exp(-jnp.abs(x)))


def _ssd_kernel(xsf_ref, bcf_ref, dtf_ref, xsb_ref, bcb_ref, dtb_ref, alog_ref, dtbias_ref,
                e_ref, init_ref, yf_ref, yb_ref, fin_ref, state_ref):
    c = pl.program_id(1)
    q = CHUNK
    cps = xsf_ref.shape[0] // q
    gw = D_INNER // SSD_GROUPS
    hpg = SSD_HEADS // SSD_GROUPS

    @pl.when(c == 0)
    def _():
        state_ref[...] = init_ref[0]

    row = lax.broadcasted_iota(jnp.int32, (q, q), 0)
    col = lax.broadcasted_iota(jnp.int32, (q, q), 1)
    head_of_lane = lax.broadcasted_iota(jnp.int32, (q, gw), 1) // SSD_P
    neg_a = -jnp.exp(alog_ref[...])
    dtbias = dtbias_ref[...]

    def expand(x, e2):
        x1 = x.astype(BF16)
        x2 = (x - x1.astype(F32)).astype(BF16)
        return _dot(jnp.concatenate([x1, x2], axis=1), e2)

    tris = (row >= col, row <= col)
    tri_bs = tuple(t.astype(F32).astype(BF16) for t in tris)
    tri_b2s = tuple(jnp.concatenate([t, t], axis=1) for t in tri_bs)

    def one_chunk(d, rows, xs_ref, bc_ref, dt_ref, y_ref):
        fwd = d == 0
        tri, tri_b = tris[d], tri_bs[d]
        dt = _softplus(dt_ref[rows, :] + dtbias)
        a1, a2, a3 = _split3(dt * neg_a)
        acs = (_dot(tri_b2s[d], jnp.concatenate([a1, a2], axis=0))
               + _dot(tri_b, a3))
        acs_t = acs.T
        last = q - 1 if fwd else 0
        tot_row = acs[last:last + 1, :]
        e2 = e_ref[d]
        dt_w = expand(dt, e2)
        eacs_w = expand(jnp.exp(acs), e2)
        dec_w = expand(jnp.exp(tot_row - acs), e2)
        sdec = _dot_sel_r(jnp.broadcast_to(jnp.exp(tot_row), (SUBLANE, LANE)), e2[0:LANE])[0:1, :]
        xd = xs_ref[rows, :].astype(F32) * dt_w
        xdb = xd.astype(BF16)
        xdd = (xd * dec_w).astype(BF16)
        bc = bc_ref[rows, :]
        for g in range(SSD_GROUPS):
            bg = bc[:, g * D_STATE:(g + 1) * D_STATE]
            cg = bc[:, (SSD_GROUPS + g) * D_STATE:(SSD_GROUPS + g + 1) * D_STATE]
            cb = _dot_nt(cg, bg)
            lanes = slice(g * gw, (g + 1) * gw)
            s_g = state_ref[d, :, lanes]
            y = _dot(cg, s_g.astype(BF16)) * eacs_w[:, lanes]
            xg = xdb[:, lanes]
            zero = jnp.zeros_like(xg)
            for r in range(0, hpg, 2):
                mats, xms = [], []
                for rr in (r, r + 1):
                    jl = d * SSD_HEADS + g * hpg + rr
                    seg = jnp.where(tri, acs[:, jl:jl + 1] - acs_t[jl:jl + 1, :], -jnp.inf)
                    mats.append((cb * jnp.exp(seg)).astype(BF16))
                    xms.append(jnp.where(head_of_lane == rr, xg, zero))
                y = y + _dot(jnp.concatenate(mats, axis=1), jnp.concatenate(xms, axis=0))
            y_ref[rows, lanes] = y.astype(y_ref.dtype)
            bg_t = bg.astype(F32).T.astype(BF16)
            state_ref[d, :, lanes] = s_g * sdec[:, lanes] + _dot(bg_t, xdd[:, lanes])

    for i in range(cps):
        one_chunk(0, pl.ds(i * q, q), xsf_ref, bcf_ref, dtf_ref, yf_ref)
        one_chunk(1, pl.ds((cps - 1 - i) * q, q), xsb_ref, bcb_ref, dtb_ref, yb_ref)

    @pl.when(c == pl.num_programs(1) - 1)
    def _():
        fin_ref[0] = state_ref[...]


def _ssd(act, dt, alog, dtbias, emat2, init, batch, n, cps):
    nc = n // (CHUNK * cps)
    tr = CHUNK * cps
    fidx = lambda b, c: b * nc + c
    bidx = lambda b, c: b * nc + (nc - 1 - c)
    st_shape = (2, D_STATE, D_INNER)
    return pl.pallas_call(
        _ssd_kernel,
        grid=(batch, nc),
        in_specs=[
            pl.BlockSpec((tr, CB), lambda b, c: (fidx(b, c), 0)),
            pl.BlockSpec((tr, CB), lambda b, c: (fidx(b, c), 1)),
            pl.BlockSpec((tr, LANE), lambda b, c: (fidx(b, c), 0)),
            pl.BlockSpec((tr, CB), lambda b, c: (bidx(b, c), 0)),
            pl.BlockSpec((tr, CB), lambda b, c: (bidx(b, c), 1)),
            pl.BlockSpec((tr, LANE), lambda b, c: (bidx(b, c), 0)),
            pl.BlockSpec((1, LANE), lambda b, c: (0, 0)),
            pl.BlockSpec((1, LANE), lambda b, c: (0, 0)),
            pl.BlockSpec((2, 2 * LANE, D_INNER), lambda b, c: (0, 0, 0)),
            pl.BlockSpec((1,) + st_shape, lambda b, c: (b, 0, 0, 0)),
        ],
        out_specs=[
            pl.BlockSpec((tr, D_INNER), lambda b, c: (fidx(b, c), 0)),
            pl.BlockSpec((tr, D_INNER), lambda b, c: (bidx(b, c), 0)),
            pl.BlockSpec((1,) + st_shape, lambda b, c: (b, 0, 0, 0)),
        ],
        out_shape=[
            jax.ShapeDtypeStruct((batch * n, D_INNER), BF16),
            jax.ShapeDtypeStruct((batch * n, D_INNER), BF16),
            jax.ShapeDtypeStruct((batch,) + st_shape, F32),
        ],
        scratch_shapes=[pltpu.VMEM(st_shape, F32)],
        compiler_params=_params("parallel", "arbitrary"),
        name="ssd_scan",
    )(act, act, dt, act, act, dt, alog, dtbias, emat2, init)


def _merge_kernel(four_ref, att_ref, yf_ref, yb_ref, xs_ref, fg_ref, ag_ref, z_ref,
                  g0_ref, g1_ref, g2_ref, x_ref, mod_ref, dskip_ref, snw_ref,
                  wof_ref, woa_ref, wos_ref, wout_ref, nf_ref, o_ref, *, final_norm):
    f = lambda ref: ref[...].astype(F32)
    y_f = _dot((f(four_ref) * f(fg_ref)).astype(BF16), wof_ref[...])
    y_a = _dot((f(att_ref) * f(ag_ref)).astype(BF16), woa_ref[...])
    ys = f(yf_ref) + f(yb_ref) + dskip_ref[...] * f(xs_ref)
    t = ys * f(z_ref)
    t = t * lax.rsqrt(jnp.mean(t * t, axis=-1, keepdims=True) + EPS) * snw_ref[...]
    y_s = _dot(t.astype(BF16), wos_ref[...])
    y = f(g0_ref) * y_f + f(g1_ref) * y_a + f(g2_ref) * y_s
    out = _dot(y.astype(BF16), wout_ref[...])
    xn = x_ref[...] + mod_ref[0][:, 2 * D_MODEL:3 * D_MODEL] * out
    if final_norm:
        xn = xn * lax.rsqrt(jnp.mean(xn * xn, axis=-1, keepdims=True) + EPS) * nf_ref[...]
    o_ref[...] = xn


def _merge(four, att, yf, yb, p, x2, mod, dskip_w, snw, wof, woa, wos, wout, norm_f,
           n, tm, final_norm):
    rows = x2.shape[0]
    tiles_per_b = n // tm
    gcol = COL_GATES * CB // D_MODEL
    row_blk = lambda w, cidx: pl.BlockSpec((tm, w), lambda i: (i, cidx))
    const = lambda shape: pl.BlockSpec(shape, lambda i: (0,) * len(shape))
    return pl.pallas_call(
        functools.partial(_merge_kernel, final_norm=final_norm),
        grid=(rows // tm,),
        in_specs=[
            row_blk(F_W, 0), row_blk(CB, 0), row_blk(D_INNER, 0), row_blk(D_INNER, 0),
            row_blk(CB, 0),
            row_blk(CB, COL_FG), row_blk(CB, COL_AG), row_blk(CB, COL_Z),
            row_blk(D_MODEL, gcol), row_blk(D_MODEL, gcol + 1), row_blk(D_MODEL, gcol + 2),
            row_blk(D_MODEL, 0),
            pl.BlockSpec((1, 1, 3 * D_MODEL), lambda i: (i // tiles_per_b, 0, 0)),
            const((1, D_INNER)), const((1, D_INNER)),
            const((F_W, D_MODEL)), const((CB, D_MODEL)), const((D_INNER, D_MODEL)),
            const((D_MODEL, D_MODEL)), const((1, D_MODEL)),
        ],
        out_specs=pl.BlockSpec((tm, D_MODEL), lambda i: (i, 0)),
        out_shape=jax.ShapeDtypeStruct((rows, D_MODEL), F32),
        compiler_params=_params("parallel"),
        name="branch_merge",
    )(four, att, yf, yb, p, p, p, p, p, p, p, x2, mod, dskip_w, snw, wof, woa, wos, wout, norm_f)


def _rope_tables(n):
    rows = n // GRID_W
    row = jnp.repeat(jnp.arange(rows, dtype=F32), GRID_W)
    col = jnp.tile(jnp.arange(GRID_W, dtype=F32), rows)
    freqs = ROPE_BASE ** (-jnp.arange(0, ROPE_AXIS, 2, dtype=F32) / ROPE_AXIS)
    ang_r = row[:, None] * freqs
    ang_c = col[:, None] * freqs
    ang = jnp.concatenate([ang_r, ang_r, ang_c, ang_c], axis=-1)
    reps = CB // ATT_QK
    return jnp.tile(jnp.cos(ang), (1, reps)), jnp.tile(jnp.sin(ang), (1, reps))


def _head_expanders():
    e = np.zeros((2, LANE, D_INNER), np.float32)
    for d in range(2):
        for h in range(SSD_HEADS):
            e[d, d * SSD_HEADS + h, h * SSD_P:(h + 1) * SSD_P] = 1.0
    return jnp.asarray(np.concatenate([e, e], axis=1), BF16)


def _group_dft(w):
    cw, sw = _dft_tables(F_GROUP_W)
    eye = jnp.eye(w // F_GROUP_W, dtype=F32)
    cc, sc = jnp.kron(eye, cw), jnp.kron(eye, sw)
    return (jnp.concatenate([cc, cc], axis=0).astype(BF16),
            jnp.concatenate([sc, -sc], axis=0).astype(BF16))


def kernel(x, c, ctx, c_ctx, w_mod, b_mod, norm_w, w_in, conv_w, conv_b, a_log, dt_bias, d_skip,
           ssd_norm_w, lam, subln_w, w_of, w_oa, w_os, w_out, norm_f):
    batch, n, _ = x.shape
    n_ctx = ctx.shape[1]
    depth = w_mod.shape[0]
    assert n % GRID_W == 0 and n % CHUNK == 0 and n_ctx % CHUNK == 0

    o_q = 2 * F_W
    o_ag = o_q + 3 * CB
    o_xbc = o_ag + CB + D_INNER
    o_dt = o_xbc + XBC_W
    o_gt = o_dt + DT_W
    w_main = jnp.concatenate([w_in[:, :, o_xbc:o_dt], w_in[:, :, o_gt:], w_in[:, :, :o_q],
                              w_in[:, :, o_ag:o_xbc], w_in[:, :, o_q:o_ag]], axis=-1).astype(BF16)
    w_dt = jnp.pad(w_in[:, :, o_dt:o_gt], ((0, 0), (0, 0), (0, LANE - DT_W))).astype(BF16)
    conv_w8 = jnp.pad(conv_w, ((0, 0), (0, SUBLANE - CONV_W), (0, 0)))
    pad_lanes = lambda a: jnp.pad(a.reshape(depth, 1, DT_W), ((0, 0), (0, 0), (0, LANE - DT_W)))
    alog_p, dtbias_p = pad_lanes(a_log), pad_lanes(dt_bias)
    dskip_w = jnp.repeat(d_skip, SSD_P, axis=-1).reshape(depth, 1, D_INNER)
    wof_b, woa_b, wos_b, wout_b = (w.astype(BF16) for w in (w_of, w_oa, w_os, w_out))
    lam_inits = [0.8 - 0.6 * math.exp(-0.3 * l) for l in range(depth)]
    linit = jnp.asarray(np.broadcast_to(np.asarray(lam_inits, np.float32)[:, None, None],
                                        (depth, SUBLANE, LANE)))

    cos_t, sin_t = _rope_tables(n)
    tf_l, tf_c = math.gcd(n // 2, TK_FOURIER), math.gcd(n_ctx // 2, TK_FOURIER)
    cs_l = _seq_dft_table(n, tf_l)
    cs_c = _seq_dft_table(n_ctx, tf_c)
    ccd, scd = _group_dft(F_W)
    emat = _head_expanders()
    zero_state = jnp.zeros((batch, 2, D_STATE, D_INNER), F32)

    mod_rows = -(-(batch + 1) // SUBLANE) * SUBLANE
    cc = jnp.concatenate([c, c_ctx[None, :], jnp.zeros((mod_rows - batch - 1, D_MODEL), F32)], axis=0)
    mod_all, lam_all = _modulation(cc, w_mod, b_mod.reshape(depth, 1, 3 * D_MODEL), lam, linit)

    tm_l, tm_c = math.gcd(n, TM_PROJ), math.gcd(n_ctx, TM_PROJ)
    tq_l, tk_l = math.gcd(n, TQ_ATT), math.gcd(n, TK_ATT)
    tmerge_l = math.gcd(n, TM_MERGE)
    rows_c = batch * n_ctx
    cps_l = math.gcd(n // CHUNK, SSD_CHUNKS_PER_STEP)
    cps_c = math.gcd(n_ctx // CHUNK, SSD_CHUNKS_PER_STEP)
    xl = x.reshape(batch * n, D_MODEL)
    xc = ctx.reshape(batch * n_ctx, D_MODEL)
    for l in range(depth):
        last = l == depth - 1
        mod_l = mod_all[l, :batch].reshape(batch, 1, 3 * D_MODEL)
        mod_c = mod_all[l, batch].reshape(1, 1, 3 * D_MODEL)
        nw = norm_w[l].reshape(1, D_MODEL)
        cb_ = conv_b[l].reshape(1, XBC_W)
        p_c, dt_c = _inproj(xc, mod_c, nw, w_main[l], w_dt[l], conv_w8[l], cb_, None, None,
                            rows_c, n_ctx, tm_c)
        p_l, dt_l = _inproj(xl, mod_l, nw, w_main[l], w_dt[l], conv_w8[l], cb_, cos_t, sin_t,
                            n, n, tm_l)
        yf_c, yb_c, st_c = _ssd(p_c, dt_c, alog_p[l], dtbias_p[l], emat, zero_state, batch, n_ctx,
                                cps_c)
        yf_l, yb_l, _ = _ssd(p_l, dt_l, alog_p[l], dtbias_p[l], emat, st_c, batch, n, cps_l)
        sw = subln_w[l].reshape(1, ATT_V)
        att_l = _attention(p_l, p_c, p_l, lam_all[l], sw, lam_inits[l], batch, n, n_ctx, n,
                           tq_l, tk_l)
        four_l = _fourier(p_l, cs_l, ccd, scd, batch, n, tf_l)
        merge_w = (dskip_w[l], ssd_norm_w[l].reshape(1, D_INNER), wof_b[l], woa_b[l], wos_b[l],
                   wout_b[l], norm_f.reshape(1, D_MODEL))
        if not last:
            att_c = _attention(p_c, p_c, None, lam_all[l], sw, lam_inits[l], batch, n_ctx, n_ctx, 0,
                               n_ctx, 0)
            four_c = _fourier(p_c, cs_c, ccd, scd, batch, n_ctx, tf_c)
            xc = _merge(four_c, att_c, yf_c, yb_c, p_c, xc, mod_c, *merge_w, rows_c, n_ctx, False)
        xl = _merge(four_l, att_l, yf_l, yb_l, p_l, xl, mod_l, *merge_w, n, tmerge_l, last)
    return xl.reshape(batch, n, D_MODEL)
```

```python
import functools
import math

import numpy as np
import jax
import jax.numpy as jnp
from jax import lax
from jax.experimental import pallas as pl
from jax.experimental.pallas import tpu as pltpu

F32 = jnp.float32
BF16 = jnp.bfloat16
EPS = 1e-6

D_MODEL = 1024
GRID_W = 64
F_GROUP_W = 128
F_W = 512
ATT_HEADS = 4
ATT_QK = 64
ATT_V = 128
ROPE_AXIS = 32
ROPE_BASE = 10000.0
D_INNER = 512
SSD_P = 64
SSD_HEADS = 8
SSD_GROUPS = 2
D_STATE = 128
CONV_W = 5
CHUNK = 128
XBC_W = 1024
DT_W = 16

LANE = 128
SUBLANE = 8
SUBLANE_BF16 = 16
VMEM_LIMIT = 56 * 1024 * 1024

TM_PROJ = 512
TQ_ATT = 1024
TK_ATT = 1024
TK_FOURIER = 512
TM_MERGE = 512
SSD_CHUNKS_PER_STEP = 8

CB = 512
COL_XBC, COL_GATES, COL_FU, COL_FG, COL_AG, COL_Z = 0, 2, 8, 9, 10, 11
COL_Q, COL_K, COL_V = 12, 13, 14
N_COLB = 15
N_MAIN = N_COLB * CB
PROJ_STEP_B = 3
PROJ_STEPS = N_COLB // PROJ_STEP_B
assert PROJ_STEPS * PROJ_STEP_B == N_COLB
Q_SCALE = ATT_QK ** -0.5 * math.log2(math.e)


def _dot(a, b):
    return jnp.dot(a, b, preferred_element_type=F32)


def _dot_nt(a, b):
    return lax.dot_general(a, b, (((1,), (1,)), ((), ())), preferred_element_type=F32)


def _split3(x):
    x1 = x.astype(BF16)
    r1 = x - x1.astype(F32)
    x2 = r1.astype(BF16)
    x3 = (r1 - x2.astype(F32)).astype(BF16)
    return x1, x2, x3


def _dot_sel_r(x, sel):
    x1, x2, x3 = _split3(x)
    return _dot(x1, sel) + _dot(x2, sel) + _dot(x3, sel)


def _sigmoid(x):
    return 0.5 * jnp.tanh(0.5 * x) + 0.5


def _silu(x):
    return x * _sigmoid(x)


def _params(*sem):
    return pltpu.CompilerParams(dimension_semantics=sem, vmem_limit_bytes=VMEM_LIMIT)


def _mod_kernel(cc_ref, w_ref, b_ref, lam_ref, linit_ref, mod_ref, lam_out_ref):
    s = _silu(cc_ref[...])
    mod_ref[0] = jnp.dot(s, w_ref[0], precision=lax.Precision.HIGHEST,
                         preferred_element_type=F32) + b_ref[0]
    lp = lam_ref[0]
    s1 = jnp.sum(lp[0:1] * lp[1:2], axis=-1, keepdims=True)
    s2 = jnp.sum(lp[2:3] * lp[3:4], axis=-1, keepdims=True)
    lam_out_ref[0] = jnp.broadcast_to(jnp.exp(s1) - jnp.exp(s2), (SUBLANE, LANE)) + linit_ref[0]


def _modulation(cc, w_mod, b_mod, lam, linit):
    depth = w_mod.shape[0]
    rows = cc.shape[0]
    tn = D_MODEL
    return pl.pallas_call(
        _mod_kernel,
        grid=(depth, 3 * D_MODEL // tn),
        in_specs=[
            pl.BlockSpec((rows, D_MODEL), lambda l, j: (0, 0)),
            pl.BlockSpec((1, D_MODEL, tn), lambda l, j: (l, 0, j)),
            pl.BlockSpec((1, 1, tn), lambda l, j: (l, 0, j)),
            pl.BlockSpec((1, 4, ATT_QK), lambda l, j: (l, 0, 0)),
            pl.BlockSpec((1, SUBLANE, LANE), lambda l, j: (l, 0, 0)),
        ],
        out_specs=[
            pl.BlockSpec((1, rows, tn), lambda l, j: (l, 0, j)),
            pl.BlockSpec((1, SUBLANE, LANE), lambda l, j: (l, 0, 0)),
        ],
        out_shape=[
            jax.ShapeDtypeStruct((depth, rows, 3 * D_MODEL), F32),
            jax.ShapeDtypeStruct((depth, SUBLANE, LANE), F32),
        ],
        compiler_params=_params("arbitrary", "arbitrary"),
        name="adaln_mod",
    )(cc, w_mod, b_mod, lam, linit)


def _rope(t, cos, sin):
    w = t.shape[-1]
    lane = lax.broadcasted_iota(jnp.int32, t.shape, 1)
    first = (lane % ROPE_AXIS) < (ROPE_AXIS // 2)
    rot = jnp.where(first, -pltpu.roll(t, w - ROPE_AXIS // 2, 1), pltpu.roll(t, ROPE_AXIS // 2, 1))
    return t * cos + rot * sin


HALO = SUBLANE


def _inproj_kernel(*refs, rope, seq_tiles):
    if rope:
        (x_ref, xp_ref, xn_ref, mod_ref, nw_ref, w_ref, wdt_ref, cw_ref, cb_ref, cos_ref, sin_ref,
         p_ref, dt_ref, ext_ref) = refs
    else:
        (x_ref, xp_ref, xn_ref, mod_ref, nw_ref, w_ref, wdt_ref, cw_ref, cb_ref,
         p_ref, dt_ref, ext_ref) = refs
    tm = x_ref.shape[0]
    t = pl.program_id(0) % seq_tiles
    m = mod_ref[0]

    def norm_mod(x):
        y = x * lax.rsqrt(jnp.mean(x * x, axis=-1, keepdims=True) + EPS) * nw_ref[...]
        return (y * (1.0 + m[:, D_MODEL:2 * D_MODEL]) + m[:, 0:D_MODEL]).astype(BF16)

    hb = norm_mod(x_ref[...])
    dt_ref[...] = _dot(hb, wdt_ref[...])
    halo = _dot(norm_mod(jnp.concatenate([xp_ref[...], xn_ref[...]], axis=0)), w_ref[:, 0:XBC_W])
    ext_ref[0:HALO, :] = jnp.where(t == 0, 0.0, halo[0:HALO])
    ext_ref[HALO + tm:2 * HALO + tm, :] = jnp.where(t == seq_tiles - 1, 0.0, halo[HALO:2 * HALO])
    ws = PROJ_STEP_B * CB
    for j in range(PROJ_STEPS):
        acc = _dot(hb, w_ref[:, j * ws:(j + 1) * ws])
        for b in range(PROJ_STEP_B):
            blk = j * PROJ_STEP_B + b
            t = acc[:, b * CB:(b + 1) * CB]
            if blk * CB < XBC_W:
                ext_ref[HALO:HALO + tm, blk * CB:(blk + 1) * CB] = t
                continue
            if COL_GATES <= blk < COL_FU:
                t = _sigmoid(t)
            elif blk in (COL_FG, COL_AG, COL_Z):
                t = _silu(t)
            elif blk in (COL_Q, COL_K) and rope:
                t = _rope(t, cos_ref[...], sin_ref[...])
            if blk == COL_Q:
                t = t * Q_SCALE
            p_ref[:, blk * CB:(blk + 1) * CB] = t.astype(BF16)
    ext_rows = tm + 2 * HALO
    for cols in (slice(c0, c0 + CB) for c0 in range(0, XBC_W, CB)):
        e = ext_ref[:, cols]
        w = [cw_ref[k:k + 1, cols] for k in range(CONV_W)]
        before = pltpu.roll(pltpu.roll(w[0] * e, 1, 0) + w[1] * e, 1, 0)
        after = pltpu.roll(pltpu.roll(w[4] * e, ext_rows - 1, 0) + w[3] * e, ext_rows - 1, 0)
        conv = (before + w[2] * e + after)[HALO:HALO + tm] + cb_ref[:, cols]
        p_ref[:, cols] = _silu(conv).astype(BF16)


def _inproj(x2, mod, norm_w, w_main, w_dt, conv_w8, conv_b, cos, sin, n_mod, n_seq, tm):
    rows = x2.shape[0]
    mod_tiles, seq_tiles = n_mod // tm, n_seq // tm
    hb = tm // HALO
    last_hb = rows // HALO - 1
    rope = cos is not None
    resident = lambda shape: pl.BlockSpec(shape, lambda i: (0, 0), pipeline_mode=pl.Buffered(1))
    in_specs = [
        pl.BlockSpec((tm, D_MODEL), lambda i: (i, 0)),
        pl.BlockSpec((HALO, D_MODEL), lambda i: (jnp.maximum(i * hb - 1, 0), 0)),
        pl.BlockSpec((HALO, D_MODEL), lambda i: (jnp.minimum((i + 1) * hb, last_hb), 0)),
        pl.BlockSpec((1, 1, 3 * D_MODEL), lambda i: (i // mod_tiles, 0, 0)),
        pl.BlockSpec((1, D_MODEL), lambda i: (0, 0)),
        resident((D_MODEL, N_MAIN)),
        resident((D_MODEL, LANE)),
        pl.BlockSpec((SUBLANE, XBC_W), lambda i: (0, 0)),
        pl.BlockSpec((1, XBC_W), lambda i: (0, 0)),
    ]
    args = [x2, x2, x2, mod, norm_w, w_main, w_dt, conv_w8, conv_b]
    if rope:
        in_specs += [pl.BlockSpec((tm, CB), lambda i: (i % seq_tiles, 0))] * 2
        args += [cos, sin]
    return pl.pallas_call(
        functools.partial(_inproj_kernel, rope=rope, seq_tiles=seq_tiles),
        grid=(rows // tm,),
        in_specs=in_specs,
        out_specs=[
            pl.BlockSpec((tm, N_MAIN), lambda i: (i, 0)),
            pl.BlockSpec((tm, LANE), lambda i: (i, 0)),
        ],
        out_shape=[
            jax.ShapeDtypeStruct((rows, N_MAIN), BF16),
            jax.ShapeDtypeStruct((rows, LANE), F32),
        ],
        scratch_shapes=[pltpu.VMEM((tm + 2 * HALO, XBC_W), F32)],
        compiler_params=_params("parallel"),
        name="inproj",
    )(*args)


def _attn_kernel(*refs, with_latent, one_minus_lam_init, n_ctx, chunks):
    if with_latent:
        (q_ref, kc_ref, vc_ref, kl_ref, vl_ref, lam_ref, sw_ref, o_ref,
         k_scr, v_scr, m_scr, acc_scr) = refs
    else:
        q_ref, kc_ref, vc_ref, lam_ref, sw_ref, o_ref, k_scr, v_scr, m_scr, acc_scr = refs
    tq = q_ref.shape[0]

    @pl.when(pl.program_id(2) == 0)
    def _():
        def put(k_ref, v_ref, off):
            rows = k_ref.shape[0]
            k = k_ref[...]
            lane = lax.broadcasted_iota(jnp.int32, k.shape, 1)
            zero = jnp.zeros_like(k)
            k_scr[0, off:off + rows, :] = jnp.where(lane < ATT_QK, k, zero)
            k_scr[1, off:off + rows, :] = jnp.where(lane >= ATT_QK, k, zero)
            v_scr[off:off + rows, 0:ATT_V] = v_ref[...]
            v_scr[off:off + rows, ATT_V:2 * ATT_V] = jnp.ones((rows, ATT_V), BF16)
        put(kc_ref, vc_ref, 0)
        if with_latent:
            put(kl_ref, vl_ref, n_ctx)

    m_scr[...] = jnp.full(m_scr.shape, -jnp.inf, F32)
    acc_scr[...] = jnp.zeros(acc_scr.shape, F32)
    q = q_ref[...]

    def chunk(off, size):
        v = v_scr[pl.ds(off, size), :]
        nt = size // LANE
        kk = jnp.concatenate([k_scr[0, pl.ds(off, size), :], k_scr[1, pl.ds(off, size), :]], axis=0)
        s_both = _dot_nt(q, kk)
        for mi in range(2):
            tiles = [s_both[:, mi * size + t * LANE:mi * size + (t + 1) * LANE] for t in range(nt)]
            mx = functools.reduce(jnp.maximum, tiles)
            m_prev = m_scr[mi]
            m_new = jnp.maximum(m_prev, jnp.max(mx, axis=-1, keepdims=True))
            p = jnp.concatenate([jnp.exp2(t - m_new) for t in tiles], axis=1).astype(BF16)
            alpha = jnp.exp2(m_prev - m_new)
            acc_scr[mi] = jnp.concatenate([alpha, alpha], axis=1) * acc_scr[mi] + _dot(p, v)
            m_scr[mi] = m_new

    for off, size in chunks:
        chunk(off, size)

    lam = lam_ref[0:1, :]
    a0, a1 = acc_scr[0], acc_scr[1]
    o = a0[:, 0:ATT_V] / a0[:, ATT_V:] - lam * (a1[:, 0:ATT_V] / a1[:, ATT_V:])
    y = o * lax.rsqrt(jnp.mean(o * o, axis=-1, keepdims=True) + EPS) * sw_ref[...]
    o_ref[...] = (y * one_minus_lam_init).astype(o_ref.dtype)


def _attention(qkv_q, qkv_c, qkv_l, lam_l, subln_w, lam_init, batch, nq_len, nc_len, nl_len, tq, tk):
    with_latent = qkv_l is not None
    nq = nq_len // tq
    hq, hk, hv = (c * (CB // LANE) for c in (COL_Q, COL_K, COL_V))
    in_specs = [
        pl.BlockSpec((tq, LANE), lambda b, h, i: (b * nq + i, hq + h)),
        pl.BlockSpec((nc_len, LANE), lambda b, h, i: (b, hk + h)),
        pl.BlockSpec((nc_len, LANE), lambda b, h, i: (b, hv + h)),
    ]
    args = [qkv_q, qkv_c, qkv_c]
    if with_latent:
        in_specs += [
            pl.BlockSpec((nl_len, LANE), lambda b, h, i: (b, hk + h)),
            pl.BlockSpec((nl_len, LANE), lambda b, h, i: (b, hv + h)),
        ]
        args += [qkv_l, qkv_l]
    in_specs += [
        pl.BlockSpec((SUBLANE, LANE), lambda b, h, i: (0, 0)),
        pl.BlockSpec((1, ATT_V), lambda b, h, i: (0, 0)),
    ]
    args += [lam_l, subln_w]
    n_keys = nc_len + nl_len
    sizes = [nc_len + tk] + [tk] * (nl_len // tk - 1) if with_latent else [nc_len]
    chunks = tuple((sum(sizes[:t]), sizes[t]) for t in range(len(sizes)))
    assert sum(sizes) == n_keys
    return pl.pallas_call(
        functools.partial(_attn_kernel, with_latent=with_latent,
                          one_minus_lam_init=1.0 - lam_init, n_ctx=nc_len, chunks=chunks),
        grid=(batch, ATT_HEADS, nq),
        in_specs=in_specs,
        out_specs=pl.BlockSpec((tq, ATT_V), lambda b, h, i: (b * nq + i, h)),
        out_shape=jax.ShapeDtypeStruct((batch * nq_len, ATT_HEADS * ATT_V), BF16),
        scratch_shapes=[
            pltpu.VMEM((2, n_keys, LANE), BF16),
            pltpu.VMEM((n_keys, 2 * ATT_V), BF16),
            pltpu.VMEM((2, tq, LANE), F32),
            pltpu.VMEM((2, tq, 2 * ATT_V), F32),
        ],
        compiler_params=_params("parallel", "parallel", "arbitrary"),
        name="diff_attention",
    )(*args)


def _fourier_kernel(x_ref, xb_ref, xn_ref, xm_ref, rev_ref, cs_ref, cc_ref, sc_ref, o_ref, acc_ref,
                    *, scale):
    k = pl.program_id(1)
    xn = jnp.where(k == 0, jnp.zeros_like(xn_ref[...]), xn_ref[...])
    pad = jnp.zeros((LANE - xn.shape[0], xn.shape[1]), BF16)
    xr = _dot(rev_ref[...], jnp.concatenate([xb_ref[...], xn, pad], axis=0)).astype(BF16)

    @pl.when(k == 0)
    def _():
        mid = _dot(xm_ref[...], cc_ref[0:F_W, :])[0:1, :]
        row = lax.broadcasted_iota(jnp.int32, acc_ref.shape, 0)
        acc_ref[...] = jnp.where(row % 2 == 0, mid, -mid)

    xx = jnp.concatenate([x_ref[...], xr], axis=1)
    ue = _dot(xx, cc_ref[...]).astype(BF16)
    wo = _dot(xx, sc_ref[...]).astype(BF16)
    acc_ref[...] += _dot(cs_ref[...], jnp.concatenate([ue, wo], axis=0))

    @pl.when(k == pl.num_programs(1) - 1)
    def _():
        o_ref[...] = (acc_ref[...] * scale).astype(o_ref.dtype)


DFT_SPLIT = 64


def _dft_tables(n, cols=None):
    cols = jnp.arange(n, dtype=jnp.int32) if cols is None else cols
    m = cols.shape[0]

    def cos_sin(rows):
        ang = ((rows[:, None] * cols[None, :]) % n).astype(F32) * (2.0 * math.pi / n)
        return jnp.cos(ang), jnp.sin(ang)

    if n <= DFT_SPLIT * DFT_SPLIT // 4 or n % DFT_SPLIT:
        return cos_sin(jnp.arange(n, dtype=jnp.int32))
    ca, sa = cos_sin(jnp.arange(n // DFT_SPLIT, dtype=jnp.int32) * DFT_SPLIT)
    cb, sb = cos_sin(jnp.arange(DFT_SPLIT, dtype=jnp.int32))
    ca, sa, cb, sb = ca[:, None, :], sa[:, None, :], cb[None, :, :], sb[None, :, :]
    return (ca * cb - sa * sb).reshape(n, m), (sa * cb + ca * sb).reshape(n, m)


def _seq_dft_table(n, tk):
    col = jnp.arange(n, dtype=jnp.int32)
    k = (col // (2 * tk)) * tk + col % tk
    cn, sn = _dft_tables(n, k)
    return jnp.where(((col // tk) % 2 == 0)[None, :], cn, -sn).astype(BF16)


def _fourier(p, cs, ccd2, scd2, batch, n, tk):
    half = n // 2
    nk, nkh = n // tk, half // tk
    scale = 1.0 / math.sqrt(n * F_GROUP_W)
    sb = SUBLANE_BF16
    rev = np.zeros((tk, tk + LANE), np.float32)
    rev[np.arange(1, tk), tk - np.arange(1, tk)] = 1.0
    rev[0, tk] = 1.0
    seq_blks, tk_blks = n // sb, tk // sb
    mid_blk = half // sb
    return pl.pallas_call(
        functools.partial(_fourier_kernel, scale=scale),
        grid=(batch, nkh),
        in_specs=[
            pl.BlockSpec((tk, CB), lambda b, k: (b * nk + k, COL_FU)),
            pl.BlockSpec((tk, CB), lambda b, k: (b * nk + nk - 1 - k, COL_FU)),
            pl.BlockSpec((sb, CB), lambda b, k: (b * seq_blks + jnp.minimum((nk - k) * tk_blks,
                                                                            seq_blks - 1), COL_FU)),
            pl.BlockSpec((sb, CB), lambda b, k: (b * seq_blks + mid_blk, COL_FU)),
            pl.BlockSpec((tk, tk + LANE), lambda b, k: (0, 0)),
            pl.BlockSpec((n, 2 * tk), lambda b, k: (0, k)),
            pl.BlockSpec((2 * F_W, F_W), lambda b, k: (0, 0)),
            pl.BlockSpec((2 * F_W, F_W), lambda b, k: (0, 0)),
        ],
        out_specs=pl.BlockSpec((n, F_W), lambda b, k: (b, 0)),
        out_shape=jax.ShapeDtypeStruct((batch * n, F_W), BF16),
        scratch_shapes=[pltpu.VMEM((n, F_W), F32)],
        compiler_params=_params("parallel", "arbitrary"),
        name="fourier_mix",
    )(p, p, p, p, jnp.asarray(rev, BF16), cs, ccd2, scd2)


def _softplus(x):
    return jnp.maximum(x, 0.0) + jnp.log1p(jnp.exp(-jnp.abs(x)))


def _ssd_kernel(xsf_ref, bcf_ref, dtf_ref, xsb_ref, bcb_ref, dtb_ref, alog_ref, dtbias_ref,
                e_ref, init_ref, yf_ref, yb_ref, fin_ref, state_ref):
    c = pl.program_id(1)
    q = CHUNK
    cps = xsf_ref.shape[0] // q
    gw = D_INNER // SSD_GROUPS
    hpg = SSD_HEADS // SSD_GROUPS

    @pl.when(c == 0)
    def _():
        state_ref[...] = init_ref[0]

    row = lax.broadcasted_iota(jnp.int32, (q, q), 0)
    col = lax.broadcasted_iota(jnp.int32, (q, q), 1)
    head_of_lane = lax.broadcasted_iota(jnp.int32, (q, gw), 1) // SSD_P
    neg_a = -jnp.exp(alog_ref[...])
    dtbias = dtbias_ref[...]

    def expand(x, e2):
        x1 = x.astype(BF16)
        x2 = (x - x1.astype(F32)).astype(BF16)
        return _dot(jnp.concatenate([x1, x2], axis=1), e2)

    tris = (row >= col, row <= col)
    tri_bs = tuple(t.astype(F32).astype(BF16) for t in tris)
    tri_b2s = tuple(jnp.concatenate([t, t], axis=1) for t in tri_bs)

    def one_chunk(d, rows, xs_ref, bc_ref, dt_ref, y_ref):
        fwd = d == 0
        tri, tri_b = tris[d], tri_bs[d]
        dt = _softplus(dt_ref[rows, :] + dtbias)
        a1, a2, a3 = _split3(dt * neg_a)
        acs = (_dot(tri_b2s[d], jnp.concatenate([a1, a2], axis=0))
               + _dot(tri_b, a3))
        acs_t = acs.T
        last = q - 1 if fwd else 0
        tot_row = acs[last:last + 1, :]
        e2 = e_ref[d]
        wide = expand(jnp.concatenate([dt, jnp.exp(acs), jnp.exp(tot_row - acs)], axis=0), e2)
        dt_w, eacs_w, dec_w = wide[0:q], wide[q:2 * q], wide[2 * q:3 * q]
        sdec = _dot_sel_r(jnp.broadcast_to(jnp.exp(tot_row), (SUBLANE, LANE)), e2[0:LANE])[0:1, :]
        xd = xs_ref[rows, :].astype(F32) * dt_w
        xdb = xd.astype(BF16)
        xdd = (xd * dec_w).astype(BF16)
        bc = bc_ref[rows, :]
        for g in range(SSD_GROUPS):
            bg = bc[:, g * D_STATE:(g + 1) * D_STATE]
            cg = bc[:, (SSD_GROUPS + g) * D_STATE:(SSD_GROUPS + g + 1) * D_STATE]
            cb = _dot_nt(cg, bg)
            lanes = slice(g * gw, (g + 1) * gw)
            s_g = state_ref[d, :, lanes]
            y = _dot(cg, s_g.astype(BF16)) * eacs_w[:, lanes]
            xg = xdb[:, lanes]
            zero = jnp.zeros_like(xg)
            for r in range(0, hpg, 2):
                mats, xms = [], []
                for rr in (r, r + 1):
                    jl = d * SSD_HEADS + g * hpg + rr
                    seg = jnp.where(tri, acs[:, jl:jl + 1] - acs_t[jl:jl + 1, :], -jnp.inf)
                    mats.append((cb * jnp.exp(seg)).astype(BF16))
                    xms.append(jnp.where(head_of_lane == rr, xg, zero))
                y = y + _dot(jnp.concatenate(mats, axis=1), jnp.concatenate(xms, axis=0))
            y_ref[rows, lanes] = y.astype(y_ref.dtype)
            bg_t = bg.astype(F32).T.astype(BF16)
            state_ref[d, :, lanes] = s_g * sdec[:, lanes] + _dot(bg_t, xdd[:, lanes])

    for i in range(cps):
        one_chunk(0, pl.ds(i * q, q), xsf_ref, bcf_ref, dtf_ref, yf_ref)
        one_chunk(1, pl.ds((cps - 1 - i) * q, q), xsb_ref, bcb_ref, dtb_ref, yb_ref)

    @pl.when(c == pl.num_programs(1) - 1)
    def _():
        fin_ref[0] = state_ref[...]


def _ssd(act, dt, alog, dtbias, emat2, init, batch, n, cps):
    nc = n // (CHUNK * cps)
    tr = CHUNK * cps
    fidx = lambda b, c: b * nc + c
    bidx = lambda b, c: b * nc + (nc - 1 - c)
    st_shape = (2, D_STATE, D_INNER)
    return pl.pallas_call(
        _ssd_kernel,
        grid=(batch, nc),
        in_specs=[
            pl.BlockSpec((tr, CB), lambda b, c: (fidx(b, c), 0)),
            pl.BlockSpec((tr, CB), lambda b, c: (fidx(b, c), 1)),
            pl.BlockSpec((tr, LANE), lambda b, c: (fidx(b, c), 0)),
            pl.BlockSpec((tr, CB), lambda b, c: (bidx(b, c), 0)),
            pl.BlockSpec((tr, CB), lambda b, c: (bidx(b, c), 1)),
            pl.BlockSpec((tr, LANE), lambda b, c: (bidx(b, c), 0)),
            pl.BlockSpec((1, LANE), lambda b, c: (0, 0)),
            pl.BlockSpec((1, LANE), lambda b, c: (0, 0)),
            pl.BlockSpec((2, 2 * LANE, D_INNER), lambda b, c: (0, 0, 0)),
            pl.BlockSpec((1,) + st_shape, lambda b, c: (b, 0, 0, 0)),
        ],
        out_specs=[
            pl.BlockSpec((tr, D_INNER), lambda b, c: (fidx(b, c), 0)),
            pl.BlockSpec((tr, D_INNER), lambda b, c: (bidx(b, c), 0)),
            pl.BlockSpec((1,) + st_shape, lambda b, c: (b, 0, 0, 0)),
        ],
        out_shape=[
            jax.ShapeDtypeStruct((batch * n, D_INNER), BF16),
            jax.ShapeDtypeStruct((batch * n, D_INNER), BF16),
            jax.ShapeDtypeStruct((batch,) + st_shape, F32),
        ],
        scratch_shapes=[pltpu.VMEM(st_shape, F32)],
        compiler_params=_params("parallel", "arbitrary"),
        name="ssd_scan",
    )(act, act, dt, act, act, dt, alog, dtbias, emat2, init)


def _merge_kernel(four_ref, att_ref, yf_ref, yb_ref, xs_ref, fg_ref, ag_ref, z_ref,
                  g0_ref, g1_ref, g2_ref, x_ref, mod_ref, dskip_ref, snw_ref,
                  wof_ref, woa_ref, wos_ref, wout_ref, nf_ref, o_ref, *, final_norm):
    f = lambda ref: ref[...].astype(F32)
    y_f = _dot((f(four_ref) * f(fg_ref)).astype(BF16), wof_ref[...])
    y_a = _dot((f(att_ref) * f(ag_ref)).astype(BF16), woa_ref[...])
    ys = f(yf_ref) + f(yb_ref) + dskip_ref[...] * f(xs_ref)
    t = ys * f(z_ref)
    t = t * lax.rsqrt(jnp.mean(t * t, axis=-1, keepdims=True) + EPS) * snw_ref[...]
    y_s = _dot(t.astype(BF16), wos_ref[...])
    y = f(g0_ref) * y_f + f(g1_ref) * y_a + f(g2_ref) * y_s
    out = _dot(y.astype(BF16), wout_ref[...])
    xn = x_ref[...] + mod_ref[0][:, 2 * D_MODEL:3 * D_MODEL] * out
    if final_norm:
        xn = xn * lax.rsqrt(jnp.mean(xn * xn, axis=-1, keepdims=True) + EPS) * nf_ref[...]
    o_ref[...] = xn


def _merge(four, att, yf, yb, p, x2, mod, dskip_w, snw, wof, woa, wos, wout, norm_f,
           n, tm, final_norm):
    rows = x2.shape[0]
    tiles_per_b = n // tm
    gcol = COL_GATES * CB // D_MODEL
    row_blk = lambda w, cidx: pl.BlockSpec((tm, w), lambda i: (i, cidx))
    const = lambda shape: pl.BlockSpec(shape, lambda i: (0,) * len(shape))
    return pl.pallas_call(
        functools.partial(_merge_kernel, final_norm=final_norm),
        grid=(rows // tm,),
        in_specs=[
            row_blk(F_W, 0), row_blk(CB, 0), row_blk(D_INNER, 0), row_blk(D_INNER, 0),
            row_blk(CB, 0),
            row_blk(CB, COL_FG), row_blk(CB, COL_AG), row_blk(CB, COL_Z),
            row_blk(D_MODEL, gcol), row_blk(D_MODEL, gcol + 1), row_blk(D_MODEL, gcol + 2),
            row_blk(D_MODEL, 0),
            pl.BlockSpec((1, 1, 3 * D_MODEL), lambda i: (i // tiles_per_b, 0, 0)),
            const((1, D_INNER)), const((1, D_INNER)),
            const((F_W, D_MODEL)), const((CB, D_MODEL)), const((D_INNER, D_MODEL)),
            const((D_MODEL, D_MODEL)), const((1, D_MODEL)),
        ],
        out_specs=pl.BlockSpec((tm, D_MODEL), lambda i: (i, 0)),
        out_shape=jax.ShapeDtypeStruct((rows, D_MODEL), F32),
        compiler_params=_params("parallel"),
        name="branch_merge",
    )(four, att, yf, yb, p, p, p, p, p, p, p, x2, mod, dskip_w, snw, wof, woa, wos, wout, norm_f)


def _rope_tables(n):
    rows = n // GRID_W
    row = jnp.repeat(jnp.arange(rows, dtype=F32), GRID_W)
    col = jnp.tile(jnp.arange(GRID_W, dtype=F32), rows)
    freqs = ROPE_BASE ** (-jnp.arange(0, ROPE_AXIS, 2, dtype=F32) / ROPE_AXIS)
    ang_r = row[:, None] * freqs
    ang_c = col[:, None] * freqs
    ang = jnp.concatenate([ang_r, ang_r, ang_c, ang_c], axis=-1)
    reps = CB // ATT_QK
    return jnp.tile(jnp.cos(ang), (1, reps)), jnp.tile(jnp.sin(ang), (1, reps))


def _head_expanders():
    e = np.zeros((2, LANE, D_INNER), np.float32)
    for d in range(2):
        for h in range(SSD_HEADS):
            e[d, d * SSD_HEADS + h, h * SSD_P:(h + 1) * SSD_P] = 1.0
    return jnp.asarray(np.concatenate([e, e], axis=1), BF16)


def _group_dft(w):
    cw, sw = _dft_tables(F_GROUP_W)
    eye = jnp.eye(w // F_GROUP_W, dtype=F32)
    cc, sc = jnp.kron(eye, cw), jnp.kron(eye, sw)
    return (jnp.concatenate([cc, cc], axis=0).astype(BF16),
            jnp.concatenate([sc, -sc], axis=0).astype(BF16))


def kernel(x, c, ctx, c_ctx, w_mod, b_mod, norm_w, w_in, conv_w, conv_b, a_log, dt_bias, d_skip,
           ssd_norm_w, lam, subln_w, w_of, w_oa, w_os, w_out, norm_f):
    batch, n, _ = x.shape
    n_ctx = ctx.shape[1]
    depth = w_mod.shape[0]
    assert n % GRID_W == 0 and n % CHUNK == 0 and n_ctx % CHUNK == 0

    o_q = 2 * F_W
    o_ag = o_q + 3 * CB
    o_xbc = o_ag + CB + D_INNER
    o_dt = o_xbc + XBC_W
    o_gt = o_dt + DT_W
    w_main = jnp.concatenate([w_in[:, :, o_xbc:o_dt], w_in[:, :, o_gt:], w_in[:, :, :o_q],
                              w_in[:, :, o_ag:o_xbc], w_in[:, :, o_q:o_ag]], axis=-1).astype(BF16)
    w_dt = jnp.pad(w_in[:, :, o_dt:o_gt], ((0, 0), (0, 0), (0, LANE - DT_W))).astype(BF16)
    conv_w8 = jnp.pad(conv_w, ((0, 0), (0, SUBLANE - CONV_W), (0, 0)))
    pad_lanes = lambda a: jnp.pad(a.reshape(depth, 1, DT_W), ((0, 0), (0, 0), (0, LANE - DT_W)))
    alog_p, dtbias_p = pad_lanes(a_log), pad_lanes(dt_bias)
    dskip_w = jnp.repeat(d_skip, SSD_P, axis=-1).reshape(depth, 1, D_INNER)
    wof_b, woa_b, wos_b, wout_b = (w.astype(BF16) for w in (w_of, w_oa, w_os, w_out))
    lam_inits = [0.8 - 0.6 * math.exp(-0.3 * l) for l in range(depth)]
    linit = jnp.asarray(np.broadcast_to(np.asarray(lam_inits, np.float32)[:, None, None],
                                        (depth, SUBLANE, LANE)))

    cos_t, sin_t = _rope_tables(n)
    tf_l, tf_c = math.gcd(n // 2, TK_FOURIER), math.gcd(n_ctx // 2, TK_FOURIER)
    cs_l = _seq_dft_table(n, tf_l)
    cs_c = _seq_dft_table(n_ctx, tf_c)
    ccd, scd = _group_dft(F_W)
    emat = _head_expanders()
    zero_state = jnp.zeros((batch, 2, D_STATE, D_INNER), F32)

    mod_rows = -(-(batch + 1) // SUBLANE) * SUBLANE
    cc = jnp.concatenate([c, c_ctx[None, :], jnp.zeros((mod_rows - batch - 1, D_MODEL), F32)], axis=0)
    mod_all, lam_all = _modulation(cc, w_mod, b_mod.reshape(depth, 1, 3 * D_MODEL), lam, linit)

    tm_l, tm_c = math.gcd(n, TM_PROJ), math.gcd(n_ctx, TM_PROJ)
    tq_l, tk_l = math.gcd(n, TQ_ATT), math.gcd(n, TK_ATT)
    tmerge_l = math.gcd(n, TM_MERGE)
    rows_c = batch * n_ctx
    cps_l = math.gcd(n // CHUNK, SSD_CHUNKS_PER_STEP)
    cps_c = math.gcd(n_ctx // CHUNK, SSD_CHUNKS_PER_STEP)
    xl = x.reshape(batch * n, D_MODEL)
    xc = ctx.reshape(batch * n_ctx, D_MODEL)
    for l in range(depth):
        last = l == depth - 1
        mod_l = mod_all[l, :batch].reshape(batch, 1, 3 * D_MODEL)
        mod_c = mod_all[l, batch].reshape(1, 1, 3 * D_MODEL)
        nw = norm_w[l].reshape(1, D_MODEL)
        cb_ = conv_b[l].reshape(1, XBC_W)
        p_c, dt_c = _inproj(xc, mod_c, nw, w_main[l], w_dt[l], conv_w8[l], cb_, None, None,
                            rows_c, n_ctx, tm_c)
        p_l, dt_l = _inproj(xl, mod_l, nw, w_main[l], w_dt[l], conv_w8[l], cb_, cos_t, sin_t,
                            n, n, tm_l)
        yf_c, yb_c, st_c = _ssd(p_c, dt_c, alog_p[l], dtbias_p[l], emat, zero_state, batch, n_ctx,
                                cps_c)
        yf_l, yb_l, _ = _ssd(p_l, dt_l, alog_p[l], dtbias_p[l], emat, st_c, batch, n, cps_l)
        sw = subln_w[l].reshape(1, ATT_V)
        att_l = _attention(p_l, p_c, p_l, lam_all[l], sw, lam_inits[l], batch, n, n_ctx, n,
                           tq_l, tk_l)
        four_l = _fourier(p_l, cs_l, ccd, scd, batch, n, tf_l)
        merge_w = (dskip_w[l], ssd_norm_w[l].reshape(1, D_INNER), wof_b[l], woa_b[l], wos_b[l],
                   wout_b[l], norm_f.reshape(1, D_MODEL))
        if not last:
            att_c = _attention(p_c, p_c, None, lam_all[l], sw, lam_inits[l], batch, n_ctx, n_ctx, 0,
                               n_ctx, 0)
            four_c = _fourier(p_c, cs_c, ccd, scd, batch, n_ctx, tf_c)
            xc = _merge(four_c, att_c, yf_c, yb_c, p_c, xc, mod_c, *merge_w, rows_c, n_ctx, False)
        xl = _merge(four_l, att_l, yf_l, yb_l, p_l, xl, mod_l, *merge_w, n, tmerge_l, last)
    return xl.reshape(batch, n, D_MODEL)
```
